```python
import math, functools
import jax, jax.numpy as jnp
from jax import lax
import numpy as np

D_MODEL = 1024
BATCH = 8
SEQ = 2048
DEPTH = 2
DEC_BATCH = 128
DEC_SEQ = 1
PAST_LEN = 16384
PAGE_SIZE = 128

N_EVEN = (DEPTH + 1) // 2
N_ODD = DEPTH // 2
NORM_EPS = 1e-5
CHUNK = 128
CONV_W = 4
SSD_D_INNER = D_MODEL
SSD_HEAD_DIM = 64
SSD_HEADS = SSD_D_INNER // SSD_HEAD_DIM
SSD_GROUPS = 2
SSD_D_STATE = 128
MLSTM_D_INNER = D_MODEL
MLSTM_HEADS = 4
MLSTM_HEAD_DIM = MLSTM_D_INNER // MLSTM_HEADS
QKV_BLOCK = 4
CONV_DIM = SSD_D_INNER + 2 * SSD_GROUPS * SSD_D_STATE + MLSTM_D_INNER
IN0_DIM = SSD_D_INNER + CONV_DIM + SSD_HEADS + MLSTM_D_INNER + 2 * MLSTM_HEADS
MIX0_DIM = SSD_D_INNER + MLSTM_D_INNER
RWKV_HEAD_DIM = 64
RWKV_HEADS = D_MODEL // RWKV_HEAD_DIM
DECAY_LORA = 64
AAA_LORA = 64
GATE_LORA = 160
RWKV_LN_EPS = 64e-5
D_FF = -(-8 * D_MODEL // (3 * 256)) * 256

kernel_name = 'hybrid_ssd_mlstm_rwkv7_decode_step'


def _rms(x, g):
    xf = x.astype(jnp.float32)
    y = xf * lax.rsqrt(jnp.mean(xf * xf, axis=-1, keepdims=True) + NORM_EPS)
    return y.astype(x.dtype) * g


def _group_rms(x, groups, g):
    shp = x.shape
    xf = x.astype(jnp.float32).reshape(shp[:-1] + (groups, shp[-1] // groups))
    y = xf * lax.rsqrt(jnp.mean(xf * xf, axis=-1, keepdims=True) + NORM_EPS)
    return y.reshape(shp).astype(x.dtype) * g


def _chunk_len(t):
    return CHUNK if t % CHUNK == 0 else t


def _to_chunks(a, length):
    b, t = a.shape[:2]
    a = a.astype(jnp.float32).reshape((b, t // length, length) + a.shape[2:])
    return jnp.moveaxis(a, 1, 0)


def _from_chunks(a):
    a = jnp.moveaxis(a, 0, 1)
    return a.reshape((a.shape[0], a.shape[1] * a.shape[2]) + a.shape[3:])


def _causal_conv(u, buf, w, bias):
    t = u.shape[1]
    ext = jnp.concatenate([buf.astype(u.dtype), u], axis=1)
    out = bias
    for j in range(CONV_W):
        out = out + ext[:, j:j + t] * w[j]
    return out, ext[:, t:]


def _blockdiag(x, w):
    shp = x.shape
    xb = x.reshape(shp[:-1] + (shp[-1] // QKV_BLOCK, QKV_BLOCK))
    return jnp.einsum('btnc,ncd->btnd', xb, w).reshape(shp)


def _ssd_scan(x, dt, a_neg, bm, cm, h0):
    b, t, h, p = x.shape
    g, n = bm.shape[2], bm.shape[3]
    hg = h // g
    length = _chunk_len(t)
    mask = jnp.tril(jnp.ones((length, length), dtype=bool))[None, :, :, None, None]
    a_g = a_neg.astype(jnp.float32).reshape(g, hg)

    def step(state, inp):
        xc, dtc, bc, cc = inp
        acum = jnp.cumsum(dtc * a_g, axis=1)
        seg = jnp.where(mask, acum[:, :, None] - acum[:, None], -jnp.inf)
        cb = jnp.einsum('btgn,bsgn->btsg', cc, bc)
        wts = jnp.exp(seg) * cb[..., None] * dtc[:, None]
        y = jnp.einsum('btsgh,bsghp->btghp', wts, xc)
        y = y + jnp.exp(acum)[..., None] * jnp.einsum('btgn,bghpn->btghp', cc, state)
        w_end = jnp.exp(acum[:, -1:] - acum) * dtc
        state = (jnp.exp(acum[:, -1])[..., None, None] * state
                 + jnp.einsum('bsghp,bsgn->bghpn', w_end[..., None] * xc, bc))
        return state, y

    xs = _to_chunks(x.reshape(b, t, g, hg, p), length)
    dts = _to_chunks(dt.reshape(b, t, g, hg), length)
    bs = _to_chunks(bm, length)
    cs = _to_chunks(cm, length)
    state, ys = lax.scan(step, h0.astype(jnp.float32).reshape(b, g, hg, p, n), (xs, dts, bs, cs))
    return _from_chunks(ys).reshape(b, t, h, p), state.reshape(b, h, p, n)


def _mlstm_scan(q, k, v, logi, logf, c0, n0, m0):
    b, t = q.shape[:2]
    length = _chunk_len(t)
    mask = jnp.tril(jnp.ones((length, length), dtype=bool))[None, :, :, None]

    def step(carry, inp):
        c, n, m = carry
        qc, kc, vc, lic, lfc = inp
        bcum = jnp.cumsum(lfc, axis=1)
        dlog = jnp.where(mask, bcum[:, :, None] - bcum[:, None] + lic[:, None], -jnp.inf)
        inter = bcum + m[:, None]
        m_t = jnp.maximum(inter, jnp.max(dlog, axis=2))
        s = jnp.einsum('bthd,bshd->btsh', qc, kc) * jnp.exp(dlog - m_t[:, :, None])
        w_inter = jnp.exp(inter - m_t)
        num = (jnp.einsum('btsh,bshv->bthv', s, vc)
               + w_inter[..., None] * jnp.einsum('bthd,bhdv->bthv', qc, c))
        den = jnp.sum(s, axis=2) + w_inter * jnp.einsum('bthd,bhd->bth', qc, n)
        h = num / jnp.maximum(jnp.abs(den), jnp.exp(-m_t))[..., None]
        b_end = bcum[:, -1]
        wlog = b_end[:, None] - bcum + lic
        m_new = jnp.maximum(b_end + m, jnp.max(wlog, axis=1))
        ws = jnp.exp(wlog - m_new[:, None])
        dc = jnp.exp(b_end + m - m_new)
        c = dc[..., None, None] * c + jnp.einsum('bshd,bshv->bhdv', ws[..., None] * kc, vc)
        n = dc[..., None] * n + jnp.einsum('bsh,bshd->bhd', ws, kc)
        return (c, n, m_new), h

    seq = (_to_chunks(q, length), _to_chunks(k, length), _to_chunks(v, length),
           _to_chunks(logi, length), _to_chunks(logf, length))
    init = (c0.astype(jnp.float32), n0.astype(jnp.float32), m0.astype(jnp.float32))
    (c, n, m), hs = lax.scan(step, init, seq)
    return _from_chunks(hs), (c, n, m)


def _rwkv7_scan(r, decay, k, v, kk, a, s0):
    def step(s, inp):
        r_t, w_t, k_t, v_t, kk_t, a_t = inp
        sa = jnp.einsum('bhvk,bhk->bhv', s, -kk_t)
        s = (s * w_t[:, :, None, :] + sa[..., None] * (kk_t * a_t)[:, :, None, :]
             + v_t[..., None] * k_t[:, :, None, :])
        return s, jnp.einsum('bhvk,bhk->bhv', s, r_t)

    seq = tuple(jnp.moveaxis(z.astype(jnp.float32), 1, 0) for z in (r, decay, k, v, kk, a))
    s, ys = lax.scan(step, s0.astype(jnp.float32), seq)
    return jnp.moveaxis(ys, 0, 1), s


def _ssd_mlstm_mixer(xn, conv_buf, ssm_h, mc, mn, mm, p, i):
    b, t, _ = xn.shape
    dty = xn.dtype
    proj = xn @ p['w_in0'][i]
    s1 = SSD_D_INNER
    s2 = s1 + CONV_DIM
    s3 = s2 + SSD_HEADS
    s4 = s3 + MLSTM_D_INNER
    s5 = s4 + MLSTM_HEADS
    z_ssd, conv_in, dt_pre, o_pre, i_pre, f_pre = jnp.split(proj, [s1, s2, s3, s4, s5], axis=-1)
    conv_out, new_conv = _causal_conv(conv_in, conv_buf, p['conv_w'][i], p['conv_b'][i])
    conv_act = jax.nn.silu(conv_out)
    gn = SSD_GROUPS * SSD_D_STATE
    xs, bm, cm, xm_act = jnp.split(conv_act, [SSD_D_INNER, SSD_D_INNER + gn, SSD_D_INNER + 2 * gn], axis=-1)
    xm_raw = conv_in[..., SSD_D_INNER + 2 * gn:]
    dt = jax.nn.softplus((dt_pre + p['ssd_dt_bias'][i]).astype(jnp.float32))
    a_neg = -jnp.exp(p['ssd_a_log'][i].astype(jnp.float32))
    xh = xs.reshape(b, t, SSD_HEADS, SSD_HEAD_DIM)
    y, new_h = _ssd_scan(xh, dt, a_neg, bm.reshape(b, t, SSD_GROUPS, SSD_D_STATE),
                         cm.reshape(b, t, SSD_GROUPS, SSD_D_STATE), ssm_h)
    y = y.astype(dty) + p['ssd_d'][i][:, None] * xh
    y = _group_rms(y.reshape(b, t, SSD_D_INNER) * jax.nn.silu(z_ssd), SSD_GROUPS, p['ssd_norm'][i])
    hs = (b, t, MLSTM_HEADS, MLSTM_HEAD_DIM)
    q = _blockdiag(xm_act, p['ml_wq'][i]).reshape(hs)
    k = _blockdiag(xm_act, p['ml_wk'][i]).reshape(hs) * (MLSTM_HEAD_DIM ** -0.5)
    v = _blockdiag(xm_raw, p['ml_wv'][i]).reshape(hs)
    logi = (i_pre + p['ml_i_bias'][i]).astype(jnp.float32)
    logf = jax.nn.log_sigmoid((f_pre + p['ml_f_bias'][i]).astype(jnp.float32))
    hm, (new_c, new_n, new_m) = _mlstm_scan(q, k, v, logi, logf, mc, mn, mm)
    hm = _group_rms(hm.astype(dty).reshape(b, t, MLSTM_D_INNER), MLSTM_HEADS, p['ml_norm'][i])
    hm = (hm + p['ml_skip'][i] * xm_act) * jax.nn.sigmoid(o_pre)
    out = jnp.concatenate([y, hm], axis=-1) @ p['w_out0'][i]
    return out, (new_conv, new_h, new_c, new_n, new_m)


def _rwkv7_mixer(xn, shift, wkv, p, j):
    b, t, d = xn.shape
    hs = (b, t, RWKV_HEADS, RWKV_HEAD_DIM)
    x_prev = jnp.concatenate([shift[:, None].astype(xn.dtype), xn[:, :-1]], axis=1)
    xx = x_prev - xn
    mu = p['rw_mu'][j]
    xr, xw, xk, xv, xa, xg = (xn + xx * mu[c] for c in range(6))
    r = xr @ p['rw_wr'][j]
    w = -jax.nn.softplus(-(p['rw_w0'][j] + jnp.tanh(xw @ p['rw_w1'][j]) @ p['rw_w2'][j])) - 0.5
    k = xk @ p['rw_wk'][j]
    v = xv @ p['rw_wv'][j]
    a = jax.nn.sigmoid(p['rw_a0'][j] + (xa @ p['rw_a1'][j]) @ p['rw_a2'][j])
    g = jax.nn.sigmoid(xg @ p['rw_g1'][j]) @ p['rw_g2'][j]
    kk = (k * p['rw_k_k'][j]).reshape(hs).astype(jnp.float32)
    kk = kk / jnp.maximum(jnp.sqrt(jnp.sum(kk * kk, axis=-1, keepdims=True)), 1e-12)
    k = k * (1 + (a - 1) * p['rw_k_a'][j])
    decay = jnp.exp(-jnp.exp(w.astype(jnp.float32)))
    y, new_s = _rwkv7_scan(r.reshape(hs), decay.reshape(hs), k.reshape(hs), v.reshape(hs),
                           kk, a.reshape(hs), wkv)
    mean = jnp.mean(y, axis=-1, keepdims=True)
    var = jnp.mean(jnp.square(y - mean), axis=-1, keepdims=True)
    y = ((y - mean) * lax.rsqrt(var + RWKV_LN_EPS)).reshape(b, t, d).astype(xn.dtype)
    y = y * p['rw_ln_w'][j] + p['rw_ln_b'][j]
    bonus = jnp.sum(r.reshape(hs) * k.reshape(hs) * p['rw_r_k'][j].reshape(RWKV_HEADS, RWKV_HEAD_DIM),
                    axis=-1, keepdims=True) * v.reshape(hs)
    y = (y + bonus.reshape(b, t, d)) * g
    return y @ p['rw_wo'][j], (xn[:, -1], new_s)


def _swiglu(x, w_gu, w_down):
    gate, up = jnp.split(x @ w_gu, 2, axis=-1)
    return (jax.nn.silu(gate) * up) @ w_down


def _trunk(x, conv, ssm, mc, mn, mm, shift, wkv, p):
    even_names = ('conv', 'ssm', 'mc', 'mn', 'mm')
    odd_names = ('shift', 'wkv')
    outs = {name: [] for name in even_names + odd_names}
    for layer in range(DEPTH):
        xn = _rms(x, p['norm_mix'][layer])
        if layer % 2 == 0:
            i = layer // 2
            mix, st = _ssd_mlstm_mixer(xn, conv[i], ssm[i], mc[i], mn[i], mm[i], p, i)
            for name, s in zip(even_names, st):
                outs[name].append(s)
        else:
            j = layer // 2
            mix, st = _rwkv7_mixer(xn, shift[j], wkv[j], p, j)
            for name, s in zip(odd_names, st):
                outs[name].append(s)
        x = x + mix.astype(x.dtype)
        x = x + _swiglu(_rms(x, p['norm_ffn'][layer]), p['ffn_w_gate_up'][layer], p['ffn_w_down'][layer])
    y = _rms(x, p['norm_final'])
    return y, tuple(jnp.stack(outs[name]) for name in even_names + odd_names)


def setup_inputs(seed: int = 0) -> dict:
    key = jax.random.key(seed)
    ks = iter(jax.random.split(key, 64))

    def nrm(shape, scale):
        return scale * jax.random.normal(next(ks), shape, jnp.float32)

    d = D_MODEL
    x_prompt = nrm((BATCH, SEQ, d), 1.0)
    x_sample = nrm((DEC_BATCH, DEC_SEQ, d), 1.0)
    state_conv = nrm((N_EVEN, DEC_BATCH, CONV_W - 1, CONV_DIM), 1.0)
    state_ssm = nrm((N_EVEN, DEC_BATCH, SSD_HEADS, SSD_HEAD_DIM, SSD_D_STATE), 0.1)
    state_mlstm_c = nrm((N_EVEN, DEC_BATCH, MLSTM_HEADS, MLSTM_HEAD_DIM, MLSTM_HEAD_DIM), 0.1)
    state_mlstm_n = nrm((N_EVEN, DEC_BATCH, MLSTM_HEADS, MLSTM_HEAD_DIM), 0.1)
    state_mlstm_m = nrm((N_EVEN, DEC_BATCH, MLSTM_HEADS), 1.0)
    state_shift = nrm((N_ODD, DEC_BATCH, d), 1.0)
    state_wkv = nrm((N_ODD, DEC_BATCH, RWKV_HEADS, RWKV_HEAD_DIM, RWKV_HEAD_DIM), 0.1)

    norm_mix = 1.0 + nrm((DEPTH, d), 0.02)
    norm_ffn = 1.0 + nrm((DEPTH, d), 0.02)
    norm_final = 1.0 + nrm((d,), 0.02)
    w_in0 = nrm((N_EVEN, d, IN0_DIM), d ** -0.5)
    conv_w = nrm((N_EVEN, CONV_W, CONV_DIM), 0.5)
    conv_b = nrm((N_EVEN, CONV_DIM), 0.02)
    dt0 = jnp.exp(jax.random.uniform(next(ks), (N_EVEN, SSD_HEADS), jnp.float32,
                                     minval=math.log(1e-3), maxval=math.log(1e-1)))
    ssd_dt_bias = dt0 + jnp.log(-jnp.expm1(-dt0))
    ssd_a_log = jnp.log(jax.random.uniform(next(ks), (N_EVEN, SSD_HEADS), jnp.float32, minval=1.0, maxval=16.0))
    ssd_d = 1.0 + nrm((N_EVEN, SSD_HEADS), 0.1)
    ssd_norm = 1.0 + nrm((N_EVEN, SSD_D_INNER), 0.02)
    nb = MLSTM_D_INNER // QKV_BLOCK
    ml_wq = nrm((N_EVEN, nb, QKV_BLOCK, QKV_BLOCK), QKV_BLOCK ** -0.5)
    ml_wk = nrm((N_EVEN, nb, QKV_BLOCK, QKV_BLOCK), QKV_BLOCK ** -0.5)
    ml_wv = nrm((N_EVEN, nb, QKV_BLOCK, QKV_BLOCK), QKV_BLOCK ** -0.5)
    ml_i_bias = nrm((N_EVEN, MLSTM_HEADS), 0.1)
    ml_f_bias = jnp.linspace(3.0, 6.0, MLSTM_HEADS)[None, :] + nrm((N_EVEN, MLSTM_HEADS), 0.1)
    ml_norm = 1.0 + nrm((N_EVEN, MLSTM_D_INNER), 0.02)
    ml_skip = 1.0 + nrm((N_EVEN, MLSTM_D_INNER), 0.02)
    w_out0 = nrm((N_EVEN, MIX0_DIM, d), MIX0_DIM ** -0.5)

    rw_mu = jax.random.uniform(next(ks), (N_ODD, 6, d), jnp.float32)
    rw_wr = nrm((N_ODD, d, d), d ** -0.5)
    rw_wk = nrm((N_ODD, d, d), d ** -0.5)
    rw_wv = nrm((N_ODD, d, d), d ** -0.5)
    rw_wo = nrm((N_ODD, d, d), d ** -0.5)
    rw_w0 = jnp.linspace(-6.0, 1.0, d)[None, :] + nrm((N_ODD, d), 0.1)
    rw_w1 = nrm((N_ODD, d, DECAY_LORA), d ** -0.5)
    rw_w2 = nrm((N_ODD, DECAY_LORA, d), 0.1 * DECAY_LORA ** -0.5)
    rw_a0 = nrm((N_ODD, d), 0.1)
    rw_a1 = nrm((N_ODD, d, AAA_LORA), d ** -0.5)
    rw_a2 = nrm((N_ODD, AAA_LORA, d), 0.1 * AAA_LORA ** -0.5)
    rw_g1 = nrm((N_ODD, d, GATE_LORA), d ** -0.5)
    rw_g2 = nrm((N_ODD, GATE_LORA, d), GATE_LORA ** -0.5)
    rw_k_k = 0.85 + nrm((N_ODD, d), 0.02)
    rw_k_a = 1.0 + nrm((N_ODD, d), 0.02)
    rw_r_k = nrm((N_ODD, d), 0.1)
    rw_ln_w = 1.0 + nrm((N_ODD, d), 0.02)
    rw_ln_b = nrm((N_ODD, d), 0.02)
    ffn_w_gate_up = nrm((DEPTH, d, 2 * D_FF), d ** -0.5)
    ffn_w_down = nrm((DEPTH, D_FF, d), D_FF ** -0.5)
    return {'x_prompt': x_prompt, 'x_sample': x_sample,
            'state_conv': state_conv, 'state_ssm': state_ssm, 'state_mlstm_c': state_mlstm_c,
            'state_mlstm_n': state_mlstm_n, 'state_mlstm_m': state_mlstm_m,
            'state_shift': state_shift, 'state_wkv': state_wkv,
            'norm_mix': norm_mix, 'norm_ffn': norm_ffn, 'norm_final': norm_final,
            'w_in0': w_in0, 'conv_w': conv_w, 'conv_b': conv_b, 'ssd_dt_bias': ssd_dt_bias,
            'ssd_a_log': ssd_a_log, 'ssd_d': ssd_d, 'ssd_norm': ssd_norm,
            'ml_wq': ml_wq, 'ml_wk': ml_wk, 'ml_wv': ml_wv, 'ml_i_bias': ml_i_bias, 'ml_f_bias': ml_f_bias,
            'ml_norm': ml_norm, 'ml_skip': ml_skip, 'w_out0': w_out0,
            'rw_mu': rw_mu, 'rw_wr': rw_wr, 'rw_wk': rw_wk, 'rw_wv': rw_wv, 'rw_wo': rw_wo,
            'rw_w0': rw_w0, 'rw_w1': rw_w1, 'rw_w2': rw_w2, 'rw_a0': rw_a0, 'rw_a1': rw_a1, 'rw_a2': rw_a2,
            'rw_g1': rw_g1, 'rw_g2': rw_g2, 'rw_k_k': rw_k_k, 'rw_k_a': rw_k_a, 'rw_r_k': rw_r_k,
            'rw_ln_w': rw_ln_w, 'rw_ln_b': rw_ln_b,
            'ffn_w_gate_up': ffn_w_gate_up, 'ffn_w_down': ffn_w_down}


def reference(x_prompt, x_sample, state_conv, state_ssm, state_mlstm_c, state_mlstm_n, state_mlstm_m,
              state_shift, state_wkv, norm_mix, norm_ffn, norm_final, w_in0, conv_w, conv_b,
              ssd_dt_bias, ssd_a_log, ssd_d, ssd_norm, ml_wq, ml_wk, ml_wv, ml_i_bias, ml_f_bias,
              ml_norm, ml_skip, w_out0, rw_mu, rw_wr, rw_wk, rw_wv, rw_wo, rw_w0, rw_w1, rw_w2,
              rw_a0, rw_a1, rw_a2, rw_g1, rw_g2, rw_k_k, rw_k_a, rw_r_k, rw_ln_w, rw_ln_b,
              ffn_w_gate_up, ffn_w_down):
    p = dict(norm_mix=norm_mix, norm_ffn=norm_ffn, norm_final=norm_final, w_in0=w_in0,
             conv_w=conv_w, conv_b=conv_b, ssd_dt_bias=ssd_dt_bias, ssd_a_log=ssd_a_log,
             ssd_d=ssd_d, ssd_norm=ssd_norm, ml_wq=ml_wq, ml_wk=ml_wk, ml_wv=ml_wv,
             ml_i_bias=ml_i_bias, ml_f_bias=ml_f_bias, ml_norm=ml_norm, ml_skip=ml_skip,
             w_out0=w_out0, rw_mu=rw_mu, rw_wr=rw_wr, rw_wk=rw_wk, rw_wv=rw_wv, rw_wo=rw_wo,
             rw_w0=rw_w0, rw_w1=rw_w1, rw_w2=rw_w2, rw_a0=rw_a0, rw_a1=rw_a1, rw_a2=rw_a2,
             rw_g1=rw_g1, rw_g2=rw_g2, rw_k_k=rw_k_k, rw_k_a=rw_k_a, rw_r_k=rw_r_k,
             rw_ln_w=rw_ln_w, rw_ln_b=rw_ln_b, ffn_w_gate_up=ffn_w_gate_up, ffn_w_down=ffn_w_down)
    bp = x_prompt.shape[0]
    zeros = functools.partial(jnp.zeros, dtype=x_prompt.dtype)
    y_prompt, (conv_p, ssm_p, mlstm_c_p, mlstm_n_p, mlstm_m_p, shift_p, wkv_p) = _trunk(
        x_prompt,
        zeros((N_EVEN, bp) + state_conv.shape[2:]),
        zeros((N_EVEN, bp) + state_ssm.shape[2:]),
        zeros((N_EVEN, bp) + state_mlstm_c.shape[2:]),
        zeros((N_EVEN, bp) + state_mlstm_n.shape[2:]),
        zeros((N_EVEN, bp) + state_mlstm_m.shape[2:]),
        zeros((N_ODD, bp) + state_shift.shape[2:]),
        zeros((N_ODD, bp) + state_wkv.shape[2:]),
        p)
    y_sample, (conv_s, ssm_s, mlstm_c_s, mlstm_n_s, mlstm_m_s, shift_s, wkv_s) = _trunk(
        x_sample, state_conv, state_ssm, state_mlstm_c, state_mlstm_n, state_mlstm_m,
        state_shift, state_wkv, p)
    return (y_prompt, y_sample, conv_p, conv_s, ssm_p, ssm_s, mlstm_c_p, mlstm_c_s,
            mlstm_n_p, mlstm_n_s, mlstm_m_p, mlstm_m_s, shift_p, shift_s, wkv_p, wkv_s)
```

```python
import functools

import jax
import jax.numpy as jnp
from jax import lax
from jax.experimental import pallas as pl
from jax.experimental.pallas import tpu as pltpu

F32 = jnp.float32
BF16 = jnp.bfloat16

D_MODEL = 1024
NORM_EPS = 1e-5
CONV_W = 4
SSD_HEADS = 16
SSD_HEAD_DIM = 64
SSD_GROUPS = 2
SSD_D_STATE = 128
SSD_D_INNER = SSD_HEADS * SSD_HEAD_DIM
XBC_DIM = SSD_D_INNER + 2 * SSD_GROUPS * SSD_D_STATE
MLSTM_HEADS = 4
MLSTM_HEAD_DIM = 256
MLSTM_D_INNER = MLSTM_HEADS * MLSTM_HEAD_DIM
QKV_BLOCK = 4
RWKV_HEADS = 16
RWKV_HEAD_DIM = 64
RWKV_LN_EPS = 64e-5
D_FF = 2816
FF_CHUNK = 256
LANES = 128
SUBLANES = 8
SMALL_DT, SMALL_I, SMALL_F = 0, 16, 20
NEG = -1e30
SCAN_CHUNK = 128
ROW_TILE = 256
VMEM_LIMIT = 56 * 1024 * 1024


def _dot(a, b):
    return jnp.dot(a.astype(BF16), b.astype(BF16), preferred_element_type=F32)


def _dot_nt(a, b):
    return lax.dot_general(a.astype(BF16), b.astype(BF16), (((1,), (1,)), ((), ())),
                           preferred_element_type=F32)


def _dot_tn(a, b):
    return lax.dot_general(a.astype(BF16), b.astype(BF16), (((0,), (0,)), ((), ())),
                           preferred_element_type=F32)


def _split(x, terms):
    parts = []
    rem = x
    for _ in range(terms):
        p = rem.astype(BF16)
        parts.append(p)
        rem = rem - p.astype(F32)
    return parts


def _dot_exact_rhs(x, m, terms=2):
    out = None
    for p in _split(x, terms):
        t = jnp.dot(p, m, preferred_element_type=F32)
        out = t if out is None else out + t
    return out


def _tri_cumsum(tri, x):
    out = None
    for p in _split(x, 3):
        t = jnp.dot(tri, p, preferred_element_type=F32)
        out = t if out is None else out + t
    return out


def _rms_rows(x, g):
    return x * lax.rsqrt(jnp.mean(x * x, axis=-1, keepdims=True) + NORM_EPS) * g


def _sigmoid(x):
    return 1.0 / (1.0 + jnp.exp(-x))


def _silu(x):
    return x * _sigmoid(x)


def _softplus(x):
    return jnp.maximum(x, 0.0) + jnp.log(1.0 + jnp.exp(-jnp.abs(x)))


def _pad_rows(x, rows):
    if x.shape[0] == rows:
        return x
    return jnp.concatenate([x, jnp.zeros((rows - x.shape[0], x.shape[1]), x.dtype)], axis=0)


def _tri_masks(n):
    row = lax.broadcasted_iota(jnp.int32, (n, n), 0)
    col = lax.broadcasted_iota(jnp.int32, (n, n), 1)
    return row >= col, row > col


def _causal_conv_silu(ext, u, conv0_ref, cw_ref, cb_ref, first, rows):
    @pl.when(first)
    def _():
        ext[0:SUBLANES, :] = conv0_ref[...]

    @pl.when(jnp.logical_not(first))
    def _():
        ext[0:SUBLANES, :] = ext[rows:rows + SUBLANES, :]

    ext[SUBLANES:SUBLANES + rows, :] = u
    conv = cb_ref[...]
    base = SUBLANES - (CONV_W - 1)
    for j in range(CONV_W):
        conv = conv + ext[base + j:base + j + rows, :] * cw_ref[j:j + 1, :]
    return conv


def _in0_kernel(x_ref, g_ref, w_ref, z_ref, xbc_ref, xm_ref, o_ref, sm_ref):
    xn = _rms_rows(x_ref[...], g_ref[...]).astype(BF16)
    off = 0
    for ref in (z_ref, xbc_ref, xm_ref, o_ref, sm_ref):
        n = ref.shape[-1]
        ref[...] = jnp.dot(xn, w_ref[:, off:off + n], preferred_element_type=F32)
        off += n


def _out0_kernel(y_ref, h_ref, x_ref, w_ref, o_ref):
    o_ref[...] = (x_ref[...] + _dot(y_ref[...], w_ref[0:SSD_D_INNER, :])
                  + _dot(h_ref[...], w_ref[SSD_D_INNER:SSD_D_INNER + MLSTM_D_INNER, :]))


def _ffn_kernel(x_ref, g_ref, wgu_ref, wd_ref, pg_ref, *out_refs, keep_x):
    x = x_ref[...]
    xn = _rms_rows(x, g_ref[...]).astype(BF16)
    acc = x
    for c in range(D_FF // FF_CHUNK):
        lo = c * FF_CHUNK
        gate = jnp.dot(xn, wgu_ref[:, lo:lo + FF_CHUNK], preferred_element_type=F32)
        up = jnp.dot(xn, wgu_ref[:, D_FF + lo:D_FF + lo + FF_CHUNK], preferred_element_type=F32)
        acc = acc + _dot(_silu(gate) * up, wd_ref[lo:lo + FF_CHUNK, :])
    out_refs[0][...] = _rms_rows(acc, pg_ref[...])
    if keep_x:
        out_refs[1][...] = acc


def _rw_in_kernel(xn_ref, xp_ref, mu_ref, wr_ref, wk_ref, wv_ref, w1_ref, w2_ref, a1_ref, a2_ref,
                  g1_ref, g2_ref, w0_ref, a0_ref, kk_w_ref, ka_w_ref, seg_ref, segt_ref,
                  r_ref, lw_ref, k_ref, v_ref, kk_ref, a_ref, g_ref):
    xn = xn_ref[...]
    xx = xp_ref[...] - xn
    mix = [xn + xx * mu_ref[c:c + 1, :] for c in range(6)]
    xr, xw, xk, xv, xa, xg = mix
    r_ref[...] = _dot(xr, wr_ref[...])
    k = _dot(xk, wk_ref[...])
    v_ref[...] = _dot(xv, wv_ref[...])
    w = -_softplus(-(w0_ref[...] + _dot(jnp.tanh(_dot(xw, w1_ref[...])), w2_ref[...]))) - 0.5
    lw_ref[...] = -jnp.exp(w)
    a = _sigmoid(a0_ref[...] + _dot(_dot(xa, a1_ref[...]), a2_ref[...]))
    a_ref[...] = a
    g_ref[...] = _dot(_sigmoid(_dot(xg, g1_ref[...])), g2_ref[...])
    kk = k * kk_w_ref[...]
    ss = _dot_exact_rhs(_dot_exact_rhs(kk * kk, seg_ref[...]), segt_ref[...])
    kk_ref[...] = kk / jnp.maximum(jnp.sqrt(ss), 1e-12)
    k_ref[...] = k * (1.0 + (a - 1.0) * ka_w_ref[...])


def _rw_out_kernel(y_ref, g_ref, x_ref, w_ref, o_ref):
    o_ref[...] = x_ref[...] + _dot(y_ref[...] * g_ref[...], w_ref[...])


def _ssd_kernel(xbc_ref, z_ref, sm_ref, conv0_ref, st0_ref, cw_ref, cb_ref, bias_ref, alog_ref,
                dexp_ref, nw_ref, segt_ref, y_ref, st_ref, ext, state, *, rows, n_valid):
    c = pl.program_id(1)
    first = c == 0

    @pl.when(first)
    def _():
        state[...] = st0_ref[...]

    u = _pad_rows(xbc_ref[...], rows)
    act = _silu(_causal_conv_silu(ext, u, conv0_ref, cw_ref, cb_ref, first, rows))
    xs = act[:, 0:SSD_D_INNER]
    gn = SSD_GROUPS * SSD_D_STATE
    bm = act[:, SSD_D_INNER:SSD_D_INNER + gn]
    cm = act[:, SSD_D_INNER + gn:SSD_D_INNER + 2 * gn]

    tril, _ = _tri_masks(rows)
    tri = jnp.where(tril, 1.0, 0.0).astype(BF16)
    lane = lax.broadcasted_iota(jnp.int32, (rows, LANES), 1)
    rowi = lax.broadcasted_iota(jnp.int32, (rows, LANES), 0)
    head_lane = (lane >= SMALL_DT) & (lane < SMALL_DT + SSD_HEADS)
    dt = _softplus(_pad_rows(sm_ref[...], rows) + bias_ref[...])
    dt = jnp.where(head_lane & (rowi < n_valid), dt, 0.0)
    a = dt * (-jnp.exp(alog_ref[...]))
    acum = _tri_cumsum(tri, a)
    acum_t = acum.T
    dt_t = dt.T
    segt = segt_ref[...]
    eac_x = _dot_exact_rhs(jnp.exp(acum), segt)
    wend_x = _dot_exact_rhs(jnp.exp(acum[rows - 1:rows, :] - acum) * dt, segt)
    xw = xs * wend_x

    half = lax.broadcasted_iota(jnp.int32, (rows, LANES), 1) < SSD_HEAD_DIM
    heads_per_group = SSD_HEADS // SSD_GROUPS
    pairs = []
    for g in range(SSD_GROUPS):
        bg = bm[:, g * SSD_D_STATE:(g + 1) * SSD_D_STATE]
        cg = cm[:, g * SSD_D_STATE:(g + 1) * SSD_D_STATE]
        cb = _dot_nt(cg, bg)
        for q in range(heads_per_group // 2):
            wts = []
            for e in range(2):
                j = g * heads_per_group + 2 * q + e
                seg = acum[:, j:j + 1] - acum_t[j:j + 1, :]
                wts.append(jnp.exp(jnp.where(tril, seg, NEG)) * cb * dt_t[j:j + 1, :])
            p = (g * heads_per_group) // 2 + q
            xp = xs[:, p * LANES:(p + 1) * LANES]
            rhs = jnp.concatenate([jnp.where(half, xp, 0.0), jnp.where(half, 0.0, xp)], axis=0)
            pairs.append(_dot(jnp.concatenate(wts, axis=1), rhs))
    y = jnp.concatenate(pairs, axis=1)

    st = state[...]
    gw = heads_per_group * SSD_HEAD_DIM
    y_state = jnp.concatenate(
        [_dot_nt(cm[:, g * SSD_D_STATE:(g + 1) * SSD_D_STATE], st[g * gw:(g + 1) * gw, :])
         for g in range(SSD_GROUPS)], axis=1)
    y = y + eac_x * y_state + dexp_ref[...] * xs

    yz = y * _silu(_pad_rows(z_ref[...], rows))
    outs = []
    for g in range(SSD_GROUPS):
        grp = yz[:, g * gw:(g + 1) * gw]
        outs.append(grp * lax.rsqrt(jnp.mean(grp * grp, axis=-1, keepdims=True) + NORM_EPS))
    y_ref[...] = (jnp.concatenate(outs, axis=1) * nw_ref[...])[0:y_ref.shape[0], :]

    dcol = jnp.exp(acum_t[:, rows - 1:rows])
    for g in range(SSD_GROUPS):
        upd = _dot_tn(xw[:, g * gw:(g + 1) * gw], bm[:, g * SSD_D_STATE:(g + 1) * SSD_D_STATE])
        for h in range(heads_per_group):
            j = g * heads_per_group + h
            lo = j * SSD_HEAD_DIM
            state[lo:lo + SSD_HEAD_DIM, :] = (st[lo:lo + SSD_HEAD_DIM, :] * dcol[j:j + 1, :]
                                              + upd[h * SSD_HEAD_DIM:(h + 1) * SSD_HEAD_DIM, :])

    @pl.when(c == pl.num_programs(1) - 1)
    def _():
        st_ref[...] = state[...]


def _mlstm_kernel(xm_ref, o_ref, sm_ref, conv0_ref, c0_ref, n0_ref, m0_ref, cw_ref, cb_ref, bias_ref,
                  wq_ref, wk_ref, wv_ref, nw_ref, skip_ref, h_ref, c_ref, n_ref, m_ref,
                  ext, cst, nst, mst, *, rows, n_valid):
    c = pl.program_id(1)
    first = c == 0

    @pl.when(first)
    def _():
        cst[...] = c0_ref[...]
        nst[0:MLSTM_HEADS, :] = n0_ref[...]
        mst[...] = m0_ref[...]

    raw = _pad_rows(xm_ref[...], rows)
    act = _silu(_causal_conv_silu(ext, raw, conv0_ref, cw_ref, cb_ref, first, rows))
    hd = MLSTM_HEAD_DIM
    q = jnp.concatenate([_dot(act[:, s * hd:(s + 1) * hd], wq_ref[s]) for s in range(MLSTM_HEADS)], axis=1)
    k = jnp.concatenate([_dot(act[:, s * hd:(s + 1) * hd], wk_ref[s]) for s in range(MLSTM_HEADS)], axis=1)
    v = jnp.concatenate([_dot(raw[:, s * hd:(s + 1) * hd], wv_ref[s]) for s in range(MLSTM_HEADS)], axis=1)

    tril, _ = _tri_masks(rows)
    tri = jnp.where(tril, 1.0, 0.0).astype(BF16)
    rowi = lax.broadcasted_iota(jnp.int32, (rows, LANES), 0)
    valid = rowi < n_valid
    pre = _pad_rows(sm_ref[...], rows) + bias_ref[...]
    logi = jnp.where(valid, pre, NEG)
    logf = jnp.where(valid, jnp.minimum(pre, 0.0) - jnp.log(1.0 + jnp.exp(-jnp.abs(pre))), 0.0)
    bcum = _tri_cumsum(tri, logf)
    bcum_t = bcum.T
    logi_t = logi.T
    m_all = mst[...]
    lane1 = lax.broadcasted_iota(jnp.int32, (1, LANES), 1)

    hs = []
    for h in range(MLSTM_HEADS):
        qh, kh, vh = (t[:, h * hd:(h + 1) * hd] for t in (q, k, v))
        bc = bcum[:, SMALL_F + h:SMALL_F + h + 1]
        bct = bcum_t[SMALL_F + h:SMALL_F + h + 1, :]
        lit = logi_t[SMALL_I + h:SMALL_I + h + 1, :]
        li = logi[:, SMALL_I + h:SMALL_I + h + 1]
        m_h = m_all[:, h:h + 1]
        c_h = cst[h * hd:(h + 1) * hd, :]
        n_h = nst[h:h + 1, :]
        dlog = jnp.where(tril, bc - bct + lit, NEG)
        inter = bc + m_h
        m_t = jnp.maximum(inter, jnp.max(dlog, axis=1, keepdims=True))
        s = _dot_nt(qh, kh) * jnp.exp(dlog - m_t)
        w_inter = jnp.exp(inter - m_t)
        num = _dot(s, vh) + w_inter * _dot(qh, c_h)
        den = jnp.sum(s, axis=1, keepdims=True) + w_inter * jnp.sum(qh * n_h, axis=1, keepdims=True)
        hs.append(num / jnp.maximum(jnp.abs(den), jnp.exp(-m_t)))
        b_end = bc[rows - 1:rows, :]
        wlog = b_end - bc + li
        m_new = jnp.maximum(b_end + m_h, jnp.max(wlog, axis=0, keepdims=True))
        ws = jnp.exp(wlog - m_new)
        dc = jnp.exp(b_end + m_h - m_new)
        kws = kh * ws
        cst[h * hd:(h + 1) * hd, :] = dc * c_h + _dot_tn(kws, vh)
        nst[h:h + 1, :] = dc * n_h + jnp.sum(kws, axis=0, keepdims=True)
        m_all = jnp.where(lane1 == h, m_new, m_all)
    mst[...] = m_all

    outs = []
    for h in range(MLSTM_HEADS):
        outs.append(hs[h] * lax.rsqrt(jnp.mean(hs[h] * hs[h], axis=-1, keepdims=True) + NORM_EPS))
    hm = jnp.concatenate(outs, axis=1) * nw_ref[...]
    hm = (hm + skip_ref[...] * act) * _sigmoid(_pad_rows(o_ref[...], rows))
    h_ref[...] = hm[0:h_ref.shape[0], :]

    @pl.when(c == pl.num_programs(1) - 1)
    def _():
        c_ref[...] = cst[...]
        n_ref[...] = nst[0:MLSTM_HEADS, :]
        m_ref[...] = mst[...]


def _rwkv_kernel(r_ref, lw_ref, k_ref, v_ref, kk_ref, a_ref, s0_ref, rk_ref, lnw_ref, lnb_ref,
                 seg_ref, segt_ref, y_ref, s_ref, sblk, *, rows, n_valid):
    c = pl.program_id(1)

    @pl.when(c == 0)
    def _():
        sblk[...] = s0_ref[...]

    r, lw, k, v, kk, a = (_pad_rows(t[...], rows) for t in (r_ref, lw_ref, k_ref, v_ref, kk_ref, a_ref))
    tril, strict = _tri_masks(rows)
    tri = jnp.where(tril, 1.0, 0.0).astype(BF16)
    cum = _tri_cumsum(tri, lw)
    last = cum[rows - 1:rows, :]
    einv = jnp.exp(-cum)
    eend = jnp.exp(last - cum)
    ka = kk * a
    at = -(kk * jnp.exp(cum - lw))
    rt = r * jnp.exp(cum)
    bt = ka * einv
    kt = k * einv
    bh = ka * eend
    kh = k * eend
    wl = jnp.exp(last)
    half = lax.broadcasted_iota(jnp.int32, (rows, LANES), 1) < RWKV_HEAD_DIM
    steps = rows.bit_length() - 1

    def two(x):
        return [jnp.where(half, x, 0.0), jnp.where(half, 0.0, x)]

    ys = []
    for p in range(RWKV_HEADS // 2):
        sl = slice(p * LANES, (p + 1) * LANES)
        lhs4 = jnp.concatenate(two(at[:, sl]) + two(rt[:, sl]), axis=0).astype(BF16)
        sp = sblk[sl, :]
        pb = _dot_nt(lhs4, bt[:, sl])
        pk = _dot_nt(lhs4, kt[:, sl])
        xs = _dot_nt(lhs4, sp)
        vs = two(v[:, sl])
        us = []
        for e in range(2):
            x_e = xs[e * rows:(e + 1) * rows, :]
            if n_valid > 1:
                mak = jnp.where(strict, pk[e * rows:(e + 1) * rows, :], 0.0)
                u_e = x_e + _dot(mak, vs[e])
                pw = jnp.where(strict, pb[e * rows:(e + 1) * rows, :], 0.0).astype(BF16)
                for j in range(steps):
                    u_e = u_e + _dot(pw, u_e)
                    if j < steps - 1:
                        pw = _dot(pw, pw).astype(BF16)
            else:
                u_e = x_e
            us.append(u_e)
        w4 = jnp.concatenate(
            [jnp.where(tril, pb[(2 + e) * rows:(3 + e) * rows, :], 0.0) for e in range(2)]
            + [jnp.where(tril, pk[(2 + e) * rows:(3 + e) * rows, :], 0.0) for e in range(2)], axis=1)
        r4 = jnp.concatenate(us + vs, axis=0).astype(BF16)
        ys.append(xs[2 * rows:3 * rows, :] + xs[3 * rows:4 * rows, :] + _dot(w4, r4))
        rh = jnp.concatenate(two(bh[:, sl]) + two(kh[:, sl]), axis=0)
        sblk[sl, :] = sp * wl[:, sl] + _dot_tn(r4, rh)
    y = jnp.concatenate(ys, axis=1)

    seg = seg_ref[...]
    segt = segt_ref[...]

    def head_sum(x):
        return _dot_exact_rhs(_dot_exact_rhs(x, seg), segt)

    inv_n = 1.0 / RWKV_HEAD_DIM
    dlt = y - head_sum(y) * inv_n
    yn = dlt * lax.rsqrt(head_sum(dlt * dlt) * inv_n + RWKV_LN_EPS)
    out = yn * lnw_ref[...] + lnb_ref[...] + head_sum(r * k * rk_ref[...]) * v
    y_ref[...] = out[0:y_ref.shape[0], :]

    @pl.when(c == pl.num_programs(1) - 1)
    def _():
        s_ref[...] = sblk[...]


def _params(sem):
    return pltpu.CompilerParams(dimension_semantics=sem, vmem_limit_bytes=VMEM_LIMIT)


def _row_tile(m):
    return ROW_TILE if m % ROW_TILE == 0 else m


def _rows_spec(tm, n):
    return pl.BlockSpec((tm, n), lambda i: (i, 0))


def _full_spec(shape):
    nd = len(shape)
    return pl.BlockSpec(shape, lambda *_: (0,) * nd)


def _rowwise_call(kernel, row_ins, full_ins, out_widths, m):
    tm = _row_tile(m)
    return pl.pallas_call(
        kernel,
        grid=(m // tm,),
        in_specs=[_rows_spec(tm, a.shape[1]) for a in row_ins] + [_full_spec(a.shape) for a in full_ins],
        out_specs=[_rows_spec(tm, n) for n in out_widths],
        out_shape=[jax.ShapeDtypeStruct((m, n), F32) for n in out_widths],
        compiler_params=_params(("parallel",)),
    )(*row_ins, *full_ins)


def _seq_spec(rows, n):
    return pl.BlockSpec((None, rows, n), lambda b, c: (b, c, 0))


def _per_seq_spec(shape):
    nd = len(shape)
    return pl.BlockSpec((None,) + tuple(shape[1:]), lambda b, c: (b,) + (0,) * (nd - 1))


def _const_spec(shape):
    nd = len(shape)
    return pl.BlockSpec(shape, lambda b, c: (0,) * nd)


def _scan_call(kernel, seq_ins, state_ins, const_ins, seq_out_widths, state_out_shapes, scratch, t_valid):
    bsz, tpad = seq_ins[0].shape[:2]
    rows = SCAN_CHUNK
    blk = rows if tpad % rows == 0 else tpad
    n_valid = rows if blk == rows else t_valid
    return pl.pallas_call(
        functools.partial(kernel, rows=rows, n_valid=n_valid),
        grid=(bsz, tpad // blk),
        in_specs=([_seq_spec(blk, a.shape[2]) for a in seq_ins] + [_per_seq_spec(a.shape) for a in state_ins]
                  + [_const_spec(a.shape) for a in const_ins]),
        out_specs=([_seq_spec(blk, n) for n in seq_out_widths] + [_per_seq_spec(s) for s in state_out_shapes]),
        out_shape=([jax.ShapeDtypeStruct((bsz, tpad, n), F32) for n in seq_out_widths]
                   + [jax.ShapeDtypeStruct(s, F32) for s in state_out_shapes]),
        scratch_shapes=scratch,
        compiler_params=_params(("parallel", "arbitrary")),
    )(*seq_ins, *state_ins, *const_ins)


def _row(v):
    return v.reshape(1, -1).astype(F32)


def _pad_lanes(v, offset=0, width=LANES):
    v = v.reshape(1, -1).astype(F32)
    return jnp.pad(v, ((0, 0), (offset, width - offset - v.shape[1])))


def _pad_cols(w, width):
    return jnp.pad(w, ((0, 0), (0, width - w.shape[1])))


def _pad_rows_to(w, rows):
    return jnp.pad(w, ((0, rows - w.shape[0]), (0, 0)))


def _blockdiag_slabs(w, scale=1.0):
    per = MLSTM_HEAD_DIM // QKV_BLOCK
    ws = (w * scale).reshape(-1, per, QKV_BLOCK, QKV_BLOCK)
    eye = jnp.eye(per, dtype=w.dtype)
    return jnp.einsum('sncd,nm->sncmd', ws, eye).reshape(-1, MLSTM_HEAD_DIM, MLSTM_HEAD_DIM).astype(BF16)


def _head_indicator():
    ch = jnp.arange(RWKV_HEADS * RWKV_HEAD_DIM) // RWKV_HEAD_DIM
    seg = (ch[:, None] == jnp.arange(LANES)[None, :]).astype(BF16)
    return seg, seg.T


def _wkv_to_blocks(s):
    bsz = s.shape[0]
    n = RWKV_HEAD_DIM
    sp = s.reshape(bsz, RWKV_HEADS // 2, 2, n, n)
    eye = jnp.eye(2, dtype=s.dtype)
    return jnp.einsum('bpevk,ef->bpevfk', sp, eye).reshape(bsz, RWKV_HEADS * n, 2 * n)


def _wkv_from_blocks(sb):
    bsz = sb.shape[0]
    n = RWKV_HEAD_DIM
    s6 = sb.reshape(bsz, RWKV_HEADS // 2, 2, n, 2, n)
    return jnp.stack([s6[:, :, 0, :, 0, :], s6[:, :, 1, :, 1, :]], axis=2).reshape(bsz, RWKV_HEADS, n, n)


def _pad_time(x, tpad):
    return jnp.pad(x, ((0, 0), (0, tpad - x.shape[1]), (0, 0)))


def _conv_tail(buf):
    return jnp.pad(buf, ((0, 0), (SUBLANES - (CONV_W - 1), 0), (0, 0)))


def _trunk(x, conv, ssm, mc, mn, mm, shift, wkv, p):
    bsz, t, d = x.shape
    m = bsz * t
    tpad = t if t % SCAN_CHUNK == 0 else SUBLANES
    seg, segt = p['seg'], p['segt']
    x2 = x.reshape(m, d)

    z, xbc, xm, o_pre, small = _rowwise_call(
        _in0_kernel, [x2], [p['norm_mix0'], p['w_in0']],
        [SSD_D_INNER, XBC_DIM, MLSTM_D_INNER, MLSTM_D_INNER, LANES], m)
    seq = lambda a: _pad_time(a.reshape(bsz, t, -1), tpad)
    xbc3, xm3 = xbc.reshape(bsz, t, -1), xm.reshape(bsz, t, -1)
    conv_in = jnp.concatenate([xbc3, xm3], axis=-1)
    new_conv = jnp.concatenate([conv, conv_in], axis=1)[:, -(CONV_W - 1):]
    tail = _conv_tail(conv)

    y_ssd, new_ssm = _scan_call(
        _ssd_kernel, [seq(xbc), seq(z), seq(small)],
        [tail[:, :, :XBC_DIM], ssm.reshape(bsz, SSD_D_INNER, SSD_D_STATE)],
        [p['conv_w'][:, :XBC_DIM], p['conv_b'][:, :XBC_DIM], p['small_bias'], p['ssd_a_log'], p['ssd_d'],
         p['ssd_norm'], segt],
        [SSD_D_INNER], [(bsz, SSD_D_INNER, SSD_D_STATE)],
        [pltpu.VMEM((SCAN_CHUNK + SUBLANES, XBC_DIM), F32), pltpu.VMEM((SSD_D_INNER, SSD_D_STATE), F32)], t)

    hm, new_c, new_n, new_m = _scan_call(
        _mlstm_kernel, [seq(xm), seq(o_pre), seq(small)],
        [tail[:, :, XBC_DIM:], mc.reshape(bsz, MLSTM_D_INNER, MLSTM_HEAD_DIM), mn,
         _pad_cols(mm, LANES).reshape(bsz, 1, LANES)],
        [p['conv_w'][:, XBC_DIM:], p['conv_b'][:, XBC_DIM:], p['small_bias'], p['ml_wq'], p['ml_wk'], p['ml_wv'],
         p['ml_norm'], p['ml_skip']],
        [MLSTM_D_INNER], [(bsz, MLSTM_D_INNER, MLSTM_HEAD_DIM), (bsz, MLSTM_HEADS, MLSTM_HEAD_DIM),
                          (bsz, 1, LANES)],
        [pltpu.VMEM((SCAN_CHUNK + SUBLANES, MLSTM_D_INNER), F32), pltpu.VMEM((MLSTM_D_INNER, MLSTM_HEAD_DIM), F32),
         pltpu.VMEM((SUBLANES, MLSTM_HEAD_DIM), F32), pltpu.VMEM((1, LANES), F32)], t)

    unseq = lambda a: a[:, :t].reshape(m, -1)
    (x1,) = _rowwise_call(_out0_kernel, [unseq(y_ssd), unseq(hm), x2], [p['w_out0']], [d], m)
    xn1, x2b = _rowwise_call(functools.partial(_ffn_kernel, keep_x=True), [x1],
                             [p['norm_ffn0'], p['ffn_gu0'], p['ffn_d0'], p['norm_mix1']], [d, d], m)

    xn3 = xn1.reshape(bsz, t, d)
    xprev = jnp.concatenate([shift[:, None, :], xn3[:, :-1]], axis=1).reshape(m, d)
    r, lw, k, v, kk, a, g = _rowwise_call(
        _rw_in_kernel, [xn1, xprev],
        [p['rw_mu'], p['rw_wr'], p['rw_wk'], p['rw_wv'], p['rw_w1'], p['rw_w2'], p['rw_a1'], p['rw_a2'],
         p['rw_g1'], p['rw_g2'], p['rw_w0'], p['rw_a0'], p['rw_k_k'], p['rw_k_a'], seg, segt], [d] * 7, m)
    y_rw, new_wkv = _scan_call(
        _rwkv_kernel, [seq(t_) for t_ in (r, lw, k, v, kk, a)], [_wkv_to_blocks(wkv)],
        [p['rw_r_k'], p['rw_ln_w'], p['rw_ln_b'], seg, segt],
        [d], [(bsz, RWKV_HEADS * RWKV_HEAD_DIM, 2 * RWKV_HEAD_DIM)],
        [pltpu.VMEM((RWKV_HEADS * RWKV_HEAD_DIM, 2 * RWKV_HEAD_DIM), F32)], t)
    (x3,) = _rowwise_call(_rw_out_kernel, [unseq(y_rw), g, x2b], [p['rw_wo']], [d], m)
    (y,) = _rowwise_call(functools.partial(_ffn_kernel, keep_x=False), [x3],
                         [p['norm_ffn1'], p['ffn_gu1'], p['ffn_d1'], p['norm_final']], [d], m)

    states = (new_conv, new_ssm.reshape(ssm.shape), new_c.reshape(mc.shape), new_n,
              new_m.reshape(bsz, LANES)[:, :MLSTM_HEADS], xn3[:, -1], _wkv_from_blocks(new_wkv))
    return y.reshape(bsz, t, d), states


def kernel(x_prompt, x_sample, state_conv, state_ssm, state_mlstm_c, state_mlstm_n, state_mlstm_m, state_shift, state_wkv, norm_mix, norm_ffn, norm_final, w_in0, conv_w, conv_b, ssd_dt_bias, ssd_a_log, ssd_d, ssd_norm, ml_wq, ml_wk, ml_wv, ml_i_bias, ml_f_bias, ml_norm, ml_skip, w_out0, rw_mu, rw_wr, rw_wk, rw_wv, rw_wo, rw_w0, rw_w1, rw_w2, rw_a0, rw_a1, rw_a2, rw_g1, rw_g2, rw_k_k, rw_k_a, rw_r_k, rw_ln_w, rw_ln_b, ffn_w_gate_up, ffn_w_down):
    assert norm_mix.shape[0] == 2 and w_in0.shape[0] == 1 and rw_wr.shape[0] == 1, "two layers: SSD|mLSTM then RWKV-7"
    s1 = SSD_D_INNER
    s2 = s1 + XBC_DIM + MLSTM_D_INNER
    s3 = s2 + SSD_HEADS
    s4 = s3 + MLSTM_D_INNER
    s5 = s4 + MLSTM_HEADS
    w0 = w_in0[0]
    small_w = _pad_cols(jnp.concatenate([w0[:, s2:s3], w0[:, s4:s5], w0[:, s5:]], axis=1), LANES)
    seg, segt = _head_indicator()
    lora = lambda w1, w2, width: (_pad_cols(w1, width).astype(BF16), _pad_rows_to(w2, width).astype(BF16))
    rw_w1p, rw_w2p = lora(rw_w1[0], rw_w2[0], LANES)
    rw_a1p, rw_a2p = lora(rw_a1[0], rw_a2[0], LANES)
    rw_g1p, rw_g2p = lora(rw_g1[0], rw_g2[0], 2 * LANES)
    p = dict(
        norm_mix0=_row(norm_mix[0]), norm_mix1=_row(norm_mix[1]), norm_ffn0=_row(norm_ffn[0]),
        norm_ffn1=_row(norm_ffn[1]), norm_final=_row(norm_final),
        w_in0=jnp.concatenate([w0[:, :s2], w0[:, s3:s4], small_w], axis=1).astype(BF16),
        conv_w=conv_w[0], conv_b=_row(conv_b[0]),
        small_bias=jnp.concatenate([_row(ssd_dt_bias[0]), _row(ml_i_bias[0]), _row(ml_f_bias[0]),
                                    jnp.zeros((1, LANES - SSD_HEADS - 2 * MLSTM_HEADS), F32)], axis=1),
        ssd_a_log=_pad_lanes(ssd_a_log[0]), ssd_d=_row(jnp.repeat(ssd_d[0], SSD_HEAD_DIM)),
        ssd_norm=_row(ssd_norm[0]),
        ml_wq=_blockdiag_slabs(ml_wq[0]), ml_wk=_blockdiag_slabs(ml_wk[0], MLSTM_HEAD_DIM ** -0.5),
        ml_wv=_blockdiag_slabs(ml_wv[0]), ml_norm=_row(ml_norm[0]), ml_skip=_row(ml_skip[0]),
        w_out0=w_out0[0].astype(BF16),
        rw_mu=jnp.pad(rw_mu[0], ((0, SUBLANES - rw_mu.shape[1]), (0, 0))),
        rw_wr=rw_wr[0].astype(BF16), rw_wk=rw_wk[0].astype(BF16), rw_wv=rw_wv[0].astype(BF16),
        rw_wo=rw_wo[0].astype(BF16), rw_w0=_row(rw_w0[0]), rw_a0=_row(rw_a0[0]),
        rw_w1=rw_w1p, rw_w2=rw_w2p, rw_a1=rw_a1p, rw_a2=rw_a2p, rw_g1=rw_g1p, rw_g2=rw_g2p,
        rw_k_k=_row(rw_k_k[0]), rw_k_a=_row(rw_k_a[0]), rw_r_k=_row(rw_r_k[0]),
        rw_ln_w=_row(rw_ln_w[0]), rw_ln_b=_row(rw_ln_b[0]),
        ffn_gu0=ffn_w_gate_up[0].astype(BF16), ffn_gu1=ffn_w_gate_up[1].astype(BF16),
        ffn_d0=ffn_w_down[0].astype(BF16), ffn_d1=ffn_w_down[1].astype(BF16),
        seg=seg, segt=segt,
    )
    bp = x_prompt.shape[0]
    zeros = lambda s: jnp.zeros((bp,) + s.shape[2:], F32)
    y_p, st_p = _trunk(x_prompt, zeros(state_conv), zeros(state_ssm), zeros(state_mlstm_c),
                       zeros(state_mlstm_n), zeros(state_mlstm_m), zeros(state_shift), zeros(state_wkv), p)
    y_s, st_s = _trunk(x_sample, state_conv[0], state_ssm[0], state_mlstm_c[0], state_mlstm_n[0],
                       state_mlstm_m[0], state_shift[0], state_wkv[0], p)
    out = [y_p, y_s]
    for a, b in zip(st_p, st_s):
        out += [a[None], b[None]]
    return tuple(out)
```

```python
import functools

import jax
import jax.numpy as jnp
from jax import lax
from jax.experimental import pallas as pl
from jax.experimental.pallas import tpu as pltpu

F32 = jnp.float32
BF16 = jnp.bfloat16

D_MODEL = 1024
NORM_EPS = 1e-5
CONV_W = 4
SSD_HEADS = 16
SSD_HEAD_DIM = 64
SSD_GROUPS = 2
SSD_D_STATE = 128
SSD_D_INNER = SSD_HEADS * SSD_HEAD_DIM
SSD_GROUP_W = SSD_D_INNER // SSD_GROUPS
XBC_DIM = SSD_D_INNER + 2 * SSD_GROUPS * SSD_D_STATE
MLSTM_HEADS = 4
MLSTM_HEAD_DIM = 256
MLSTM_D_INNER = MLSTM_HEADS * MLSTM_HEAD_DIM
QKV_BLOCK = 4
RWKV_HEADS = 16
RWKV_HEAD_DIM = 64
RWKV_D = RWKV_HEADS * RWKV_HEAD_DIM
RWKV_LN_EPS = 64e-5
D_FF = 2816
FF_CHUNK = 256
LANES = 128
SUBLANES = 8
SMALL_DT, SMALL_I, SMALL_F = 0, 16, 20
NEG = -1e30
SCAN_CHUNK = 128
STEP_SEQS = SUBLANES
ROW_TILE = 256
VMEM_LIMIT = 56 * 1024 * 1024


def _dot(a, b):
    return jnp.dot(a.astype(BF16), b.astype(BF16), preferred_element_type=F32)


def _dot_nt(a, b):
    return lax.dot_general(a.astype(BF16), b.astype(BF16), (((1,), (1,)), ((), ())),
                           preferred_element_type=F32)


def _dot_tn(a, b):
    return lax.dot_general(a.astype(BF16), b.astype(BF16), (((0,), (0,)), ((), ())),
                           preferred_element_type=F32)


def _split(x, terms):
    parts = []
    rem = x
    for _ in range(terms):
        p = rem.astype(BF16)
        parts.append(p)
        rem = rem - p.astype(F32)
    return parts


def _dot_exact_rhs(x, m, terms=2):
    out = None
    for p in _split(x, terms):
        t = jnp.dot(p, m, preferred_element_type=F32)
        out = t if out is None else out + t
    return out


def _tri_cumsum(tri, x):
    out = None
    for p in _split(x, 3):
        t = jnp.dot(tri, p, preferred_element_type=F32)
        out = t if out is None else out + t
    return out


def _rms_rows(x, g):
    return x * lax.rsqrt(jnp.mean(x * x, axis=-1, keepdims=True) + NORM_EPS) * g


def _group_rms(x, width):
    outs = []
    for lo in range(0, x.shape[1], width):
        grp = x[:, lo:lo + width]
        outs.append(grp * lax.rsqrt(jnp.mean(grp * grp, axis=-1, keepdims=True) + NORM_EPS))
    return jnp.concatenate(outs, axis=1)


def _sigmoid(x):
    return 1.0 / (1.0 + jnp.exp(-x))


def _silu(x):
    return x * _sigmoid(x)


def _softplus(x):
    return jnp.maximum(x, 0.0) + jnp.log(1.0 + jnp.exp(-jnp.abs(x)))


def _log_sigmoid(x):
    return jnp.minimum(x, 0.0) - jnp.log(1.0 + jnp.exp(-jnp.abs(x)))


def _tri_masks(n):
    row = lax.broadcasted_iota(jnp.int32, (n, n), 0)
    col = lax.broadcasted_iota(jnp.int32, (n, n), 1)
    return row >= col, row > col


def _chunk_conv(ext, u, conv0_ref, cw_ref, cb_ref, first):
    rows = u.shape[0]

    @pl.when(first)
    def _():
        ext[0:SUBLANES, :] = conv0_ref[...]

    @pl.when(jnp.logical_not(first))
    def _():
        ext[0:SUBLANES, :] = ext[rows:rows + SUBLANES, :]

    ext[SUBLANES:SUBLANES + rows, :] = u
    conv = cb_ref[...]
    base = SUBLANES - (CONV_W - 1)
    for j in range(CONV_W):
        conv = conv + ext[base + j:base + j + rows, :] * cw_ref[j:j + 1, :]
    return conv


def _step_conv(u_ref, tail_refs, cw_ref, cb_ref):
    conv = cb_ref[...] + u_ref[...] * cw_ref[CONV_W - 1:CONV_W, :]
    for j, t in enumerate(tail_refs):
        conv = conv + t[...] * cw_ref[j:j + 1, :]
    return conv


def _blockdiag_qkv(act, raw, wq_ref, wk_ref, wv_ref):
    hd = MLSTM_HEAD_DIM
    cat = lambda x, w: jnp.concatenate(
        [_dot(x[:, s * hd:(s + 1) * hd], w[s]) for s in range(MLSTM_HEADS)], axis=1)
    return cat(act, wq_ref), cat(act, wk_ref), cat(raw, wv_ref)


def _rwkv_head_norm(y, r, k, v, rk_ref, lnw_ref, lnb_ref, seg_ref, segt_ref):
    seg = seg_ref[...]
    segt = segt_ref[...]
    head_sum = lambda x: _dot_exact_rhs(_dot_exact_rhs(x, seg), segt)
    inv_n = 1.0 / RWKV_HEAD_DIM
    dlt = y - head_sum(y) * inv_n
    yn = dlt * lax.rsqrt(head_sum(dlt * dlt) * inv_n + RWKV_LN_EPS)
    return yn * lnw_ref[...] + lnb_ref[...] + head_sum(r * k * rk_ref[...]) * v


def _in0_kernel(x_ref, g_ref, w_ref, z_ref, xbc_ref, xm_ref, o_ref, sm_ref):
    xn = _rms_rows(x_ref[...], g_ref[...]).astype(BF16)
    off = 0
    for ref in (z_ref, xbc_ref, xm_ref, o_ref, sm_ref):
        n = ref.shape[-1]
        ref[...] = jnp.dot(xn, w_ref[:, off:off + n], preferred_element_type=F32)
        off += n


def _out0_kernel(y_ref, h_ref, x_ref, w_ref, o_ref):
    o_ref[...] = (x_ref[...] + _dot(y_ref[...], w_ref[0:SSD_D_INNER, :])
                  + _dot(h_ref[...], w_ref[SSD_D_INNER:SSD_D_INNER + MLSTM_D_INNER, :]))


def _ffn_kernel(x_ref, g_ref, wgu_ref, wd_ref, pg_ref, *out_refs, keep_x):
    x = x_ref[...]
    xn = _rms_rows(x, g_ref[...]).astype(BF16)
    acc = x
    for c in range(D_FF // FF_CHUNK):
        lo = c * FF_CHUNK
        gate = jnp.dot(xn, wgu_ref[:, lo:lo + FF_CHUNK], preferred_element_type=F32)
        up = jnp.dot(xn, wgu_ref[:, D_FF + lo:D_FF + lo + FF_CHUNK], preferred_element_type=F32)
        acc = acc + _dot(_silu(gate) * up, wd_ref[lo:lo + FF_CHUNK, :])
    out_refs[0][...] = _rms_rows(acc, pg_ref[...])
    if keep_x:
        out_refs[1][...] = acc


def _rw_in_kernel(xn_ref, xp_ref, mu_ref, wr_ref, wk_ref, wv_ref, w1_ref, w2_ref, a1_ref, a2_ref,
                  g1_ref, g2_ref, w0_ref, a0_ref, kk_w_ref, ka_w_ref, seg_ref, segt_ref,
                  r_ref, lw_ref, k_ref, v_ref, kk_ref, a_ref, g_ref):
    xn = xn_ref[...]
    xx = xp_ref[...] - xn
    xr, xw, xk, xv, xa, xg = (xn + xx * mu_ref[c:c + 1, :] for c in range(6))
    r_ref[...] = _dot(xr, wr_ref[...])
    k = _dot(xk, wk_ref[...])
    v_ref[...] = _dot(xv, wv_ref[...])
    w = -_softplus(-(w0_ref[...] + _dot(jnp.tanh(_dot(xw, w1_ref[...])), w2_ref[...]))) - 0.5
    lw_ref[...] = -jnp.exp(w)
    a = _sigmoid(a0_ref[...] + _dot(_dot(xa, a1_ref[...]), a2_ref[...]))
    a_ref[...] = a
    g_ref[...] = _dot(_sigmoid(_dot(xg, g1_ref[...])), g2_ref[...])
    kk = k * kk_w_ref[...]
    ss = _dot_exact_rhs(_dot_exact_rhs(kk * kk, seg_ref[...]), segt_ref[...])
    kk_ref[...] = kk / jnp.maximum(jnp.sqrt(ss), 1e-12)
    k_ref[...] = k * (1.0 + (a - 1.0) * ka_w_ref[...])


def _rw_out_kernel(y_ref, g_ref, x_ref, w_ref, o_ref):
    o_ref[...] = x_ref[...] + _dot(y_ref[...] * g_ref[...], w_ref[...])


def _ssd_kernel(xbc_ref, z_ref, sm_ref, conv0_ref, st0_ref, cw_ref, cb_ref, bias_ref, alog_ref,
                dexp_ref, nw_ref, segt_ref, y_ref, st_ref, ext, state):
    c = pl.program_id(1)
    first = c == 0
    rows = xbc_ref.shape[0]

    @pl.when(first)
    def _():
        state[...] = st0_ref[...]

    act = _silu(_chunk_conv(ext, xbc_ref[...], conv0_ref, cw_ref, cb_ref, first))
    xs = act[:, 0:SSD_D_INNER]
    gn = SSD_GROUPS * SSD_D_STATE
    bm = act[:, SSD_D_INNER:SSD_D_INNER + gn]
    cm = act[:, SSD_D_INNER + gn:SSD_D_INNER + 2 * gn]

    tril, _ = _tri_masks(rows)
    tri = jnp.where(tril, 1.0, 0.0).astype(BF16)
    lane = lax.broadcasted_iota(jnp.int32, (rows, LANES), 1)
    head_lane = (lane >= SMALL_DT) & (lane < SMALL_DT + SSD_HEADS)
    dt = jnp.where(head_lane, _softplus(sm_ref[...] + bias_ref[...]), 0.0)
    acum = _tri_cumsum(tri, dt * (-jnp.exp(alog_ref[...])))
    acum_t = acum.T
    dt_t = dt.T
    segt = segt_ref[...]
    eac_x = _dot_exact_rhs(jnp.exp(acum), segt)
    wend_x = _dot_exact_rhs(jnp.exp(acum[rows - 1:rows, :] - acum) * dt, segt)
    xw = xs * wend_x

    half = lane < SSD_HEAD_DIM
    heads_per_group = SSD_HEADS // SSD_GROUPS
    pairs = []
    for g in range(SSD_GROUPS):
        bg = bm[:, g * SSD_D_STATE:(g + 1) * SSD_D_STATE]
        cg = cm[:, g * SSD_D_STATE:(g + 1) * SSD_D_STATE]
        cb = _dot_nt(cg, bg)
        for q in range(heads_per_group // 2):
            wts = []
            for e in range(2):
                j = g * heads_per_group + 2 * q + e
                seg = acum[:, j:j + 1] - acum_t[j:j + 1, :]
                wts.append(jnp.exp(jnp.where(tril, seg, NEG)) * cb * dt_t[j:j + 1, :])
            p = (g * heads_per_group) // 2 + q
            xp = xs[:, p * LANES:(p + 1) * LANES]
            rhs = jnp.concatenate([jnp.where(half, xp, 0.0), jnp.where(half, 0.0, xp)], axis=0)
            pairs.append(_dot(jnp.concatenate(wts, axis=1), rhs))
    y = jnp.concatenate(pairs, axis=1)

    st = state[...]
    gw = SSD_GROUP_W
    y_state = jnp.concatenate(
        [_dot_nt(cm[:, g * SSD_D_STATE:(g + 1) * SSD_D_STATE], st[g * gw:(g + 1) * gw, :])
         for g in range(SSD_GROUPS)], axis=1)
    y = y + eac_x * y_state + dexp_ref[...] * xs
    y_ref[...] = _group_rms(y * _silu(z_ref[...]), gw) * nw_ref[...]

    dcol = jnp.exp(acum_t[:, rows - 1:rows])
    for g in range(SSD_GROUPS):
        upd = _dot_tn(xw[:, g * gw:(g + 1) * gw], bm[:, g * SSD_D_STATE:(g + 1) * SSD_D_STATE])
        for h in range(heads_per_group):
            j = g * heads_per_group + h
            lo = j * SSD_HEAD_DIM
            state[lo:lo + SSD_HEAD_DIM, :] = (st[lo:lo + SSD_HEAD_DIM, :] * dcol[j:j + 1, :]
                                              + upd[h * SSD_HEAD_DIM:(h + 1) * SSD_HEAD_DIM, :])

    @pl.when(c == pl.num_programs(1) - 1)
    def _():
        st_ref[...] = state[...]


def _mlstm_kernel(xm_ref, o_ref, sm_ref, conv0_ref, c0_ref, n0_ref, m0_ref, cw_ref, cb_ref, bias_ref,
                  wq_ref, wk_ref, wv_ref, nw_ref, skip_ref, h_ref, c_ref, n_ref, m_ref,
                  ext, cst, nst, mst):
    c = pl.program_id(1)
    first = c == 0
    rows = xm_ref.shape[0]

    @pl.when(first)
    def _():
        cst[...] = c0_ref[...]
        nst[...] = n0_ref[...]
        mst[...] = m0_ref[...]

    raw = xm_ref[...]
    act = _silu(_chunk_conv(ext, raw, conv0_ref, cw_ref, cb_ref, first))
    q, k, v = _blockdiag_qkv(act, raw, wq_ref, wk_ref, wv_ref)

    tril, _ = _tri_masks(rows)
    tri = jnp.where(tril, 1.0, 0.0).astype(BF16)
    logi = sm_ref[...] + bias_ref[...]
    bcum = _tri_cumsum(tri, _log_sigmoid(logi))
    bcum_t = bcum.T
    logi_t = logi.T
    m_all = mst[...]
    lane1 = lax.broadcasted_iota(jnp.int32, (1, LANES), 1)
    hd = MLSTM_HEAD_DIM

    hs = []
    for h in range(MLSTM_HEADS):
        qh, kh, vh = (t[:, h * hd:(h + 1) * hd] for t in (q, k, v))
        bc = bcum[:, SMALL_F + h:SMALL_F + h + 1]
        bct = bcum_t[SMALL_F + h:SMALL_F + h + 1, :]
        lit = logi_t[SMALL_I + h:SMALL_I + h + 1, :]
        li = logi[:, SMALL_I + h:SMALL_I + h + 1]
        m_h = m_all[:, h:h + 1]
        c_h = cst[h * hd:(h + 1) * hd, :]
        n_h = nst[:, h * hd:(h + 1) * hd]
        dlog = jnp.where(tril, bc - bct + lit, NEG)
        inter = bc + m_h
        m_t = jnp.maximum(inter, jnp.max(dlog, axis=1, keepdims=True))
        s = _dot_nt(qh, kh) * jnp.exp(dlog - m_t)
        w_inter = jnp.exp(inter - m_t)
        num = _dot(s, vh) + w_inter * _dot(qh, c_h)
        den = jnp.sum(s, axis=1, keepdims=True) + w_inter * jnp.sum(qh * n_h, axis=1, keepdims=True)
        hs.append(num / jnp.maximum(jnp.abs(den), jnp.exp(-m_t)))
        b_end = bc[rows - 1:rows, :]
        wlog = b_end - bc + li
        m_new = jnp.maximum(b_end + m_h, jnp.max(wlog, axis=0, keepdims=True))
        dc = jnp.exp(b_end + m_h - m_new)
        kws = kh * jnp.exp(wlog - m_new)
        cst[h * hd:(h + 1) * hd, :] = dc * c_h + _dot_tn(kws, vh)
        nst[:, h * hd:(h + 1) * hd] = dc * n_h + jnp.sum(kws, axis=0, keepdims=True)
        m_all = jnp.where(lane1 == h, m_new, m_all)
    mst[...] = m_all

    hm = _group_rms(jnp.concatenate(hs, axis=1), hd) * nw_ref[...]
    h_ref[...] = (hm + skip_ref[...] * act) * _sigmoid(o_ref[...])

    @pl.when(c == pl.num_programs(1) - 1)
    def _():
        c_ref[...] = cst[...]
        n_ref[...] = nst[...]
        m_ref[...] = mst[...]


def _rwkv_kernel(r_ref, lw_ref, k_ref, v_ref, kk_ref, a_ref, s0_ref, rk_ref, lnw_ref, lnb_ref,
                 seg_ref, segt_ref, y_ref, s_ref, sblk):
    c = pl.program_id(1)
    rows = r_ref.shape[0]

    @pl.when(c == 0)
    def _():
        sblk[...] = s0_ref[...]

    r, lw, k, v, kk, a = (t[...] for t in (r_ref, lw_ref, k_ref, v_ref, kk_ref, a_ref))
    tril, strict = _tri_masks(rows)
    tri = jnp.where(tril, 1.0, 0.0).astype(BF16)
    cum = _tri_cumsum(tri, lw)
    last = cum[rows - 1:rows, :]
    einv = jnp.exp(-cum)
    eend = jnp.exp(last - cum)
    ka = kk * a
    at = -(kk * jnp.exp(cum - lw))
    rt = r * jnp.exp(cum)
    bt = ka * einv
    kt = k * einv
    bh = ka * eend
    kh = k * eend
    wl = jnp.exp(last)
    half = lax.broadcasted_iota(jnp.int32, (rows, LANES), 1) < RWKV_HEAD_DIM
    steps = rows.bit_length() - 1

    def two(x):
        return [jnp.where(half, x, 0.0), jnp.where(half, 0.0, x)]

    npairs = RWKV_HEADS // 2
    tril4 = jnp.concatenate([tril] * 4, axis=1)
    strict4 = jnp.concatenate([strict] * 4, axis=1)
    lanes = [slice(p * LANES, (p + 1) * LANES) for p in range(npairs)]

    sps, ybase, us, pws, w4s, vss = [], [], [], [], [], []
    for sl in lanes:
        a2 = jnp.concatenate([at[:, sl], rt[:, sl]], axis=0).astype(BF16)
        rb = jnp.concatenate(two(bt[:, sl]) + two(kt[:, sl]), axis=0).astype(BF16)
        sp = sblk[sl, :]
        pbk = _dot_nt(a2, rb)
        xs = _dot_nt(a2, sp)
        vs = two(v[:, sl])
        top = jnp.where(strict4, pbk[0:rows, :], 0.0)
        us.append(xs[0:rows, :] + _dot(top[:, 2 * rows:4 * rows], jnp.concatenate(vs, axis=0)))
        pws.append([top[:, e * rows:(e + 1) * rows].astype(BF16) for e in range(2)])
        sps.append(sp)
        ybase.append(xs[rows:2 * rows, :])
        vss.append(vs)
        w4s.append(jnp.where(tril4, pbk[rows:2 * rows, :], 0.0).astype(BF16))

    for j in range(steps):
        for p in range(npairs):
            u2 = jnp.concatenate(two(us[p]), axis=0)
            us[p] = us[p] + _dot(jnp.concatenate(pws[p], axis=1), u2)
        if j < steps - 1:
            for p in range(npairs):
                pws[p] = [_dot(pw, pw).astype(BF16) for pw in pws[p]]

    ys = []
    for p, sl in enumerate(lanes):
        r4 = jnp.concatenate(two(us[p]) + vss[p], axis=0).astype(BF16)
        ys.append(ybase[p] + _dot(w4s[p], r4))
        rh = jnp.concatenate(two(bh[:, sl]) + two(kh[:, sl]), axis=0)
        sblk[sl, :] = sps[p] * wl[:, sl] + _dot_tn(r4, rh)
    y = jnp.concatenate(ys, axis=1)
    y_ref[...] = _rwkv_head_norm(y, r, k, v, rk_ref, lnw_ref, lnb_ref, seg_ref, segt_ref)

    @pl.when(c == pl.num_programs(1) - 1)
    def _():
        s_ref[...] = sblk[...]


def _ssd_step_kernel(xbc_ref, t0_ref, t1_ref, t2_ref, z_ref, sm_ref, st_ref, cw_ref, cb_ref, bias_ref,
                     alog_ref, dexp_ref, nw_ref, segt_ref, y_ref, sto_ref):
    nseq = xbc_ref.shape[0]
    act = _silu(_step_conv(xbc_ref, (t0_ref, t1_ref, t2_ref), cw_ref, cb_ref))
    xs = act[:, 0:SSD_D_INNER]
    gn = SSD_GROUPS * SSD_D_STATE
    bm = act[:, SSD_D_INNER:SSD_D_INNER + gn]
    cm = act[:, SSD_D_INNER + gn:SSD_D_INNER + 2 * gn]
    lane = lax.broadcasted_iota(jnp.int32, (nseq, LANES), 1)
    head_lane = (lane >= SMALL_DT) & (lane < SMALL_DT + SSD_HEADS)
    dt = jnp.where(head_lane, _softplus(sm_ref[...] + bias_ref[...]), 0.0)
    segt = segt_ref[...]
    dec_t = _dot_exact_rhs(jnp.exp(dt * (-jnp.exp(alog_ref[...]))), segt).T
    xdt_t = (xs * _dot_exact_rhs(dt, segt)).T
    gw = SSD_GROUP_W
    rowi = lax.broadcasted_iota(jnp.int32, (nseq, gw), 0)
    ys = [jnp.zeros((nseq, gw), F32) for _ in range(SSD_GROUPS)]
    for s in range(nseq):
        for g in range(SSD_GROUPS):
            rs = slice(g * gw, (g + 1) * gw)
            ns = slice(g * SSD_D_STATE, (g + 1) * SSD_D_STATE)
            new = st_ref[s, rs, :] * dec_t[rs, s:s + 1] + xdt_t[rs, s:s + 1] * bm[s:s + 1, ns]
            sto_ref[s, rs, :] = new
            ys[g] = jnp.where(rowi == s, _dot_nt(cm[:, ns], new), ys[g])
    y = jnp.concatenate(ys, axis=1) + dexp_ref[...] * xs
    y_ref[...] = _group_rms(y * _silu(z_ref[...]), gw) * nw_ref[...]


def _mlstm_step_kernel(xm_ref, t0_ref, t1_ref, t2_ref, o_ref, sm_ref, n_ref, m_ref, c_ref, cw_ref, cb_ref,
                       bias_ref, wq_ref, wk_ref, wv_ref, nw_ref, skip_ref, h_ref, no_ref, mo_ref, co_ref):
    nseq = xm_ref.shape[0]
    raw = xm_ref[...]
    act = _silu(_step_conv(xm_ref, (t0_ref, t1_ref, t2_ref), cw_ref, cb_ref))
    q, k, v = _blockdiag_qkv(act, raw, wq_ref, wk_ref, wv_ref)
    logi = sm_ref[...] + bias_ref[...]
    logf = _log_sigmoid(logi)
    m_old = m_ref[...]
    m_all = m_old
    n_old = n_ref[...]
    lane = lax.broadcasted_iota(jnp.int32, (nseq, LANES), 1)
    hd = MLSTM_HEAD_DIM
    rowi = lax.broadcasted_iota(jnp.int32, (nseq, hd), 0)
    hs, ns = [], []
    for h in range(MLSTM_HEADS):
        cols = slice(h * hd, (h + 1) * hd)
        qh, kh, vh = q[:, cols], k[:, cols], v[:, cols]
        lf = logf[:, SMALL_F + h:SMALL_F + h + 1] + m_old[:, h:h + 1]
        li = logi[:, SMALL_I + h:SMALL_I + h + 1]
        m_new = jnp.maximum(lf, li)
        dc = jnp.exp(lf - m_new)
        kws = kh * jnp.exp(li - m_new)
        n_new = dc * n_old[:, cols] + kws
        den = jnp.sum(qh * n_new, axis=1, keepdims=True)
        kws_t = kws.T
        q_t = qh.T
        num = jnp.zeros((nseq, hd), F32)
        for s in range(nseq):
            c_new = dc[s:s + 1, :] * c_ref[s, cols, :] + kws_t[:, s:s + 1] * vh[s:s + 1, :]
            co_ref[s, cols, :] = c_new
            num = jnp.where(rowi == s, jnp.sum(q_t[:, s:s + 1] * c_new, axis=0, keepdims=True), num)
        hs.append(num / jnp.maximum(jnp.abs(den), jnp.exp(-m_new)))
        ns.append(n_new)
        m_all = jnp.where(lane == h, m_new, m_all)
    no_ref[...] = jnp.concatenate(ns, axis=1)
    mo_ref[...] = m_all
    hm = _group_rms(jnp.concatenate(hs, axis=1), hd) * nw_ref[...]
    h_ref[...] = (hm + skip_ref[...] * act) * _sigmoid(o_ref[...])


def _rwkv_step_kernel(r_ref, lw_ref, k_ref, v_ref, kk_ref, a_ref, s_ref, rk_ref, lnw_ref, lnb_ref,
                      seg_ref, segt_ref, y_ref, so_ref):
    nseq = r_ref.shape[0]
    r, k, v, kk = r_ref[...], k_ref[...], v_ref[...], kk_ref[...]
    w = jnp.exp(lw_ref[...])
    ka = kk * a_ref[...]
    v_t = v.T
    lane = lax.broadcasted_iota(jnp.int32, (RWKV_D, LANES), 1)
    y_t = jnp.zeros((RWKV_D, LANES), F32)
    n = RWKV_HEAD_DIM

    def head_rows(x, s):
        return jnp.concatenate([jnp.broadcast_to(x[s:s + 1, h * n:(h + 1) * n], (n, n))
                                for h in range(RWKV_HEADS)], axis=0)

    for s in range(nseq):
        st = s_ref[s]
        sa = jnp.sum(st * head_rows(kk, s), axis=1, keepdims=True)
        new = st * head_rows(w, s) - sa * head_rows(ka, s) + v_t[:, s:s + 1] * head_rows(k, s)
        so_ref[s] = new
        y_t = jnp.where(lane == s, jnp.sum(new * head_rows(r, s), axis=1, keepdims=True), y_t)
    y = y_t.T[0:nseq, :]
    y_ref[...] = _rwkv_head_norm(y, r, k, v, rk_ref, lnw_ref, lnb_ref, seg_ref, segt_ref)


def _params(sem):
    return pltpu.CompilerParams(dimension_semantics=sem, vmem_limit_bytes=VMEM_LIMIT)


def _call_name(kernel, n):
    fn = getattr(kernel, "func", kernel)
    return f"{fn.__name__.strip('_')}_{n}"


def _const_spec(shape, grid_rank):
    nd = len(shape)
    if grid_rank == 1:
        return pl.BlockSpec(shape, lambda i: (0,) * nd)
    return pl.BlockSpec(shape, lambda b, c: (0,) * nd)


def _rowwise_call(kernel, row_ins, const_ins, out_widths, m):
    tm = ROW_TILE if m % ROW_TILE == 0 else m
    rows = lambda n: pl.BlockSpec((tm, n), lambda i: (i, 0))
    return pl.pallas_call(
        kernel,
        name=_call_name(kernel, m),
        grid=(m // tm,),
        in_specs=[rows(a.shape[1]) for a in row_ins] + [_const_spec(a.shape, 1) for a in const_ins],
        out_specs=[rows(n) for n in out_widths],
        out_shape=[jax.ShapeDtypeStruct((m, n), F32) for n in out_widths],
        compiler_params=_params(("parallel",)),
    )(*row_ins, *const_ins)


def _scan_call(kernel, seq_ins, state_ins, const_ins, seq_out_widths, state_out_shapes, scratch):
    bsz, t = seq_ins[0].shape[:2]
    seq = lambda n: pl.BlockSpec((None, SCAN_CHUNK, n), lambda b, c: (b, c, 0))
    per_seq = lambda shape: pl.BlockSpec((None,) + tuple(shape[1:]), lambda b, c: (b,) + (0,) * (len(shape) - 1))
    return pl.pallas_call(
        kernel,
        name=_call_name(kernel, bsz),
        grid=(bsz, t // SCAN_CHUNK),
        in_specs=([seq(a.shape[2]) for a in seq_ins] + [per_seq(a.shape) for a in state_ins]
                  + [_const_spec(a.shape, 2) for a in const_ins]),
        out_specs=[seq(n) for n in seq_out_widths] + [per_seq(s) for s in state_out_shapes],
        out_shape=([jax.ShapeDtypeStruct((bsz, t, n), F32) for n in seq_out_widths]
                   + [jax.ShapeDtypeStruct(s, F32) for s in state_out_shapes]),
        scratch_shapes=scratch,
        compiler_params=_params(("parallel", "arbitrary")),
    )(*seq_ins, *state_ins, *const_ins)


def _step_call(kernel, row_ins, state_ins, const_ins, row_out_widths, state_out_shapes):
    bsz = row_ins[0].shape[0]
    rows = lambda n: pl.BlockSpec((STEP_SEQS, n), lambda i: (i, 0))
    state = lambda shape: pl.BlockSpec((STEP_SEQS,) + tuple(shape[1:]), lambda i: (i, 0, 0))
    return pl.pallas_call(
        kernel,
        name=_call_name(kernel, bsz),
        grid=(bsz // STEP_SEQS,),
        in_specs=([rows(a.shape[1]) for a in row_ins] + [state(a.shape) for a in state_ins]
                  + [_const_spec(a.shape, 1) for a in const_ins]),
        out_specs=[rows(n) for n in row_out_widths] + [state(s) for s in state_out_shapes],
        out_shape=([jax.ShapeDtypeStruct((bsz, n), F32) for n in row_out_widths]
                   + [jax.ShapeDtypeStruct(s, F32) for s in state_out_shapes]),
        compiler_params=_params(("parallel",)),
    )(*row_ins, *state_ins, *const_ins)


def _row(v):
    return v.reshape(1, -1).astype(F32)


def _pad_cols(w, width):
    return jnp.pad(w, ((0, 0), (0, width - w.shape[1])))


def _pad_rows_to(w, rows):
    return jnp.pad(w, ((0, rows - w.shape[0]), (0, 0)))


def _blockdiag_slabs(w, scale=1.0):
    per = MLSTM_HEAD_DIM // QKV_BLOCK
    ws = (w * scale).reshape(-1, per, QKV_BLOCK, QKV_BLOCK)
    eye = jnp.eye(per, dtype=w.dtype)
    return jnp.einsum('sncd,nm->sncmd', ws, eye).reshape(-1, MLSTM_HEAD_DIM, MLSTM_HEAD_DIM).astype(BF16)


def _head_indicator():
    ch = jnp.arange(RWKV_D) // RWKV_HEAD_DIM
    seg = (ch[:, None] == jnp.arange(LANES)[None, :]).astype(BF16)
    return seg, seg.T


def _wkv_to_blocks(s):
    bsz = s.shape[0]
    n = RWKV_HEAD_DIM
    sp = s.reshape(bsz, RWKV_HEADS // 2, 2, n, n)
    eye = jnp.eye(2, dtype=s.dtype)
    return jnp.einsum('bpevk,ef->bpevfk', sp, eye).reshape(bsz, RWKV_D, 2 * n)


def _wkv_from_blocks(sb):
    bsz = sb.shape[0]
    n = RWKV_HEAD_DIM
    s6 = sb.reshape(bsz, RWKV_HEADS // 2, 2, n, 2, n)
    return jnp.stack([s6[:, :, 0, :, 0, :], s6[:, :, 1, :, 1, :]], axis=2).reshape(bsz, RWKV_HEADS, n, n)


def _mixer0_consts(p):
    ssd = [p['conv_w'][:, :XBC_DIM], p['conv_b'][:, :XBC_DIM], p['small_bias'], p['ssd_a_log'], p['ssd_d'],
           p['ssd_norm'], p['segt']]
    ml = [p['conv_w'][:, XBC_DIM:], p['conv_b'][:, XBC_DIM:], p['small_bias'], p['ml_wq'], p['ml_wk'],
          p['ml_wv'], p['ml_norm'], p['ml_skip']]
    return ssd, ml


def _mixer0_chunked(xbc, xm, z, o_pre, small, conv, ssm, mc, mn, mm, p, bsz, t):
    seq = lambda a: a.reshape(bsz, t, -1)
    tail = jnp.pad(conv, ((0, 0), (SUBLANES - (CONV_W - 1), 0), (0, 0)))
    ssd_c, ml_c = _mixer0_consts(p)
    y_ssd, new_ssm = _scan_call(
        _ssd_kernel, [seq(xbc), seq(z), seq(small)],
        [tail[:, :, :XBC_DIM], ssm.reshape(bsz, SSD_D_INNER, SSD_D_STATE)], ssd_c,
        [SSD_D_INNER], [(bsz, SSD_D_INNER, SSD_D_STATE)],
        [pltpu.VMEM((SCAN_CHUNK + SUBLANES, XBC_DIM), F32), pltpu.VMEM((SSD_D_INNER, SSD_D_STATE), F32)])
    hm, new_c, new_n, new_m = _scan_call(
        _mlstm_kernel, [seq(xm), seq(o_pre), seq(small)],
        [tail[:, :, XBC_DIM:], mc.reshape(bsz, MLSTM_D_INNER, MLSTM_HEAD_DIM), mn.reshape(bsz, 1, MLSTM_D_INNER),
         _pad_cols(mm, LANES).reshape(bsz, 1, LANES)], ml_c,
        [MLSTM_D_INNER], [(bsz, MLSTM_D_INNER, MLSTM_HEAD_DIM), (bsz, 1, MLSTM_D_INNER), (bsz, 1, LANES)],
        [pltpu.VMEM((SCAN_CHUNK + SUBLANES, MLSTM_D_INNER), F32), pltpu.VMEM((MLSTM_D_INNER, MLSTM_HEAD_DIM), F32),
         pltpu.VMEM((1, MLSTM_D_INNER), F32), pltpu.VMEM((1, LANES), F32)])
    flat = lambda a: a.reshape(bsz * t, -1)
    return flat(y_ssd), flat(hm), new_ssm, new_c, new_n, new_m.reshape(bsz, LANES)


def _mixer0_step(xbc, xm, z, o_pre, small, conv, ssm, mc, mn, mm, p, bsz):
    ssd_c, ml_c = _mixer0_consts(p)
    tails = [conv[:, j, :] for j in range(CONV_W - 1)]
    y_ssd, new_ssm = _step_call(
        _ssd_step_kernel, [xbc] + [tl[:, :XBC_DIM] for tl in tails] + [z, small],
        [ssm.reshape(bsz, SSD_D_INNER, SSD_D_STATE)], ssd_c, [SSD_D_INNER], [(bsz, SSD_D_INNER, SSD_D_STATE)])
    hm, new_n, new_m, new_c = _step_call(
        _mlstm_step_kernel, [xm] + [tl[:, XBC_DIM:] for tl in tails]
        + [o_pre, small, mn.reshape(bsz, MLSTM_D_INNER), _pad_cols(mm, LANES)],
        [mc.reshape(bsz, MLSTM_D_INNER, MLSTM_HEAD_DIM)], ml_c,
        [MLSTM_D_INNER, MLSTM_D_INNER, LANES], [(bsz, MLSTM_D_INNER, MLSTM_HEAD_DIM)])
    return y_ssd, hm, new_ssm, new_c, new_n, new_m


def _trunk(x, conv, ssm, mc, mn, mm, shift, wkv, p):
    bsz, t, d = x.shape
    m = bsz * t
    step = t == 1
    assert step or t % SCAN_CHUNK == 0, "a group is either single-token or a multiple of the scan chunk"
    assert not step or bsz % STEP_SEQS == 0
    x2 = x.reshape(m, d)

    z, xbc, xm, o_pre, small = _rowwise_call(
        _in0_kernel, [x2], [p['norm_mix0'], p['w_in0']],
        [SSD_D_INNER, XBC_DIM, MLSTM_D_INNER, MLSTM_D_INNER, LANES], m)
    conv_in = jnp.concatenate([xbc.reshape(bsz, t, -1), xm.reshape(bsz, t, -1)], axis=-1)
    new_conv = jnp.concatenate([conv, conv_in], axis=1)[:, -(CONV_W - 1):]
    if step:
        y_ssd, hm, new_ssm, new_c, new_n, new_m = _mixer0_step(xbc, xm, z, o_pre, small, conv, ssm, mc, mn, mm, p, bsz)
    else:
        y_ssd, hm, new_ssm, new_c, new_n, new_m = _mixer0_chunked(xbc, xm, z, o_pre, small, conv, ssm, mc, mn, mm,
                                                                  p, bsz, t)
    (x1,) = _rowwise_call(_out0_kernel, [y_ssd, hm, x2], [p['w_out0']], [d], m)
    xn1, x2b = _rowwise_call(functools.partial(_ffn_kernel, keep_x=True), [x1],
                             [p['norm_ffn0'], p['ffn_gu0'], p['ffn_d0'], p['norm_mix1']], [d, d], m)

    xn3 = xn1.reshape(bsz, t, d)
    xprev = jnp.concatenate([shift[:, None, :], xn3[:, :-1]], axis=1).reshape(m, d)
    rw = _rowwise_call(
        _rw_in_kernel, [xn1, xprev],
        [p['rw_mu'], p['rw_wr'], p['rw_wk'], p['rw_wv'], p['rw_w1'], p['rw_w2'], p['rw_a1'], p['rw_a2'],
         p['rw_g1'], p['rw_g2'], p['rw_w0'], p['rw_a0'], p['rw_k_k'], p['rw_k_a'], p['seg'], p['segt']], [d] * 7, m)
    g = rw[6]
    rw_c = [p['rw_r_k'], p['rw_ln_w'], p['rw_ln_b'], p['seg'], p['segt']]
    if step:
        y_rw, new_wkv = _step_call(_rwkv_step_kernel, list(rw[:6]), [wkv.reshape(bsz, RWKV_D, RWKV_HEAD_DIM)], rw_c,
                                   [d], [(bsz, RWKV_D, RWKV_HEAD_DIM)])
        new_wkv = new_wkv.reshape(wkv.shape)
    else:
        y_rw, new_wkv = _scan_call(
            _rwkv_kernel, [a.reshape(bsz, t, d) for a in rw[:6]], [_wkv_to_blocks(wkv)], rw_c,
            [d], [(bsz, RWKV_D, 2 * RWKV_HEAD_DIM)], [pltpu.VMEM((RWKV_D, 2 * RWKV_HEAD_DIM), F32)])
        y_rw, new_wkv = y_rw.reshape(m, d), _wkv_from_blocks(new_wkv)
    (x3,) = _rowwise_call(_rw_out_kernel, [y_rw, g, x2b], [p['rw_wo']], [d], m)
    (y,) = _rowwise_call(functools.partial(_ffn_kernel, keep_x=False), [x3],
                         [p['norm_ffn1'], p['ffn_gu1'], p['ffn_d1'], p['norm_final']], [d], m)

    states = (new_conv, new_ssm.reshape(ssm.shape), new_c.reshape(mc.shape), new_n.reshape(mn.shape),
              new_m[:, :MLSTM_HEADS], xn3[:, -1], new_wkv)
    return y.reshape(bsz, t, d), states


def kernel(x_prompt, x_sample, state_conv, state_ssm, state_mlstm_c, state_mlstm_n, state_mlstm_m, state_shift, state_wkv, norm_mix, norm_ffn, norm_final, w_in0, conv_w, conv_b, ssd_dt_bias, ssd_a_log, ssd_d, ssd_norm, ml_wq, ml_wk, ml_wv, ml_i_bias, ml_f_bias, ml_norm, ml_skip, w_out0, rw_mu, rw_wr, rw_wk, rw_wv, rw_wo, rw_w0, rw_w1, rw_w2, rw_a0, rw_a1, rw_a2, rw_g1, rw_g2, rw_k_k, rw_k_a, rw_r_k, rw_ln_w, rw_ln_b, ffn_w_gate_up, ffn_w_down):
    assert norm_mix.shape[0] == 2 and w_in0.shape[0] == 1 and rw_wr.shape[0] == 1, "two layers: SSD|mLSTM then RWKV-7"
    s1 = SSD_D_INNER
    s2 = s1 + XBC_DIM + MLSTM_D_INNER
    s3 = s2 + SSD_HEADS
    s4 = s3 + MLSTM_D_INNER
    s5 = s4 + MLSTM_HEADS
    w0 = w_in0[0]
    small_w = _pad_cols(jnp.concatenate([w0[:, s2:s3], w0[:, s4:s5], w0[:, s5:]], axis=1), LANES)
    seg, segt = _head_indicator()
    lora = lambda w1, w2, width: (_pad_cols(w1, width).astype(BF16), _pad_rows_to(w2, width).astype(BF16))
    rw_w1p, rw_w2p = lora(rw_w1[0], rw_w2[0], LANES)
    rw_a1p, rw_a2p = lora(rw_a1[0], rw_a2[0], LANES)
    rw_g1p, rw_g2p = lora(rw_g1[0], rw_g2[0], 2 * LANES)
    p = dict(
        norm_mix0=_row(norm_mix[0]), norm_mix1=_row(norm_mix[1]), norm_ffn0=_row(norm_ffn[0]),
        norm_ffn1=_row(norm_ffn[1]), norm_final=_row(norm_final),
        w_in0=jnp.concatenate([w0[:, :s2], w0[:, s3:s4], small_w], axis=1).astype(BF16),
        conv_w=conv_w[0], conv_b=_row(conv_b[0]),
        small_bias=_pad_cols(jnp.concatenate([_row(ssd_dt_bias[0]), _row(ml_i_bias[0]), _row(ml_f_bias[0])], axis=1),
                             LANES),
        ssd_a_log=_pad_cols(_row(ssd_a_log[0]), LANES), ssd_d=_row(jnp.repeat(ssd_d[0], SSD_HEAD_DIM)),
        ssd_norm=_row(ssd_norm[0]),
        ml_wq=_blockdiag_slabs(ml_wq[0]), ml_wk=_blockdiag_slabs(ml_wk[0], MLSTM_HEAD_DIM ** -0.5),
        ml_wv=_blockdiag_slabs(ml_wv[0]), ml_norm=_row(ml_norm[0]), ml_skip=_row(ml_skip[0]),
        w_out0=w_out0[0].astype(BF16),
        rw_mu=_pad_rows_to(rw_mu[0], SUBLANES),
        rw_wr=rw_wr[0].astype(BF16), rw_wk=rw_wk[0].astype(BF16), rw_wv=rw_wv[0].astype(BF16),
        rw_wo=rw_wo[0].astype(BF16), rw_w0=_row(rw_w0[0]), rw_a0=_row(rw_a0[0]),
        rw_w1=rw_w1p, rw_w2=rw_w2p, rw_a1=rw_a1p, rw_a2=rw_a2p, rw_g1=rw_g1p, rw_g2=rw_g2p,
        rw_k_k=_row(rw_k_k[0]), rw_k_a=_row(rw_k_a[0]), rw_r_k=_row(rw_r_k[0]),
        rw_ln_w=_row(rw_ln_w[0]), rw_ln_b=_row(rw_ln_b[0]),
        ffn_gu0=ffn_w_gate_up[0].astype(BF16), ffn_gu1=ffn_w_gate_up[1].astype(BF16),
        ffn_d0=ffn_w_down[0].astype(BF16), ffn_d1=ffn_w_down[1].astype(BF16),
        seg=seg, segt=segt,
    )
    bp = x_prompt.shape[0]
    zeros = lambda s: jnp.zeros((bp,) + s.shape[2:], F32)
    y_p, st_p = _trunk(x_prompt, zeros(state_conv), zeros(state_ssm), zeros(state_mlstm_c),
                       zeros(state_mlstm_n), zeros(state_mlstm_m), zeros(state_shift), zeros(state_wkv), p)
    y_s, st_s = _trunk(x_sample, state_conv[0], state_ssm[0], state_mlstm_c[0], state_mlstm_n[0],
                       state_mlstm_m[0], state_shift[0], state_wkv[0], p)
    out = [y_p, y_s]
    for a, b in zip(st_p, st_s):
        out += [a[None], b[None]]
    return tuple(out)
```

```python
import functools

import jax
import jax.numpy as jnp
from jax import lax
from jax.experimental import pallas as pl
from jax.experimental.pallas import tpu as pltpu

F32 = jnp.float32
BF16 = jnp.bfloat16

D_MODEL = 1024
NORM_EPS = 1e-5
CONV_W = 4
SSD_HEADS = 16
SSD_HEAD_DIM = 64
SSD_GROUPS = 2
SSD_D_STATE = 128
SSD_D_INNER = SSD_HEADS * SSD_HEAD_DIM
SSD_GROUP_W = SSD_D_INNER // SSD_GROUPS
XBC_DIM = SSD_D_INNER + 2 * SSD_GROUPS * SSD_D_STATE
MLSTM_HEADS = 4
MLSTM_HEAD_DIM = 256
MLSTM_D_INNER = MLSTM_HEADS * MLSTM_HEAD_DIM
QKV_BLOCK = 4
RWKV_HEADS = 16
RWKV_HEAD_DIM = 64
RWKV_D = RWKV_HEADS * RWKV_HEAD_DIM
RWKV_LN_EPS = 64e-5
D_FF = 2816
FF_CHUNK = 256
LANES = 128
SUBLANES = 8
SMALL_DT, SMALL_I, SMALL_F = 0, 16, 20
NEG = -1e30
SCAN_CHUNK = 128
STEP_SEQS = SUBLANES
ROW_TILE = 256
WIDE_ROW_TILE = 512
PREV_ROWS = 16
VMEM_LIMIT = 56 * 1024 * 1024


def _dot(a, b):
    return jnp.dot(a.astype(BF16), b.astype(BF16), preferred_element_type=F32)


def _dot_nt(a, b):
    return lax.dot_general(a.astype(BF16), b.astype(BF16), (((1,), (1,)), ((), ())),
                           preferred_element_type=F32)


def _dot_tn(a, b):
    return lax.dot_general(a.astype(BF16), b.astype(BF16), (((0,), (0,)), ((), ())),
                           preferred_element_type=F32)


def _split(x, terms):
    parts = []
    rem = x
    for _ in range(terms):
        p = rem.astype(BF16)
        parts.append(p)
        rem = rem - p.astype(F32)
    return parts


def _dot_exact_rhs(x, m, terms=2):
    out = None
    for p in _split(x, terms):
        t = jnp.dot(p, m, preferred_element_type=F32)
        out = t if out is None else out + t
    return out


def _tri_cumsum(tri, x):
    out = None
    for p in _split(x, 3):
        t = jnp.dot(tri, p, preferred_element_type=F32)
        out = t if out is None else out + t
    return out


def _rms_rows(x, g):
    return x * lax.rsqrt(jnp.mean(x * x, axis=-1, keepdims=True) + NORM_EPS) * g


def _group_rms(x, width):
    outs = []
    for lo in range(0, x.shape[1], width):
        grp = x[:, lo:lo + width]
        outs.append(grp * lax.rsqrt(jnp.mean(grp * grp, axis=-1, keepdims=True) + NORM_EPS))
    return jnp.concatenate(outs, axis=1)


def _sigmoid(x):
    return 1.0 / (1.0 + jnp.exp(-x))


def _silu(x):
    return x * _sigmoid(x)


def _softplus(x):
    return jnp.maximum(x, 0.0) + jnp.log(1.0 + jnp.exp(-jnp.abs(x)))


def _log_sigmoid(x):
    return jnp.minimum(x, 0.0) - jnp.log(1.0 + jnp.exp(-jnp.abs(x)))


def _tri_masks(n):
    row = lax.broadcasted_iota(jnp.int32, (n, n), 0)
    col = lax.broadcasted_iota(jnp.int32, (n, n), 1)
    return row >= col, row > col


def _chunk_conv(ext, u, conv0_ref, cw_ref, cb_ref, first):
    rows = u.shape[0]

    @pl.when(first)
    def _():
        ext[0:SUBLANES, :] = conv0_ref[...]

    @pl.when(jnp.logical_not(first))
    def _():
        ext[0:SUBLANES, :] = ext[rows:rows + SUBLANES, :]

    ext[SUBLANES:SUBLANES + rows, :] = u
    conv = cb_ref[...]
    base = SUBLANES - (CONV_W - 1)
    for j in range(CONV_W):
        conv = conv + ext[base + j:base + j + rows, :] * cw_ref[j:j + 1, :]
    return conv


def _step_conv(u_ref, tail_refs, cw_ref, cb_ref):
    conv = cb_ref[...] + u_ref[...] * cw_ref[CONV_W - 1:CONV_W, :]
    for j, t in enumerate(tail_refs):
        conv = conv + t[...] * cw_ref[j:j + 1, :]
    return conv


def _blockdiag_qkv(act, raw, wq_ref, wk_ref, wv_ref):
    hd = MLSTM_HEAD_DIM
    cat = lambda x, w: jnp.concatenate(
        [_dot(x[:, s * hd:(s + 1) * hd], w[s]) for s in range(MLSTM_HEADS)], axis=1)
    return cat(act, wq_ref), cat(act, wk_ref), cat(raw, wv_ref)


def _rwkv_head_norm(y, r, k, v, rk_ref, lnw_ref, lnb_ref, seg_ref, segt_ref):
    seg = seg_ref[...]
    segt = segt_ref[...]
    head_sum = lambda x: _dot_exact_rhs(_dot_exact_rhs(x, seg), segt)
    inv_n = 1.0 / RWKV_HEAD_DIM
    dlt = y - head_sum(y) * inv_n
    yn = dlt * lax.rsqrt(head_sum(dlt * dlt) * inv_n + RWKV_LN_EPS)
    return yn * lnw_ref[...] + lnb_ref[...] + head_sum(r * k * rk_ref[...]) * v


def _in0_kernel(x_ref, g_ref, w_ref, z_ref, xbc_ref, xm_ref, o_ref, sm_ref):
    xn = _rms_rows(x_ref[...], g_ref[...]).astype(BF16)
    off = 0
    for ref in (z_ref, xbc_ref, xm_ref, o_ref, sm_ref):
        n = ref.shape[-1]
        ref[...] = jnp.dot(xn, w_ref[:, off:off + n], preferred_element_type=F32).astype(ref.dtype)
        off += n


def _swiglu_residual(x, g_ref, wgu_ref, wd_ref):
    xn = _rms_rows(x, g_ref[...]).astype(BF16)
    acc = None
    for c in range(D_FF // FF_CHUNK):
        lo = c * FF_CHUNK
        gate = jnp.dot(xn, wgu_ref[:, lo:lo + FF_CHUNK], preferred_element_type=F32)
        up = jnp.dot(xn, wgu_ref[:, D_FF + lo:D_FF + lo + FF_CHUNK], preferred_element_type=F32)
        part = _dot(_silu(gate) * up, wd_ref[lo:lo + FF_CHUNK, :])
        acc = part if acc is None else acc + part
    return x + acc


def _mix0_ffn_kernel(y_ref, h_ref, x_ref, wo_ref, g_ref, wgu_ref, wd_ref, pg_ref, xn_ref, xo_ref):
    mixed = jnp.concatenate([y_ref[...].astype(BF16), h_ref[...].astype(BF16)], axis=1)
    x1 = x_ref[...] + jnp.dot(mixed, wo_ref[...], preferred_element_type=F32)
    x2 = _swiglu_residual(x1, g_ref, wgu_ref, wd_ref)
    xo_ref[...] = x2
    xn_ref[...] = _rms_rows(x2, pg_ref[...]).astype(xn_ref.dtype)


def _mix1_ffn_kernel(y_ref, gate_ref, x_ref, wo_ref, g_ref, wgu_ref, wd_ref, pg_ref, o_ref):
    x3 = x_ref[...] + _dot(y_ref[...].astype(F32) * gate_ref[...].astype(F32), wo_ref[...])
    o_ref[...] = _rms_rows(_swiglu_residual(x3, g_ref, wgu_ref, wd_ref), pg_ref[...])


def _rw_in_body(xn, xp, mu_ref, wr_ref, wk_ref, wv_ref, w1_ref, w2_ref, a1_ref, a2_ref,
                g1_ref, g2_ref, w0_ref, a0_ref, kk_w_ref, ka_w_ref, seg_ref, segt_ref,
                r_ref, lw_ref, k_ref, v_ref, kk_ref, a_ref, g_ref):
    xx = xp - xn
    xr, xw, xk, xv, xa, xg = (xn + xx * mu_ref[c:c + 1, :] for c in range(6))
    r_ref[...] = _dot(xr, wr_ref[...]).astype(r_ref.dtype)
    k = _dot(xk, wk_ref[...])
    v_ref[...] = _dot(xv, wv_ref[...]).astype(v_ref.dtype)
    w = -_softplus(-(w0_ref[...] + _dot(jnp.tanh(_dot(xw, w1_ref[...])), w2_ref[...]))) - 0.5
    lw_ref[...] = -jnp.exp(w)
    a = _sigmoid(a0_ref[...] + _dot(_dot(xa, a1_ref[...]), a2_ref[...]))
    a_ref[...] = a.astype(a_ref.dtype)
    g_ref[...] = _dot(_sigmoid(_dot(xg, g1_ref[...])), g2_ref[...]).astype(g_ref.dtype)
    kk = k * kk_w_ref[...]
    ss = _dot_exact_rhs(_dot_exact_rhs(kk * kk, seg_ref[...]), segt_ref[...])
    kk_ref[...] = (kk / jnp.maximum(jnp.sqrt(ss), 1e-12)).astype(kk_ref.dtype)
    k_ref[...] = (k * (1.0 + (a - 1.0) * ka_w_ref[...])).astype(k_ref.dtype)


def _rw_in_kernel(xn_ref, xp_ref, *refs):
    _rw_in_body(xn_ref[...].astype(F32), xp_ref[...].astype(F32), *refs)


def _rw_in_shift_kernel(xn_ref, prev_ref, shift_ref, *refs, tiles_per_seq):
    xn = xn_ref[...].astype(F32)
    starts_seq = pl.program_id(0) % tiles_per_seq == 0
    last = prev_ref.shape[0] - 1
    prev = jnp.where(starts_seq, shift_ref[...], prev_ref[last:last + 1, :].astype(F32))
    row = lax.broadcasted_iota(jnp.int32, xn.shape, 0)
    xp = jnp.where(row == 0, prev, pltpu.roll(xn, shift=1, axis=0))
    _rw_in_body(xn, xp, *refs)


def _ssd_kernel(xbc_ref, z_ref, sm_ref, conv0_ref, st0_ref, cw_ref, cb_ref, bias_ref, alog_ref,
                dexp_ref, nw_ref, segt_ref, y_ref, st_ref, ext, state):
    c = pl.program_id(1)
    first = c == 0
    rows = xbc_ref.shape[0]

    @pl.when(first)
    def _():
        state[...] = st0_ref[...]

    act = _silu(_chunk_conv(ext, xbc_ref[...].astype(F32), conv0_ref, cw_ref, cb_ref, first))
    xs = act[:, 0:SSD_D_INNER]
    gn = SSD_GROUPS * SSD_D_STATE
    bm = act[:, SSD_D_INNER:SSD_D_INNER + gn]
    cm = act[:, SSD_D_INNER + gn:SSD_D_INNER + 2 * gn]

    tril, _ = _tri_masks(rows)
    tri = jnp.where(tril, 1.0, 0.0).astype(BF16)
    lane = lax.broadcasted_iota(jnp.int32, (rows, LANES), 1)
    head_lane = (lane >= SMALL_DT) & (lane < SMALL_DT + SSD_HEADS)
    dt = jnp.where(head_lane, _softplus(sm_ref[...] + bias_ref[...]), 0.0)
    acum = _tri_cumsum(tri, dt * (-jnp.exp(alog_ref[...])))
    acum_t = acum.T
    dt_t = dt.T
    segt = segt_ref[...]
    eac_x = _dot_exact_rhs(jnp.exp(acum), segt)
    wend_x = _dot_exact_rhs(jnp.exp(acum[rows - 1:rows, :] - acum) * dt, segt)
    xw = xs * wend_x

    half = lane < SSD_HEAD_DIM
    heads_per_group = SSD_HEADS // SSD_GROUPS
    pairs = []
    for g in range(SSD_GROUPS):
        bg = bm[:, g * SSD_D_STATE:(g + 1) * SSD_D_STATE]
        cg = cm[:, g * SSD_D_STATE:(g + 1) * SSD_D_STATE]
        cb = _dot_nt(cg, bg)
        for q in range(heads_per_group // 2):
            wts = []
            for e in range(2):
                j = g * heads_per_group + 2 * q + e
                seg = acum[:, j:j + 1] - acum_t[j:j + 1, :]
                wts.append(jnp.exp(jnp.where(tril, seg, NEG)) * cb * dt_t[j:j + 1, :])
            p = (g * heads_per_group) // 2 + q
            xp = xs[:, p * LANES:(p + 1) * LANES]
            rhs = jnp.concatenate([jnp.where(half, xp, 0.0), jnp.where(half, 0.0, xp)], axis=0)
            pairs.append(_dot(jnp.concatenate(wts, axis=1), rhs))
    y = jnp.concatenate(pairs, axis=1)

    st = state[...]
    gw = SSD_GROUP_W
    y_state = jnp.concatenate(
        [_dot_nt(cm[:, g * SSD_D_STATE:(g + 1) * SSD_D_STATE], st[g * gw:(g + 1) * gw, :])
         for g in range(SSD_GROUPS)], axis=1)
    y = y + eac_x * y_state + dexp_ref[...] * xs
    y_ref[...] = (_group_rms(y * _silu(z_ref[...].astype(F32)), gw) * nw_ref[...]).astype(y_ref.dtype)

    dcol = jnp.exp(acum_t[:, rows - 1:rows])
    for g in range(SSD_GROUPS):
        upd = _dot_tn(xw[:, g * gw:(g + 1) * gw], bm[:, g * SSD_D_STATE:(g + 1) * SSD_D_STATE])
        for h in range(heads_per_group):
            j = g * heads_per_group + h
            lo = j * SSD_HEAD_DIM
            state[lo:lo + SSD_HEAD_DIM, :] = (st[lo:lo + SSD_HEAD_DIM, :] * dcol[j:j + 1, :]
                                              + upd[h * SSD_HEAD_DIM:(h + 1) * SSD_HEAD_DIM, :])

    @pl.when(c == pl.num_programs(1) - 1)
    def _():
        st_ref[...] = state[...]


def _mlstm_kernel(xm_ref, o_ref, sm_ref, conv0_ref, c0_ref, n0_ref, m0_ref, cw_ref, cb_ref, bias_ref,
                  wq_ref, wk_ref, wv_ref, nw_ref, skip_ref, h_ref, c_ref, n_ref, m_ref,
                  ext, cst, nst, mst):
    c = pl.program_id(1)
    first = c == 0
    rows = xm_ref.shape[0]

    @pl.when(first)
    def _():
        cst[...] = c0_ref[...]
        nst[...] = n0_ref[...]
        mst[...] = m0_ref[...]

    raw = xm_ref[...].astype(F32)
    act = _silu(_chunk_conv(ext, raw, conv0_ref, cw_ref, cb_ref, first))
    q, k, v = _blockdiag_qkv(act, raw, wq_ref, wk_ref, wv_ref)

    tril, _ = _tri_masks(rows)
    tri = jnp.where(tril, 1.0, 0.0).astype(BF16)
    logi = sm_ref[...] + bias_ref[...]
    bcum = _tri_cumsum(tri, _log_sigmoid(logi))
    bcum_t = bcum.T
    logi_t = logi.T
    m_all = mst[...]
    lane1 = lax.broadcasted_iota(jnp.int32, (1, LANES), 1)
    hd = MLSTM_HEAD_DIM

    hs = []
    for h in range(MLSTM_HEADS):
        qh, kh, vh = (t[:, h * hd:(h + 1) * hd] for t in (q, k, v))
        bc = bcum[:, SMALL_F + h:SMALL_F + h + 1]
        bct = bcum_t[SMALL_F + h:SMALL_F + h + 1, :]
        lit = logi_t[SMALL_I + h:SMALL_I + h + 1, :]
        li = logi[:, SMALL_I + h:SMALL_I + h + 1]
        m_h = m_all[:, h:h + 1]
        c_h = cst[h * hd:(h + 1) * hd, :]
        n_h = nst[:, h * hd:(h + 1) * hd]
        dlog = jnp.where(tril, bc - bct + lit, NEG)
        inter = bc + m_h
        m_t = jnp.maximum(inter, jnp.max(dlog, axis=1, keepdims=True))
        s = _dot_nt(qh, kh) * jnp.exp(dlog - m_t)
        w_inter = jnp.exp(inter - m_t)
        num = _dot(s, vh) + w_inter * _dot(qh, c_h)
        den = jnp.sum(s, axis=1, keepdims=True) + w_inter * jnp.sum(qh * n_h, axis=1, keepdims=True)
        hs.append(num / jnp.maximum(jnp.abs(den), jnp.exp(-m_t)))
        b_end = bc[rows - 1:rows, :]
        wlog = b_end - bc + li
        m_new = jnp.maximum(b_end + m_h, jnp.max(wlog, axis=0, keepdims=True))
        dc = jnp.exp(b_end + m_h - m_new)
        kws = kh * jnp.exp(wlog - m_new)
        cst[h * hd:(h + 1) * hd, :] = dc * c_h + _dot_tn(kws, vh)
        nst[:, h * hd:(h + 1) * hd] = dc * n_h + jnp.sum(kws, axis=0, keepdims=True)
        m_all = jnp.where(lane1 == h, m_new, m_all)
    mst[...] = m_all

    hm = _group_rms(jnp.concatenate(hs, axis=1), hd) * nw_ref[...]
    h_ref[...] = ((hm + skip_ref[...] * act) * _sigmoid(o_ref[...].astype(F32))).astype(h_ref.dtype)

    @pl.when(c == pl.num_programs(1) - 1)
    def _():
        c_ref[...] = cst[...]
        n_ref[...] = nst[...]
        m_ref[...] = mst[...]


def _rwkv_kernel(r_ref, lw_ref, k_ref, v_ref, kk_ref, a_ref, s0_ref, rk_ref, lnw_ref, lnb_ref,
                 seg_ref, segt_ref, y_ref, s_ref, sblk):
    c = pl.program_id(1)
    rows = r_ref.shape[0]

    @pl.when(c == 0)
    def _():
        sblk[...] = s0_ref[...]

    r, lw, k, v, kk, a = (t[...].astype(F32) for t in (r_ref, lw_ref, k_ref, v_ref, kk_ref, a_ref))
    tril, strict = _tri_masks(rows)
    tri = jnp.where(tril, 1.0, 0.0).astype(BF16)
    cum = _tri_cumsum(tri, lw)
    last = cum[rows - 1:rows, :]
    einv = jnp.exp(-cum)
    eend = jnp.exp(last - cum)
    ka = kk * a
    at = -(kk * jnp.exp(cum - lw))
    rt = r * jnp.exp(cum)
    bt = ka * einv
    kt = k * einv
    bh = ka * eend
    kh = k * eend
    wl = jnp.exp(last)
    half = lax.broadcasted_iota(jnp.int32, (rows, LANES), 1) < RWKV_HEAD_DIM
    steps = rows.bit_length() - 1

    def two(x):
        return [jnp.where(half, x, 0.0), jnp.where(half, 0.0, x)]

    npairs = RWKV_HEADS // 2
    tril4 = jnp.concatenate([tril] * 4, axis=1)
    strict4 = jnp.concatenate([strict] * 4, axis=1)
    lanes = [slice(p * LANES, (p + 1) * LANES) for p in range(npairs)]

    sps, ybase, us, pws, w4s, vss = [], [], [], [], [], []
    for sl in lanes:
        a2 = jnp.concatenate([at[:, sl], rt[:, sl]], axis=0).astype(BF16)
        rb = jnp.concatenate(two(bt[:, sl]) + two(kt[:, sl]), axis=0).astype(BF16)
        sp = sblk[sl, :]
        pbk = _dot_nt(a2, rb)
        xs = _dot_nt(a2, sp)
        vs = two(v[:, sl])
        top = jnp.where(strict4, pbk[0:rows, :], 0.0)
        us.append(xs[0:rows, :] + _dot(top[:, 2 * rows:4 * rows], jnp.concatenate(vs, axis=0)))
        pws.append([top[:, e * rows:(e + 1) * rows].astype(BF16) for e in range(2)])
        sps.append(sp)
        ybase.append(xs[rows:2 * rows, :])
        vss.append(vs)
        w4s.append(jnp.where(tril4, pbk[rows:2 * rows, :], 0.0).astype(BF16))

    for j in range(steps):
        for p in range(npairs):
            u2 = jnp.concatenate(two(us[p]), axis=0)
            us[p] = us[p] + _dot(jnp.concatenate(pws[p], axis=1), u2)
        if j < steps - 1:
            for p in range(npairs):
                pws[p] = [_dot(pw, pw).astype(BF16) for pw in pws[p]]

    ys = []
    for p, sl in enumerate(lanes):
        r4 = jnp.concatenate(two(us[p]) + vss[p], axis=0).astype(BF16)
        ys.append(ybase[p] + _dot(w4s[p], r4))
        rh = jnp.concatenate(two(bh[:, sl]) + two(kh[:, sl]), axis=0)
        sblk[sl, :] = sps[p] * wl[:, sl] + _dot_tn(r4, rh)
    y = jnp.concatenate(ys, axis=1)
    y_ref[...] = _rwkv_head_norm(y, r, k, v, rk_ref, lnw_ref, lnb_ref, seg_ref, segt_ref).astype(y_ref.dtype)

    @pl.when(c == pl.num_programs(1) - 1)
    def _():
        s_ref[...] = sblk[...]


def _ssd_step_kernel(xbc_ref, t0_ref, t1_ref, t2_ref, z_ref, sm_ref, st_ref, cw_ref, cb_ref, bias_ref,
                     alog_ref, dexp_ref, nw_ref, segt_ref, y_ref, sto_ref):
    nseq = xbc_ref.shape[0]
    act = _silu(_step_conv(xbc_ref, (t0_ref, t1_ref, t2_ref), cw_ref, cb_ref))
    xs = act[:, 0:SSD_D_INNER]
    gn = SSD_GROUPS * SSD_D_STATE
    bm = act[:, SSD_D_INNER:SSD_D_INNER + gn]
    cm = act[:, SSD_D_INNER + gn:SSD_D_INNER + 2 * gn]
    lane = lax.broadcasted_iota(jnp.int32, (nseq, LANES), 1)
    head_lane = (lane >= SMALL_DT) & (lane < SMALL_DT + SSD_HEADS)
    dt = jnp.where(head_lane, _softplus(sm_ref[...] + bias_ref[...]), 0.0)
    segt = segt_ref[...]
    dec_t = _dot_exact_rhs(jnp.exp(dt * (-jnp.exp(alog_ref[...]))), segt).T
    xdt_t = (xs * _dot_exact_rhs(dt, segt)).T
    gw = SSD_GROUP_W
    rowi = lax.broadcasted_iota(jnp.int32, (nseq, gw), 0)
    ys = [jnp.zeros((nseq, gw), F32) for _ in range(SSD_GROUPS)]
    for s in range(nseq):
        for g in range(SSD_GROUPS):
            rs = slice(g * gw, (g + 1) * gw)
            ns = slice(g * SSD_D_STATE, (g + 1) * SSD_D_STATE)
            new = st_ref[s, rs, :] * dec_t[rs, s:s + 1] + xdt_t[rs, s:s + 1] * bm[s:s + 1, ns]
            sto_ref[s, rs, :] = new
            ys[g] = jnp.where(rowi == s, _dot_nt(cm[:, ns], new), ys[g])
    y = jnp.concatenate(ys, axis=1) + dexp_ref[...] * xs
    y_ref[...] = _group_rms(y * _silu(z_ref[...]), gw) * nw_ref[...]


def _mlstm_step_kernel(xm_ref, t0_ref, t1_ref, t2_ref, o_ref, sm_ref, n_ref, m_ref, c_ref, cw_ref, cb_ref,
                       bias_ref, wq_ref, wk_ref, wv_ref, nw_ref, skip_ref, h_ref, no_ref, mo_ref, co_ref):
    nseq = xm_ref.shape[0]
    raw = xm_ref[...]
    act = _silu(_step_conv(xm_ref, (t0_ref, t1_ref, t2_ref), cw_ref, cb_ref))
    q, k, v = _blockdiag_qkv(act, raw, wq_ref, wk_ref, wv_ref)
    logi = sm_ref[...] + bias_ref[...]
    logf = _log_sigmoid(logi)
    m_old = m_ref[...]
    m_all = m_old
    n_old = n_ref[...]
    lane = lax.broadcasted_iota(jnp.int32, (nseq, LANES), 1)
    hd = MLSTM_HEAD_DIM
    rowi = lax.broadcasted_iota(jnp.int32, (nseq, hd), 0)
    hs, ns = [], []
    for h in range(MLSTM_HEADS):
        cols = slice(h * hd, (h + 1) * hd)
        qh, kh, vh = q[:, cols], k[:, cols], v[:, cols]
        lf = logf[:, SMALL_F + h:SMALL_F + h + 1] + m_old[:, h:h + 1]
        li = logi[:, SMALL_I + h:SMALL_I + h + 1]
        m_new = jnp.maximum(lf, li)
        dc = jnp.exp(lf - m_new)
        kws = kh * jnp.exp(li - m_new)
        n_new = dc * n_old[:, cols] + kws
        den = jnp.sum(qh * n_new, axis=1, keepdims=True)
        kws_t = kws.T
        q_t = qh.T
        num = jnp.zeros((nseq, hd), F32)
        for s in range(nseq):
            c_new = dc[s:s + 1, :] * c_ref[s, cols, :] + kws_t[:, s:s + 1] * vh[s:s + 1, :]
            co_ref[s, cols, :] = c_new
            num = jnp.where(rowi == s, jnp.sum(q_t[:, s:s + 1] * c_new, axis=0, keepdims=True), num)
        hs.append(num / jnp.maximum(jnp.abs(den), jnp.exp(-m_new)))
        ns.append(n_new)
        m_all = jnp.where(lane == h, m_new, m_all)
    no_ref[...] = jnp.concatenate(ns, axis=1)
    mo_ref[...] = m_all
    hm = _group_rms(jnp.concatenate(hs, axis=1), hd) * nw_ref[...]
    h_ref[...] = (hm + skip_ref[...] * act) * _sigmoid(o_ref[...])


def _rwkv_step_kernel(r_ref, lw_ref, k_ref, v_ref, kk_ref, a_ref, s_ref, rk_ref, lnw_ref, lnb_ref,
                      seg_ref, segt_ref, y_ref, so_ref):
    nseq = r_ref.shape[0]
    r, k, v, kk = r_ref[...], k_ref[...], v_ref[...], kk_ref[...]
    w = jnp.exp(lw_ref[...])
    ka = kk * a_ref[...]
    v_t = v.T
    lane = lax.broadcasted_iota(jnp.int32, (RWKV_D, LANES), 1)
    y_t = jnp.zeros((RWKV_D, LANES), F32)
    n = RWKV_HEAD_DIM

    def head_rows(x, s):
        return jnp.concatenate([jnp.broadcast_to(x[s:s + 1, h * n:(h + 1) * n], (n, n))
                                for h in range(RWKV_HEADS)], axis=0)

    for s in range(nseq):
        st = s_ref[s]
        sa = jnp.sum(st * head_rows(kk, s), axis=1, keepdims=True)
        new = st * head_rows(w, s) - sa * head_rows(ka, s) + v_t[:, s:s + 1] * head_rows(k, s)
        so_ref[s] = new
        y_t = jnp.where(lane == s, jnp.sum(new * head_rows(r, s), axis=1, keepdims=True), y_t)
    y = y_t.T[0:nseq, :]
    y_ref[...] = _rwkv_head_norm(y, r, k, v, rk_ref, lnw_ref, lnb_ref, seg_ref, segt_ref)


def _params(sem):
    return pltpu.CompilerParams(dimension_semantics=sem, vmem_limit_bytes=VMEM_LIMIT)


def _call_name(kernel, n):
    fn = getattr(kernel, "func", kernel)
    return f"{fn.__name__.strip('_')}_{n}"


def _const_spec(shape, grid_rank):
    nd = len(shape)
    if grid_rank == 1:
        return pl.BlockSpec(shape, lambda i: (0,) * nd, pipeline_mode=pl.Buffered(1))
    return pl.BlockSpec(shape, lambda b, c: (0,) * nd, pipeline_mode=pl.Buffered(1))


def _row_tile(m, tile):
    return tile if m % tile == 0 else m


def _rowwise_call(kernel, row_ins, const_ins, outs, m, tile=ROW_TILE, extra_ins=()):
    tm = _row_tile(m, tile)
    rows = lambda n: pl.BlockSpec((tm, n), lambda i: (i, 0))
    return pl.pallas_call(
        kernel,
        name=_call_name(kernel, m),
        grid=(m // tm,),
        in_specs=([rows(a.shape[1]) for a in row_ins] + [spec for _, spec in extra_ins]
                  + [_const_spec(a.shape, 1) for a in const_ins]),
        out_specs=[rows(n) for n, _ in outs],
        out_shape=[jax.ShapeDtypeStruct((m, n), dt) for n, dt in outs],
        compiler_params=_params(("parallel",)),
    )(*row_ins, *[a for a, _ in extra_ins], *const_ins)


def _scan_call(kernel, seq_ins, state_ins, const_ins, seq_out_widths, state_out_shapes, scratch):
    bsz, t = seq_ins[0].shape[:2]
    seq = lambda n: pl.BlockSpec((None, SCAN_CHUNK, n), lambda b, c: (b, c, 0))
    per_seq = lambda shape: pl.BlockSpec((None,) + tuple(shape[1:]), lambda b, c: (b,) + (0,) * (len(shape) - 1))
    return pl.pallas_call(
        kernel,
        name=_call_name(kernel, bsz),
        grid=(bsz, t // SCAN_CHUNK),
        in_specs=([seq(a.shape[2]) for a in seq_ins] + [per_seq(a.shape) for a in state_ins]
                  + [_const_spec(a.shape, 2) for a in const_ins]),
        out_specs=[seq(n) for n in seq_out_widths] + [per_seq(s) for s in state_out_shapes],
        out_shape=([jax.ShapeDtypeStruct((bsz, t, n), seq_ins[0].dtype) for n in seq_out_widths]
                   + [jax.ShapeDtypeStruct(s, F32) for s in state_out_shapes]),
        scratch_shapes=scratch,
        compiler_params=_params(("parallel", "arbitrary")),
    )(*seq_ins, *state_ins, *const_ins)


def _step_call(kernel, row_ins, state_ins, const_ins, row_out_widths, state_out_shapes):
    bsz = row_ins[0].shape[0]
    rows = lambda n: pl.BlockSpec((STEP_SEQS, n), lambda i: (i, 0))
    state = lambda shape: pl.BlockSpec((STEP_SEQS,) + tuple(shape[1:]), lambda i: (i, 0, 0))
    return pl.pallas_call(
        kernel,
        name=_call_name(kernel, bsz),
        grid=(bsz // STEP_SEQS,),
        in_specs=([rows(a.shape[1]) for a in row_ins] + [state(a.shape) for a in state_ins]
                  + [_const_spec(a.shape, 1) for a in const_ins]),
        out_specs=[rows(n) for n in row_out_widths] + [state(s) for s in state_out_shapes],
        out_shape=([jax.ShapeDtypeStruct((bsz, n), F32) for n in row_out_widths]
                   + [jax.ShapeDtypeStruct(s, F32) for s in state_out_shapes]),
        compiler_params=_params(("parallel",)),
    )(*row_ins, *state_ins, *const_ins)


def _row(v):
    return v.reshape(1, -1).astype(F32)


def _pad_cols(w, width):
    return jnp.pad(w, ((0, 0), (0, width - w.shape[1])))


def _pad_rows_to(w, rows):
    return jnp.pad(w, ((0, rows - w.shape[0]), (0, 0)))


def _blockdiag_slabs(w, scale=1.0):
    hd = MLSTM_HEAD_DIM
    rows = (w * scale).reshape(-1, hd, QKV_BLOCK)
    col = jnp.arange(hd)
    pick = (col[None, :] % QKV_BLOCK == jnp.arange(QKV_BLOCK)[:, None]).astype(w.dtype)
    tiled = jnp.einsum('srd,dc->src', rows, pick, precision=lax.Precision.HIGHEST)
    same_block = col[:, None] // QKV_BLOCK == col[None, :] // QKV_BLOCK
    return jnp.where(same_block, tiled, 0.0).astype(BF16)


def _head_indicator():
    ch = jnp.arange(RWKV_D) // RWKV_HEAD_DIM
    seg = (ch[:, None] == jnp.arange(LANES)[None, :]).astype(BF16)
    return seg, seg.T


def _wkv_to_blocks(s):
    bsz = s.shape[0]
    n = RWKV_HEAD_DIM
    sp = s.reshape(bsz, RWKV_HEADS // 2, 2, n, n)
    eye = jnp.eye(2, dtype=s.dtype)
    return jnp.einsum('bpevk,ef->bpevfk', sp, eye).reshape(bsz, RWKV_D, 2 * n)


def _wkv_from_blocks(sb):
    bsz = sb.shape[0]
    n = RWKV_HEAD_DIM
    s6 = sb.reshape(bsz, RWKV_HEADS // 2, 2, n, 2, n)
    return jnp.stack([s6[:, :, 0, :, 0, :], s6[:, :, 1, :, 1, :]], axis=2).reshape(bsz, RWKV_HEADS, n, n)


def _mixer0_consts(p):
    ssd = [p['conv_w'][:, :XBC_DIM], p['conv_b'][:, :XBC_DIM], p['small_bias'], p['ssd_a_log'], p['ssd_d'],
           p['ssd_norm'], p['segt']]
    ml = [p['conv_w'][:, XBC_DIM:], p['conv_b'][:, XBC_DIM:], p['small_bias'], p['ml_wq'], p['ml_wk'],
          p['ml_wv'], p['ml_norm'], p['ml_skip']]
    return ssd, ml


def _mixer0_chunked(xbc, xm, z, o_pre, small, conv, ssm, mc, mn, mm, p, bsz, t):
    seq = lambda a: a.reshape(bsz, t, -1)
    tail = jnp.pad(conv, ((0, 0), (SUBLANES - (CONV_W - 1), 0), (0, 0)))
    ssd_c, ml_c = _mixer0_consts(p)
    y_ssd, new_ssm = _scan_call(
        _ssd_kernel, [seq(xbc), seq(z), seq(small)],
        [tail[:, :, :XBC_DIM], ssm.reshape(bsz, SSD_D_INNER, SSD_D_STATE)], ssd_c,
        [SSD_D_INNER], [(bsz, SSD_D_INNER, SSD_D_STATE)],
        [pltpu.VMEM((SCAN_CHUNK + SUBLANES, XBC_DIM), F32), pltpu.VMEM((SSD_D_INNER, SSD_D_STATE), F32)])
    hm, new_c, new_n, new_m = _scan_call(
        _mlstm_kernel, [seq(xm), seq(o_pre), seq(small)],
        [tail[:, :, XBC_DIM:], mc.reshape(bsz, MLSTM_D_INNER, MLSTM_HEAD_DIM), mn.reshape(bsz, 1, MLSTM_D_INNER),
         _pad_cols(mm, LANES).reshape(bsz, 1, LANES)], ml_c,
        [MLSTM_D_INNER], [(bsz, MLSTM_D_INNER, MLSTM_HEAD_DIM), (bsz, 1, MLSTM_D_INNER), (bsz, 1, LANES)],
        [pltpu.VMEM((SCAN_CHUNK + SUBLANES, MLSTM_D_INNER), F32), pltpu.VMEM((MLSTM_D_INNER, MLSTM_HEAD_DIM), F32),
         pltpu.VMEM((1, MLSTM_D_INNER), F32), pltpu.VMEM((1, LANES), F32)])
    flat = lambda a: a.reshape(bsz * t, -1)
    return flat(y_ssd), flat(hm), new_ssm, new_c, new_n, new_m.reshape(bsz, LANES)


def _mixer0_step(xbc, xm, z, o_pre, small, conv, ssm, mc, mn, mm, p, bsz):
    ssd_c, ml_c = _mixer0_consts(p)
    tails = [conv[:, j, :] for j in range(CONV_W - 1)]
    y_ssd, new_ssm = _step_call(
        _ssd_step_kernel, [xbc] + [tl[:, :XBC_DIM] for tl in tails] + [z, small],
        [ssm.reshape(bsz, SSD_D_INNER, SSD_D_STATE)], ssd_c, [SSD_D_INNER], [(bsz, SSD_D_INNER, SSD_D_STATE)])
    hm, new_n, new_m, new_c = _step_call(
        _mlstm_step_kernel, [xm] + [tl[:, XBC_DIM:] for tl in tails]
        + [o_pre, small, mn.reshape(bsz, MLSTM_D_INNER), _pad_cols(mm, LANES)],
        [mc.reshape(bsz, MLSTM_D_INNER, MLSTM_HEAD_DIM)], ml_c,
        [MLSTM_D_INNER, MLSTM_D_INNER, LANES], [(bsz, MLSTM_D_INNER, MLSTM_HEAD_DIM)])
    return y_ssd, hm, new_ssm, new_c, new_n, new_m


def _trunk(x, conv, ssm, mc, mn, mm, shift, wkv, p):
    bsz, t, d = x.shape
    m = bsz * t
    step = t == 1
    assert step or t % SCAN_CHUNK == 0, "a group is either single-token or a multiple of the scan chunk"
    assert not step or bsz % STEP_SEQS == 0
    act = F32 if step else BF16
    x2 = x.reshape(m, d)

    z, xbc, xm, o_pre, small = _rowwise_call(
        _in0_kernel, [x2], [p['norm_mix0'], p['w_in0']],
        [(SSD_D_INNER, act), (XBC_DIM, act), (MLSTM_D_INNER, act), (MLSTM_D_INNER, act), (LANES, F32)], m,
        tile=WIDE_ROW_TILE)
    last_rows = lambda a: a.reshape(bsz, t, -1)[:, -(CONV_W - 1):].astype(F32)
    conv_in_tail = jnp.concatenate([last_rows(xbc), last_rows(xm)], axis=-1)
    new_conv = jnp.concatenate([conv, conv_in_tail], axis=1)[:, -(CONV_W - 1):]
    if step:
        y_ssd, hm, new_ssm, new_c, new_n, new_m = _mixer0_step(xbc, xm, z, o_pre, small, conv, ssm, mc, mn, mm, p, bsz)
    else:
        y_ssd, hm, new_ssm, new_c, new_n, new_m = _mixer0_chunked(xbc, xm, z, o_pre, small, conv, ssm, mc, mn, mm,
                                                                  p, bsz, t)
    xn1, x2b = _rowwise_call(
        _mix0_ffn_kernel, [y_ssd, hm, x2],
        [p['w_out0'], p['norm_ffn0'], p['ffn_gu0'], p['ffn_d0'], p['norm_mix1']], [(d, act), (d, F32)], m,
        tile=WIDE_ROW_TILE)

    rw_in_c = [p['rw_mu'], p['rw_wr'], p['rw_wk'], p['rw_wv'], p['rw_w1'], p['rw_w2'], p['rw_a1'], p['rw_a2'],
               p['rw_g1'], p['rw_g2'], p['rw_w0'], p['rw_a0'], p['rw_k_k'], p['rw_k_a'], p['seg'], p['segt']]
    rw_outs = [(d, act), (d, F32)] + [(d, act)] * 5
    if step:
        rw = _rowwise_call(_rw_in_kernel, [xn1, shift], rw_in_c, rw_outs, m)
    else:
        tm = _row_tile(t, ROW_TILE)
        tiles_per_seq = t // tm
        prev_spec = pl.BlockSpec((PREV_ROWS, d), lambda i: (jnp.maximum(i * (tm // PREV_ROWS) - 1, 0), 0))
        shift_spec = pl.BlockSpec((None, 1, d), lambda i: (i // tiles_per_seq, 0, 0))
        rw = _rowwise_call(functools.partial(_rw_in_shift_kernel, tiles_per_seq=tiles_per_seq), [xn1], rw_in_c,
                           rw_outs, m, tile=tm, extra_ins=[(xn1, prev_spec), (shift.reshape(bsz, 1, d), shift_spec)])
    g = rw[6]
    rw_c = [p['rw_r_k'], p['rw_ln_w'], p['rw_ln_b'], p['seg'], p['segt']]
    if step:
        y_rw, new_wkv = _step_call(_rwkv_step_kernel, list(rw[:6]), [wkv.reshape(bsz, RWKV_D, RWKV_HEAD_DIM)], rw_c,
                                   [d], [(bsz, RWKV_D, RWKV_HEAD_DIM)])
        new_wkv = new_wkv.reshape(wkv.shape)
    else:
        y_rw, new_wkv = _scan_call(
            _rwkv_kernel, [a.reshape(bsz, t, d) for a in rw[:6]], [_wkv_to_blocks(wkv)], rw_c,
            [d], [(bsz, RWKV_D, 2 * RWKV_HEAD_DIM)], [pltpu.VMEM((RWKV_D, 2 * RWKV_HEAD_DIM), F32)])
        y_rw, new_wkv = y_rw.reshape(m, d), _wkv_from_blocks(new_wkv)
    (y,) = _rowwise_call(
        _mix1_ffn_kernel, [y_rw, g, x2b],
        [p['rw_wo'], p['norm_ffn1'], p['ffn_gu1'], p['ffn_d1'], p['norm_final']], [(d, F32)], m, tile=WIDE_ROW_TILE)

    states = (new_conv, new_ssm.reshape(ssm.shape), new_c.reshape(mc.shape), new_n.reshape(mn.shape),
              new_m[:, :MLSTM_HEADS], xn1.reshape(bsz, t, d)[:, -1].astype(F32), new_wkv)
    return y.reshape(bsz, t, d), states


def kernel(x_prompt, x_sample, state_conv, state_ssm, state_mlstm_c, state_mlstm_n, state_mlstm_m, state_shift, state_wkv, norm_mix, norm_ffn, norm_final, w_in0, conv_w, conv_b, ssd_dt_bias, ssd_a_log, ssd_d, ssd_norm, ml_wq, ml_wk, ml_wv, ml_i_bias, ml_f_bias, ml_norm, ml_skip, w_out0, rw_mu, rw_wr, rw_wk, rw_wv, rw_wo, rw_w0, rw_w1, rw_w2, rw_a0, rw_a1, rw_a2, rw_g1, rw_g2, rw_k_k, rw_k_a, rw_r_k, rw_ln_w, rw_ln_b, ffn_w_gate_up, ffn_w_down):
    assert norm_mix.shape[0] == 2 and w_in0.shape[0] == 1 and rw_wr.shape[0] == 1, "two layers: SSD|mLSTM then RWKV-7"
    s1 = SSD_D_INNER
    s2 = s1 + XBC_DIM + MLSTM_D_INNER
    s3 = s2 + SSD_HEADS
    s4 = s3 + MLSTM_D_INNER
    s5 = s4 + MLSTM_HEADS
    w0 = w_in0[0]
    small_w = _pad_cols(jnp.concatenate([w0[:, s2:s3], w0[:, s4:s5], w0[:, s5:]], axis=1), LANES)
    seg, segt = _head_indicator()
    lora = lambda w1, w2, width: (_pad_cols(w1, width).astype(BF16), _pad_rows_to(w2, width).astype(BF16))
    rw_w1p, rw_w2p = lora(rw_w1[0], rw_w2[0], LANES)
    rw_a1p, rw_a2p = lora(rw_a1[0], rw_a2[0], LANES)
    rw_g1p, rw_g2p = lora(rw_g1[0], rw_g2[0], 2 * LANES)
    p = dict(
        norm_mix0=_row(norm_mix[0]), norm_mix1=_row(norm_mix[1]), norm_ffn0=_row(norm_ffn[0]),
        norm_ffn1=_row(norm_ffn[1]), norm_final=_row(norm_final),
        w_in0=jnp.concatenate([w0[:, :s2], w0[:, s3:s4], small_w], axis=1).astype(BF16),
        conv_w=conv_w[0], conv_b=_row(conv_b[0]),
        small_bias=_pad_cols(jnp.concatenate([_row(ssd_dt_bias[0]), _row(ml_i_bias[0]), _row(ml_f_bias[0])], axis=1),
                             LANES),
        ssd_a_log=_pad_cols(_row(ssd_a_log[0]), LANES), ssd_d=_row(jnp.repeat(ssd_d[0], SSD_HEAD_DIM)),
        ssd_norm=_row(ssd_norm[0]),
        ml_wq=_blockdiag_slabs(ml_wq[0]), ml_wk=_blockdiag_slabs(ml_wk[0], MLSTM_HEAD_DIM ** -0.5),
        ml_wv=_blockdiag_slabs(ml_wv[0]), ml_norm=_row(ml_norm[0]), ml_skip=_row(ml_skip[0]),
        w_out0=w_out0[0].astype(BF16),
        rw_mu=_pad_rows_to(rw_mu[0], SUBLANES),
        rw_wr=rw_wr[0].astype(BF16), rw_wk=rw_wk[0].astype(BF16), rw_wv=rw_wv[0].astype(BF16),
        rw_wo=rw_wo[0].astype(BF16), rw_w0=_row(rw_w0[0]), rw_a0=_row(rw_a0[0]),
        rw_w1=rw_w1p, rw_w2=rw_w2p, rw_a1=rw_a1p, rw_a2=rw_a2p, rw_g1=rw_g1p, rw_g2=rw_g2p,
        rw_k_k=_row(rw_k_k[0]), rw_k_a=_row(rw_k_a[0]), rw_r_k=_row(rw_r_k[0]),
        rw_ln_w=_row(rw_ln_w[0]), rw_ln_b=_row(rw_ln_b[0]),
        ffn_gu0=ffn_w_gate_up[0].astype(BF16), ffn_gu1=ffn_w_gate_up[1].astype(BF16),
        ffn_d0=ffn_w_down[0].astype(BF16), ffn_d1=ffn_w_down[1].astype(BF16),
        seg=seg, segt=segt,
    )
    bp = x_prompt.shape[0]
    zeros = lambda s: jnp.zeros((bp,) + s.shape[2:], F32)
    y_p, st_p = _trunk(x_prompt, zeros(state_conv), zeros(state_ssm), zeros(state_mlstm_c),
                       zeros(state_mlstm_n), zeros(state_mlstm_m), zeros(state_shift), zeros(state_wkv), p)
    y_s, st_s = _trunk(x_sample, state_conv[0], state_ssm[0], state_mlstm_c[0], state_mlstm_n[0],
                       state_mlstm_m[0], state_shift[0], state_wkv[0], p)
    out = [y_p, y_s]
    for a, b in zip(st_p, st_s):
        out += [a[None], b[None]]
    return tuple(out)
```

```python
import functools

import jax
import jax.numpy as jnp
from jax import lax
from jax.experimental import pallas as pl
from jax.experimental.pallas import tpu as pltpu

F32 = jnp.float32
BF16 = jnp.bfloat16

D_MODEL = 1024
NORM_EPS = 1e-5
CONV_W = 4
SSD_HEADS = 16
SSD_HEAD_DIM = 64
SSD_GROUPS = 2
SSD_D_STATE = 128
SSD_D_INNER = SSD_HEADS * SSD_HEAD_DIM
SSD_GROUP_W = SSD_D_INNER // SSD_GROUPS
XBC_DIM = SSD_D_INNER + 2 * SSD_GROUPS * SSD_D_STATE
MLSTM_HEADS = 4
MLSTM_HEAD_DIM = 256
MLSTM_D_INNER = MLSTM_HEADS * MLSTM_HEAD_DIM
QKV_BLOCK = 4
RWKV_HEADS = 16
RWKV_HEAD_DIM = 64
RWKV_D = RWKV_HEADS * RWKV_HEAD_DIM
RWKV_LN_EPS = 64e-5
D_FF = 2816
FF_CHUNK = 256
LANES = 128
SUBLANES = 8
SMALL_DT, SMALL_I, SMALL_F = 0, 16, 20
NEG = -1e30
SCAN_CHUNK = 128
STEP_SEQS = SUBLANES
ROW_TILE = 512
WIDE_ROW_TILE = 512
PREV_ROWS = 16
VMEM_LIMIT = 56 * 1024 * 1024


def _dot(a, b):
    return jnp.dot(a.astype(BF16), b.astype(BF16), preferred_element_type=F32)


def _dot_nt(a, b):
    return lax.dot_general(a.astype(BF16), b.astype(BF16), (((1,), (1,)), ((), ())),
                           preferred_element_type=F32)


def _dot_tn(a, b):
    return lax.dot_general(a.astype(BF16), b.astype(BF16), (((0,), (0,)), ((), ())),
                           preferred_element_type=F32)


def _split(x, terms):
    parts = []
    rem = x
    for _ in range(terms):
        p = rem.astype(BF16)
        parts.append(p)
        rem = rem - p.astype(F32)
    return parts


def _dot_exact_rhs(x, m, terms=2):
    out = None
    for p in _split(x, terms):
        t = jnp.dot(p, m, preferred_element_type=F32)
        out = t if out is None else out + t
    return out


def _tri_cumsum(tri, x, terms=3):
    out = None
    for p in _split(x, terms):
        t = jnp.dot(tri, p, preferred_element_type=F32)
        out = t if out is None else out + t
    return out


def _rms_rows(x, g):
    return x * lax.rsqrt(jnp.mean(x * x, axis=-1, keepdims=True) + NORM_EPS) * g


def _group_rms(x, width):
    outs = []
    for lo in range(0, x.shape[1], width):
        grp = x[:, lo:lo + width]
        outs.append(grp * lax.rsqrt(jnp.mean(grp * grp, axis=-1, keepdims=True) + NORM_EPS))
    return jnp.concatenate(outs, axis=1)


def _sigmoid(x):
    return 1.0 / (1.0 + jnp.exp(-x))


def _silu(x):
    return x * _sigmoid(x)


def _softplus(x):
    return jnp.maximum(x, 0.0) + jnp.log(1.0 + jnp.exp(-jnp.abs(x)))


def _log_sigmoid(x):
    return jnp.minimum(x, 0.0) - jnp.log(1.0 + jnp.exp(-jnp.abs(x)))


def _tri_masks(n):
    row = lax.broadcasted_iota(jnp.int32, (n, n), 0)
    col = lax.broadcasted_iota(jnp.int32, (n, n), 1)
    return row >= col, row > col


def _chunk_conv(ext, u, conv0_ref, cw_ref, cb_ref, first):
    rows = u.shape[0]

    @pl.when(first)
    def _():
        ext[0:SUBLANES, :] = conv0_ref[...]

    @pl.when(jnp.logical_not(first))
    def _():
        ext[0:SUBLANES, :] = ext[rows:rows + SUBLANES, :]

    ext[SUBLANES:SUBLANES + rows, :] = u
    e = ext[...]
    conv = cb_ref[...] + u * cw_ref[CONV_W - 1:CONV_W, :]
    for j in range(CONV_W - 1):
        back = pltpu.roll(e, shift=CONV_W - 1 - j, axis=0)[SUBLANES:SUBLANES + rows, :]
        conv = conv + back * cw_ref[j:j + 1, :]
    return conv


def _step_conv(u_ref, tail_refs, cw_ref, cb_ref):
    conv = cb_ref[...] + u_ref[...] * cw_ref[CONV_W - 1:CONV_W, :]
    for j, t in enumerate(tail_refs):
        conv = conv + t[...] * cw_ref[j:j + 1, :]
    return conv


def _blockdiag_qkv(act, raw, wq_ref, wk_ref, wv_ref):
    hd = MLSTM_HEAD_DIM
    cat = lambda x, w: jnp.concatenate(
        [_dot(x[:, s * hd:(s + 1) * hd], w[s]) for s in range(MLSTM_HEADS)], axis=1)
    return cat(act, wq_ref), cat(act, wk_ref), cat(raw, wv_ref)


def _rwkv_head_norm(y, r, k, v, rk_ref, lnw_ref, lnb_ref, seg_ref, segt_ref):
    seg = seg_ref[...]
    segt = segt_ref[...]
    head_sum = lambda x: _dot_exact_rhs(_dot_exact_rhs(x, seg), segt, terms=1)
    inv_n = 1.0 / RWKV_HEAD_DIM
    dlt = y - head_sum(y) * inv_n
    yn = dlt * lax.rsqrt(head_sum(dlt * dlt) * inv_n + RWKV_LN_EPS)
    return yn * lnw_ref[...] + lnb_ref[...] + head_sum(r * k * rk_ref[...]) * v


def _in0_kernel(x_ref, g_ref, w_ref, z_ref, xbc_ref, xm_ref, o_ref, sm_ref):
    xn = _rms_rows(x_ref[...], g_ref[...]).astype(BF16)
    off = 0
    for ref in (z_ref, xbc_ref, xm_ref, o_ref, sm_ref):
        n = ref.shape[-1]
        ref[...] = jnp.dot(xn, w_ref[:, off:off + n], preferred_element_type=F32).astype(ref.dtype)
        off += n


def _swiglu_residual(x, g_ref, wgu_ref, wd_ref):
    xn = _rms_rows(x, g_ref[...]).astype(BF16)
    acc = None
    for c in range(D_FF // FF_CHUNK):
        lo = c * FF_CHUNK
        gate = jnp.dot(xn, wgu_ref[:, lo:lo + FF_CHUNK], preferred_element_type=F32)
        up = jnp.dot(xn, wgu_ref[:, D_FF + lo:D_FF + lo + FF_CHUNK], preferred_element_type=F32)
        part = _dot(_silu(gate) * up, wd_ref[lo:lo + FF_CHUNK, :])
        acc = part if acc is None else acc + part
    return x + acc


def _mix0_ffn_kernel(y_ref, h_ref, x_ref, wo_ref, g_ref, wgu_ref, wd_ref, pg_ref, xn_ref, xo_ref):
    mixed = jnp.concatenate([y_ref[...].astype(BF16), h_ref[...].astype(BF16)], axis=1)
    x1 = x_ref[...] + jnp.dot(mixed, wo_ref[...], preferred_element_type=F32)
    x2 = _swiglu_residual(x1, g_ref, wgu_ref, wd_ref)
    xo_ref[...] = x2
    xn_ref[...] = _rms_rows(x2, pg_ref[...]).astype(xn_ref.dtype)


def _mix1_ffn_kernel(y_ref, gate_ref, x_ref, wo_ref, g_ref, wgu_ref, wd_ref, pg_ref, o_ref):
    x3 = x_ref[...] + _dot(y_ref[...].astype(F32) * gate_ref[...].astype(F32), wo_ref[...])
    o_ref[...] = _rms_rows(_swiglu_residual(x3, g_ref, wgu_ref, wd_ref), pg_ref[...])


def _rw_in_body(xn, xp, mu_ref, wr_ref, wk_ref, wv_ref, w1_ref, w2_ref, a1_ref, a2_ref,
                g1_ref, g2_ref, w0_ref, a0_ref, kk_w_ref, ka_w_ref, seg_ref, segt_ref,
                r_ref, lw_ref, k_ref, v_ref, kk_ref, a_ref, g_ref):
    xx = xp - xn
    xr, xw, xk, xv, xa, xg = (xn + xx * mu_ref[c:c + 1, :] for c in range(6))
    r_ref[...] = _dot(xr, wr_ref[...]).astype(r_ref.dtype)
    k = _dot(xk, wk_ref[...])
    v_ref[...] = _dot(xv, wv_ref[...]).astype(v_ref.dtype)
    w = -_softplus(-(w0_ref[...] + _dot(jnp.tanh(_dot(xw, w1_ref[...])), w2_ref[...]))) - 0.5
    lw_ref[...] = -jnp.exp(w)
    a = _sigmoid(a0_ref[...] + _dot(_dot(xa, a1_ref[...]), a2_ref[...]))
    a_ref[...] = a.astype(a_ref.dtype)
    g_ref[...] = _dot(_sigmoid(_dot(xg, g1_ref[...])), g2_ref[...]).astype(g_ref.dtype)
    kk = k * kk_w_ref[...]
    ss = _dot_exact_rhs(_dot_exact_rhs(kk * kk, seg_ref[...]), segt_ref[...])
    kk_ref[...] = (kk / jnp.maximum(jnp.sqrt(ss), 1e-12)).astype(kk_ref.dtype)
    k_ref[...] = (k * (1.0 + (a - 1.0) * ka_w_ref[...])).astype(k_ref.dtype)


def _rw_in_kernel(xn_ref, xp_ref, *refs):
    _rw_in_body(xn_ref[...].astype(F32), xp_ref[...].astype(F32), *refs)


def _rw_in_shift_kernel(xn_ref, prev_ref, shift_ref, *refs, tiles_per_seq):
    xn = xn_ref[...].astype(F32)
    starts_seq = pl.program_id(0) % tiles_per_seq == 0
    last = prev_ref.shape[0] - 1
    prev = jnp.where(starts_seq, shift_ref[...], prev_ref[last:last + 1, :].astype(F32))
    row = lax.broadcasted_iota(jnp.int32, xn.shape, 0)
    xp = jnp.where(row == 0, prev, pltpu.roll(xn, shift=1, axis=0))
    _rw_in_body(xn, xp, *refs)


def _ssd_kernel(xbc_ref, z_ref, sm_ref, conv0_ref, st0_ref, cw_ref, cb_ref, bias_ref, alog_ref,
                dexp_ref, nw_ref, segt_ref, y_ref, st_ref, ext, state):
    c = pl.program_id(1)
    first = c == 0
    rows = xbc_ref.shape[0]

    @pl.when(first)
    def _():
        state[...] = st0_ref[...]

    act = _silu(_chunk_conv(ext, xbc_ref[...].astype(F32), conv0_ref, cw_ref, cb_ref, first))
    xs = act[:, 0:SSD_D_INNER]
    gn = SSD_GROUPS * SSD_D_STATE
    bm = act[:, SSD_D_INNER:SSD_D_INNER + gn]
    cm = act[:, SSD_D_INNER + gn:SSD_D_INNER + 2 * gn]

    tril, _ = _tri_masks(rows)
    tri = jnp.where(tril, 1.0, 0.0).astype(BF16)
    lane = lax.broadcasted_iota(jnp.int32, (rows, LANES), 1)
    head_lane = (lane >= SMALL_DT) & (lane < SMALL_DT + SSD_HEADS)
    dt = jnp.where(head_lane, _softplus(sm_ref[...] + bias_ref[...]), 0.0)
    acum = _tri_cumsum(tri, dt * (-jnp.exp(alog_ref[...])))
    acum_t = acum.T
    dt_t = dt.T
    segt = segt_ref[...]
    eac_x = _dot_exact_rhs(jnp.exp(acum), segt)
    wend_x = _dot_exact_rhs(jnp.exp(acum[rows - 1:rows, :] - acum) * dt, segt)
    xw = xs * wend_x

    half = lane < SSD_HEAD_DIM
    heads_per_group = SSD_HEADS // SSD_GROUPS
    pairs = []
    for g in range(SSD_GROUPS):
        bg = bm[:, g * SSD_D_STATE:(g + 1) * SSD_D_STATE]
        cg = cm[:, g * SSD_D_STATE:(g + 1) * SSD_D_STATE]
        cb = _dot_nt(cg, bg)
        for q in range(heads_per_group // 2):
            wts = []
            for e in range(2):
                j = g * heads_per_group + 2 * q + e
                seg = acum[:, j:j + 1] - acum_t[j:j + 1, :]
                wts.append(jnp.exp(jnp.where(tril, seg, NEG)) * cb * dt_t[j:j + 1, :])
            p = (g * heads_per_group) // 2 + q
            xp = xs[:, p * LANES:(p + 1) * LANES]
            rhs = jnp.concatenate([jnp.where(half, xp, 0.0), jnp.where(half, 0.0, xp)], axis=0)
            pairs.append(_dot(jnp.concatenate(wts, axis=1), rhs))
    y = jnp.concatenate(pairs, axis=1)

    st = state[...]
    gw = SSD_GROUP_W
    y_state = jnp.concatenate(
        [_dot_nt(cm[:, g * SSD_D_STATE:(g + 1) * SSD_D_STATE], st[g * gw:(g + 1) * gw, :])
         for g in range(SSD_GROUPS)], axis=1)
    y = y + eac_x * y_state + dexp_ref[...] * xs
    y_ref[...] = (_group_rms(y * _silu(z_ref[...].astype(F32)), gw) * nw_ref[...]).astype(y_ref.dtype)

    dcol = jnp.exp(acum_t[:, rows - 1:rows])
    for g in range(SSD_GROUPS):
        upd = _dot_tn(xw[:, g * gw:(g + 1) * gw], bm[:, g * SSD_D_STATE:(g + 1) * SSD_D_STATE])
        for h in range(heads_per_group):
            j = g * heads_per_group + h
            lo = j * SSD_HEAD_DIM
            state[lo:lo + SSD_HEAD_DIM, :] = (st[lo:lo + SSD_HEAD_DIM, :] * dcol[j:j + 1, :]
                                              + upd[h * SSD_HEAD_DIM:(h + 1) * SSD_HEAD_DIM, :])

    @pl.when(c == pl.num_programs(1) - 1)
    def _():
        st_ref[...] = state[...]


def _mlstm_kernel(xm_ref, o_ref, sm_ref, conv0_ref, c0_ref, n0_ref, m0_ref, cw_ref, cb_ref, bias_ref,
                  wq_ref, wk_ref, wv_ref, nw_ref, skip_ref, h_ref, c_ref, n_ref, m_ref,
                  ext, cst, nst, mst):
    c = pl.program_id(1)
    first = c == 0
    rows = xm_ref.shape[0]

    @pl.when(first)
    def _():
        cst[...] = c0_ref[...]
        nst[...] = n0_ref[...]
        mst[...] = m0_ref[...]

    raw = xm_ref[...].astype(F32)
    act = _silu(_chunk_conv(ext, raw, conv0_ref, cw_ref, cb_ref, first))
    q, k, v = _blockdiag_qkv(act, raw, wq_ref, wk_ref, wv_ref)

    tril, _ = _tri_masks(rows)
    tri = jnp.where(tril, 1.0, 0.0).astype(BF16)
    logi = sm_ref[...] + bias_ref[...]
    bcum = _tri_cumsum(tri, _log_sigmoid(logi))
    bcum_t = bcum.T
    logi_t = logi.T
    m_all = mst[...]
    lane1 = lax.broadcasted_iota(jnp.int32, (1, LANES), 1)
    hd = MLSTM_HEAD_DIM

    hs = []
    for h in range(MLSTM_HEADS):
        qh, kh, vh = (t[:, h * hd:(h + 1) * hd] for t in (q, k, v))
        bc = bcum[:, SMALL_F + h:SMALL_F + h + 1]
        bct = bcum_t[SMALL_F + h:SMALL_F + h + 1, :]
        lit = logi_t[SMALL_I + h:SMALL_I + h + 1, :]
        li = logi[:, SMALL_I + h:SMALL_I + h + 1]
        m_h = m_all[:, h:h + 1]
        c_h = cst[h * hd:(h + 1) * hd, :]
        n_h = nst[:, h * hd:(h + 1) * hd]
        dlog = jnp.where(tril, bc - bct + lit, NEG)
        inter = bc + m_h
        m_t = jnp.maximum(inter, jnp.max(dlog, axis=1, keepdims=True))
        s = _dot_nt(qh, kh) * jnp.exp(dlog - m_t)
        w_inter = jnp.exp(inter - m_t)
        num = _dot(s, vh) + w_inter * _dot(qh, c_h)
        den = jnp.sum(s, axis=1, keepdims=True) + w_inter * jnp.sum(qh * n_h, axis=1, keepdims=True)
        hs.append(num / jnp.maximum(jnp.abs(den), jnp.exp(-m_t)))
        b_end = bc[rows - 1:rows, :]
        wlog = b_end - bc + li
        m_new = jnp.maximum(b_end + m_h, jnp.max(wlog, axis=0, keepdims=True))
        dc = jnp.exp(b_end + m_h - m_new)
        kws = kh * jnp.exp(wlog - m_new)
        cst[h * hd:(h + 1) * hd, :] = dc * c_h + _dot_tn(kws, vh)
        nst[:, h * hd:(h + 1) * hd] = dc * n_h + jnp.sum(kws, axis=0, keepdims=True)
        m_all = jnp.where(lane1 == h, m_new, m_all)
    mst[...] = m_all

    hm = _group_rms(jnp.concatenate(hs, axis=1), hd) * nw_ref[...]
    h_ref[...] = ((hm + skip_ref[...] * act) * _sigmoid(o_ref[...].astype(F32))).astype(h_ref.dtype)

    @pl.when(c == pl.num_programs(1) - 1)
    def _():
        c_ref[...] = cst[...]
        n_ref[...] = nst[...]
        m_ref[...] = mst[...]


def _rwkv_kernel(r_ref, lw_ref, k_ref, v_ref, kk_ref, a_ref, s0_ref, rk_ref, lnw_ref, lnb_ref,
                 seg_ref, segt_ref, y_ref, s_ref, sblk):
    c = pl.program_id(1)
    rows = r_ref.shape[0]

    @pl.when(c == 0)
    def _():
        s0 = s0_ref[...]
        shape = (RWKV_D, 2 * RWKV_HEAD_DIM)
        head_parity = (lax.broadcasted_iota(jnp.int32, shape, 0) // RWKV_HEAD_DIM) % 2
        lane_half = lax.broadcasted_iota(jnp.int32, shape, 1) // RWKV_HEAD_DIM
        sblk[...] = jnp.where(head_parity == lane_half, jnp.concatenate([s0, s0], axis=1), 0.0)

    r, lw, k, v, kk, a = (t[...].astype(F32) for t in (r_ref, lw_ref, k_ref, v_ref, kk_ref, a_ref))
    tril, strict = _tri_masks(rows)
    tri = jnp.where(tril, 1.0, 0.0).astype(BF16)
    cum = _tri_cumsum(tri, lw, terms=2)
    last = cum[rows - 1:rows, :]
    einv = jnp.exp(-cum)
    eend = jnp.exp(last - cum)
    ka = kk * a
    at = -(kk * jnp.exp(cum - lw))
    rt = r * jnp.exp(cum)
    bt = ka * einv
    kt = k * einv
    bh = ka * eend
    kh = k * eend
    wl = jnp.exp(last)
    half = lax.broadcasted_iota(jnp.int32, (rows, LANES), 1) < RWKV_HEAD_DIM
    steps = rows.bit_length() - 1

    def two(x):
        return [jnp.where(half, x, 0.0), jnp.where(half, 0.0, x)]

    npairs = RWKV_HEADS // 2
    tril4 = jnp.concatenate([tril] * 4, axis=1)
    strict4 = jnp.concatenate([strict] * 4, axis=1)
    lanes = [slice(p * LANES, (p + 1) * LANES) for p in range(npairs)]

    sps, ybase, us, pws, w4s, vss = [], [], [], [], [], []
    for sl in lanes:
        a2 = jnp.concatenate([at[:, sl], rt[:, sl]], axis=0).astype(BF16)
        rb = jnp.concatenate(two(bt[:, sl]) + two(kt[:, sl]), axis=0).astype(BF16)
        sp = sblk[sl, :]
        pbk = _dot_nt(a2, rb)
        xs = _dot_nt(a2, sp)
        vs = two(v[:, sl])
        top = jnp.where(strict4, pbk[0:rows, :], 0.0)
        us.append(xs[0:rows, :] + _dot(top[:, 2 * rows:4 * rows], jnp.concatenate(vs, axis=0)))
        pws.append([top[:, e * rows:(e + 1) * rows].astype(BF16) for e in range(2)])
        sps.append(sp)
        ybase.append(xs[rows:2 * rows, :])
        vss.append(vs)
        w4s.append(jnp.where(tril4, pbk[rows:2 * rows, :], 0.0).astype(BF16))

    for j in range(steps):
        for p in range(npairs):
            u2 = jnp.concatenate(two(us[p]), axis=0)
            us[p] = us[p] + _dot(jnp.concatenate(pws[p], axis=1), u2)
        if j < steps - 1:
            for p in range(npairs):
                pws[p] = [_dot(pw, pw).astype(BF16) for pw in pws[p]]

    ys = []
    for p, sl in enumerate(lanes):
        r4 = jnp.concatenate(two(us[p]) + vss[p], axis=0).astype(BF16)
        ys.append(ybase[p] + _dot(w4s[p], r4))
        rh = jnp.concatenate(two(bh[:, sl]) + two(kh[:, sl]), axis=0)
        sblk[sl, :] = sps[p] * wl[:, sl] + _dot_tn(r4, rh)
    y = jnp.concatenate(ys, axis=1)
    y_ref[...] = _rwkv_head_norm(y, r, k, v, rk_ref, lnw_ref, lnb_ref, seg_ref, segt_ref).astype(y_ref.dtype)

    @pl.when(c == pl.num_programs(1) - 1)
    def _():
        sb = sblk[...]
        s_ref[...] = sb[:, 0:RWKV_HEAD_DIM] + sb[:, RWKV_HEAD_DIM:2 * RWKV_HEAD_DIM]


def _ssd_step_kernel(xbc_ref, t0_ref, t1_ref, t2_ref, z_ref, sm_ref, st_ref, cw_ref, cb_ref, bias_ref,
                     alog_ref, dexp_ref, nw_ref, segt_ref, y_ref, sto_ref):
    nseq = xbc_ref.shape[0]
    act = _silu(_step_conv(xbc_ref, (t0_ref, t1_ref, t2_ref), cw_ref, cb_ref))
    xs = act[:, 0:SSD_D_INNER]
    gn = SSD_GROUPS * SSD_D_STATE
    bm = act[:, SSD_D_INNER:SSD_D_INNER + gn]
    cm = act[:, SSD_D_INNER + gn:SSD_D_INNER + 2 * gn]
    lane = lax.broadcasted_iota(jnp.int32, (nseq, LANES), 1)
    head_lane = (lane >= SMALL_DT) & (lane < SMALL_DT + SSD_HEADS)
    dt = jnp.where(head_lane, _softplus(sm_ref[...] + bias_ref[...]), 0.0)
    segt = segt_ref[...]
    dec_t = _dot_exact_rhs(jnp.exp(dt * (-jnp.exp(alog_ref[...]))), segt).T
    xdt_t = (xs * _dot_exact_rhs(dt, segt)).T
    gw = SSD_GROUP_W
    rowi = lax.broadcasted_iota(jnp.int32, (nseq, gw), 0)
    ys = [jnp.zeros((nseq, gw), F32) for _ in range(SSD_GROUPS)]
    for s in range(nseq):
        for g in range(SSD_GROUPS):
            rs = slice(g * gw, (g + 1) * gw)
            ns = slice(g * SSD_D_STATE, (g + 1) * SSD_D_STATE)
            new = st_ref[s, rs, :] * dec_t[rs, s:s + 1] + xdt_t[rs, s:s + 1] * bm[s:s + 1, ns]
            sto_ref[s, rs, :] = new
            ys[g] = jnp.where(rowi == s, _dot_nt(cm[:, ns], new), ys[g])
    y = jnp.concatenate(ys, axis=1) + dexp_ref[...] * xs
    y_ref[...] = _group_rms(y * _silu(z_ref[...]), gw) * nw_ref[...]


def _mlstm_step_kernel(xm_ref, t0_ref, t1_ref, t2_ref, o_ref, sm_ref, n_ref, m_ref, c_ref, cw_ref, cb_ref,
                       bias_ref, wq_ref, wk_ref, wv_ref, nw_ref, skip_ref, h_ref, no_ref, mo_ref, co_ref):
    nseq = xm_ref.shape[0]
    raw = xm_ref[...]
    act = _silu(_step_conv(xm_ref, (t0_ref, t1_ref, t2_ref), cw_ref, cb_ref))
    q, k, v = _blockdiag_qkv(act, raw, wq_ref, wk_ref, wv_ref)
    logi = sm_ref[...] + bias_ref[...]
    logf = _log_sigmoid(logi)
    m_old = m_ref[...]
    m_all = m_old
    n_old = n_ref[...]
    lane = lax.broadcasted_iota(jnp.int32, (nseq, LANES), 1)
    hd = MLSTM_HEAD_DIM
    rowi = lax.broadcasted_iota(jnp.int32, (nseq, hd), 0)
    hs, ns = [], []
    for h in range(MLSTM_HEADS):
        cols = slice(h * hd, (h + 1) * hd)
        qh, kh, vh = q[:, cols], k[:, cols], v[:, cols]
        lf = logf[:, SMALL_F + h:SMALL_F + h + 1] + m_old[:, h:h + 1]
        li = logi[:, SMALL_I + h:SMALL_I + h + 1]
        m_new = jnp.maximum(lf, li)
        dc = jnp.exp(lf - m_new)
        kws = kh * jnp.exp(li - m_new)
        n_new = dc * n_old[:, cols] + kws
        den = jnp.sum(qh * n_new, axis=1, keepdims=True)
        kws_t = kws.T
        q_t = qh.T
        num = jnp.zeros((nseq, hd), F32)
        for s in range(nseq):
            c_new = dc[s:s + 1, :] * c_ref[s, cols, :] + kws_t[:, s:s + 1] * vh[s:s + 1, :]
            co_ref[s, cols, :] = c_new
            num = jnp.where(rowi == s, jnp.sum(q_t[:, s:s + 1] * c_new, axis=0, keepdims=True), num)
        hs.append(num / jnp.maximum(jnp.abs(den), jnp.exp(-m_new)))
        ns.append(n_new)
        m_all = jnp.where(lane == h, m_new, m_all)
    no_ref[...] = jnp.concatenate(ns, axis=1)
    mo_ref[...] = m_all
    hm = _group_rms(jnp.concatenate(hs, axis=1), hd) * nw_ref[...]
    h_ref[...] = (hm + skip_ref[...] * act) * _sigmoid(o_ref[...])


def _rwkv_step_kernel(r_ref, lw_ref, k_ref, v_ref, kk_ref, a_ref, s_ref, rk_ref, lnw_ref, lnb_ref,
                      seg_ref, segt_ref, y_ref, so_ref):
    nseq = r_ref.shape[0]
    r, k, v, kk = r_ref[...], k_ref[...], v_ref[...], kk_ref[...]
    w = jnp.exp(lw_ref[...])
    ka = kk * a_ref[...]
    v_t = v.T
    lane = lax.broadcasted_iota(jnp.int32, (RWKV_D, LANES), 1)
    y_t = jnp.zeros((RWKV_D, LANES), F32)
    n = RWKV_HEAD_DIM

    def head_rows(x, s):
        return jnp.concatenate([jnp.broadcast_to(x[s:s + 1, h * n:(h + 1) * n], (n, n))
                                for h in range(RWKV_HEADS)], axis=0)

    for s in range(nseq):
        st = s_ref[s]
        sa = jnp.sum(st * head_rows(kk, s), axis=1, keepdims=True)
        new = st * head_rows(w, s) - sa * head_rows(ka, s) + v_t[:, s:s + 1] * head_rows(k, s)
        so_ref[s] = new
        y_t = jnp.where(lane == s, jnp.sum(new * head_rows(r, s), axis=1, keepdims=True), y_t)
    y = y_t.T[0:nseq, :]
    y_ref[...] = _rwkv_head_norm(y, r, k, v, rk_ref, lnw_ref, lnb_ref, seg_ref, segt_ref)


def _params(sem):
    return pltpu.CompilerParams(dimension_semantics=sem, vmem_limit_bytes=VMEM_LIMIT)


def _call_name(kernel, n):
    fn = getattr(kernel, "func", kernel)
    return f"{fn.__name__.strip('_')}_{n}"


def _const_spec(shape, grid_rank):
    nd = len(shape)
    if grid_rank == 1:
        return pl.BlockSpec(shape, lambda i: (0,) * nd, pipeline_mode=pl.Buffered(1))
    return pl.BlockSpec(shape, lambda b, c: (0,) * nd, pipeline_mode=pl.Buffered(1))


def _row_tile(m, tile):
    return tile if m % tile == 0 else m


def _rowwise_call(kernel, row_ins, const_ins, outs, m, tile=ROW_TILE, extra_ins=()):
    tm = _row_tile(m, tile)
    rows = lambda n: pl.BlockSpec((tm, n), lambda i: (i, 0))
    return pl.pallas_call(
        kernel,
        name=_call_name(kernel, m),
        grid=(m // tm,),
        in_specs=([rows(a.shape[1]) for a in row_ins] + [spec for _, spec in extra_ins]
                  + [_const_spec(a.shape, 1) for a in const_ins]),
        out_specs=[rows(n) for n, _ in outs],
        out_shape=[jax.ShapeDtypeStruct((m, n), dt) for n, dt in outs],
        compiler_params=_params(("parallel",)),
    )(*row_ins, *[a for a, _ in extra_ins], *const_ins)


def _scan_call(kernel, seq_ins, state_ins, const_ins, seq_out_widths, state_out_shapes, scratch):
    bsz, t = seq_ins[0].shape[:2]
    seq = lambda n: pl.BlockSpec((None, SCAN_CHUNK, n), lambda b, c: (b, c, 0))
    per_seq = lambda shape: pl.BlockSpec((None,) + tuple(shape[1:]), lambda b, c: (b,) + (0,) * (len(shape) - 1))
    return pl.pallas_call(
        kernel,
        name=_call_name(kernel, bsz),
        grid=(bsz, t // SCAN_CHUNK),
        in_specs=([seq(a.shape[2]) for a in seq_ins] + [per_seq(a.shape) for a in state_ins]
                  + [_const_spec(a.shape, 2) for a in const_ins]),
        out_specs=[seq(n) for n in seq_out_widths] + [per_seq(s) for s in state_out_shapes],
        out_shape=([jax.ShapeDtypeStruct((bsz, t, n), seq_ins[0].dtype) for n in seq_out_widths]
                   + [jax.ShapeDtypeStruct(s, F32) for s in state_out_shapes]),
        scratch_shapes=scratch,
        compiler_params=_params(("parallel", "arbitrary")),
    )(*seq_ins, *state_ins, *const_ins)


def _step_call(kernel, row_ins, state_ins, const_ins, row_out_widths, state_out_shapes):
    bsz = row_ins[0].shape[0]
    rows = lambda n: pl.BlockSpec((STEP_SEQS, n), lambda i: (i, 0))
    state = lambda shape: pl.BlockSpec((STEP_SEQS,) + tuple(shape[1:]), lambda i: (i, 0, 0))
    return pl.pallas_call(
        kernel,
        name=_call_name(kernel, bsz),
        grid=(bsz // STEP_SEQS,),
        in_specs=([rows(a.shape[1]) for a in row_ins] + [state(a.shape) for a in state_ins]
                  + [_const_spec(a.shape, 1) for a in const_ins]),
        out_specs=[rows(n) for n in row_out_widths] + [state(s) for s in state_out_shapes],
        out_shape=([jax.ShapeDtypeStruct((bsz, n), F32) for n in row_out_widths]
                   + [jax.ShapeDtypeStruct(s, F32) for s in state_out_shapes]),
        compiler_params=_params(("parallel",)),
    )(*row_ins, *state_ins, *const_ins)


def _row(v):
    return v.reshape(1, -1).astype(F32)


def _pad_cols(w, width):
    return jnp.pad(w, ((0, 0), (0, width - w.shape[1])))


def _pad_rows_to(w, rows):
    return jnp.pad(w, ((0, rows - w.shape[0]), (0, 0)))


def _blockdiag_slabs(w, scale=1.0):
    hd = MLSTM_HEAD_DIM
    rows = (w * scale).reshape(-1, hd, QKV_BLOCK)
    col = jnp.arange(hd)
    pick = (col[None, :] % QKV_BLOCK == jnp.arange(QKV_BLOCK)[:, None]).astype(w.dtype)
    tiled = jnp.einsum('srd,dc->src', rows, pick, precision=lax.Precision.HIGHEST)
    same_block = col[:, None] // QKV_BLOCK == col[None, :] // QKV_BLOCK
    return jnp.where(same_block, tiled, 0.0).astype(BF16)


def _head_indicator():
    ch = jnp.arange(RWKV_D) // RWKV_HEAD_DIM
    seg = (ch[:, None] == jnp.arange(LANES)[None, :]).astype(BF16)
    return seg, seg.T


def _mixer0_consts(p):
    ssd = [p['conv_w'][:, :XBC_DIM], p['conv_b'][:, :XBC_DIM], p['small_bias'], p['ssd_a_log'], p['ssd_d'],
           p['ssd_norm'], p['segt']]
    ml = [p['conv_w'][:, XBC_DIM:], p['conv_b'][:, XBC_DIM:], p['small_bias'], p['ml_wq'], p['ml_wk'],
          p['ml_wv'], p['ml_norm'], p['ml_skip']]
    return ssd, ml


def _mixer0_chunked(xbc, xm, z, o_pre, small, conv, ssm, mc, mn, mm, p, bsz, t):
    seq = lambda a: a.reshape(bsz, t, -1)
    tail = jnp.pad(conv, ((0, 0), (SUBLANES - (CONV_W - 1), 0), (0, 0)))
    ssd_c, ml_c = _mixer0_consts(p)
    y_ssd, new_ssm = _scan_call(
        _ssd_kernel, [seq(xbc), seq(z), seq(small)],
        [tail[:, :, :XBC_DIM], ssm.reshape(bsz, SSD_D_INNER, SSD_D_STATE)], ssd_c,
        [SSD_D_INNER], [(bsz, SSD_D_INNER, SSD_D_STATE)],
        [pltpu.VMEM((SCAN_CHUNK + SUBLANES, XBC_DIM), F32), pltpu.VMEM((SSD_D_INNER, SSD_D_STATE), F32)])
    hm, new_c, new_n, new_m = _scan_call(
        _mlstm_kernel, [seq(xm), seq(o_pre), seq(small)],
        [tail[:, :, XBC_DIM:], mc.reshape(bsz, MLSTM_D_INNER, MLSTM_HEAD_DIM), mn.reshape(bsz, 1, MLSTM_D_INNER),
         _pad_cols(mm, LANES).reshape(bsz, 1, LANES)], ml_c,
        [MLSTM_D_INNER], [(bsz, MLSTM_D_INNER, MLSTM_HEAD_DIM), (bsz, 1, MLSTM_D_INNER), (bsz, 1, LANES)],
        [pltpu.VMEM((SCAN_CHUNK + SUBLANES, MLSTM_D_INNER), F32), pltpu.VMEM((MLSTM_D_INNER, MLSTM_HEAD_DIM), F32),
         pltpu.VMEM((1, MLSTM_D_INNER), F32), pltpu.VMEM((1, LANES), F32)])
    flat = lambda a: a.reshape(bsz * t, -1)
    return flat(y_ssd), flat(hm), new_ssm, new_c, new_n, new_m.reshape(bsz, LANES)


def _mixer0_step(xbc, xm, z, o_pre, small, conv, ssm, mc, mn, mm, p, bsz):
    ssd_c, ml_c = _mixer0_consts(p)
    tails = [conv[:, j, :] for j in range(CONV_W - 1)]
    y_ssd, new_ssm = _step_call(
        _ssd_step_kernel, [xbc] + [tl[:, :XBC_DIM] for tl in tails] + [z, small],
        [ssm.reshape(bsz, SSD_D_INNER, SSD_D_STATE)], ssd_c, [SSD_D_INNER], [(bsz, SSD_D_INNER, SSD_D_STATE)])
    hm, new_n, new_m, new_c = _step_call(
        _mlstm_step_kernel, [xm] + [tl[:, XBC_DIM:] for tl in tails]
        + [o_pre, small, mn.reshape(bsz, MLSTM_D_INNER), _pad_cols(mm, LANES)],
        [mc.reshape(bsz, MLSTM_D_INNER, MLSTM_HEAD_DIM)], ml_c,
        [MLSTM_D_INNER, MLSTM_D_INNER, LANES], [(bsz, MLSTM_D_INNER, MLSTM_HEAD_DIM)])
    return y_ssd, hm, new_ssm, new_c, new_n, new_m


def _trunk(x, conv, ssm, mc, mn, mm, shift, wkv, p):
    bsz, t, d = x.shape
    m = bsz * t
    step = t == 1
    assert step or t % SCAN_CHUNK == 0, "a group is either single-token or a multiple of the scan chunk"
    assert not step or bsz % STEP_SEQS == 0
    act = F32 if step else BF16
    x2 = x.reshape(m, d)

    z, xbc, xm, o_pre, small = _rowwise_call(
        _in0_kernel, [x2], [p['norm_mix0'], p['w_in0']],
        [(SSD_D_INNER, act), (XBC_DIM, act), (MLSTM_D_INNER, act), (MLSTM_D_INNER, act), (LANES, F32)], m,
        tile=WIDE_ROW_TILE)
    last_rows = lambda a: a.reshape(bsz, t, -1)[:, -(CONV_W - 1):].astype(F32)
    conv_in_tail = jnp.concatenate([last_rows(xbc), last_rows(xm)], axis=-1)
    new_conv = jnp.concatenate([conv, conv_in_tail], axis=1)[:, -(CONV_W - 1):]
    if step:
        y_ssd, hm, new_ssm, new_c, new_n, new_m = _mixer0_step(xbc, xm, z, o_pre, small, conv, ssm, mc, mn, mm, p, bsz)
    else:
        y_ssd, hm, new_ssm, new_c, new_n, new_m = _mixer0_chunked(xbc, xm, z, o_pre, small, conv, ssm, mc, mn, mm,
                                                                  p, bsz, t)
    xn1, x2b = _rowwise_call(
        _mix0_ffn_kernel, [y_ssd, hm, x2],
        [p['w_out0'], p['norm_ffn0'], p['ffn_gu0'], p['ffn_d0'], p['norm_mix1']], [(d, act), (d, F32)], m,
        tile=WIDE_ROW_TILE)

    rw_in_c = [p['rw_mu'], p['rw_wr'], p['rw_wk'], p['rw_wv'], p['rw_w1'], p['rw_w2'], p['rw_a1'], p['rw_a2'],
               p['rw_g1'], p['rw_g2'], p['rw_w0'], p['rw_a0'], p['rw_k_k'], p['rw_k_a'], p['seg'], p['segt']]
    rw_outs = [(d, act), (d, F32)] + [(d, act)] * 5
    if step:
        rw = _rowwise_call(_rw_in_kernel, [xn1, shift], rw_in_c, rw_outs, m)
    else:
        tm = _row_tile(t, ROW_TILE)
        tiles_per_seq = t // tm
        prev_spec = pl.BlockSpec((PREV_ROWS, d), lambda i: (jnp.maximum(i * (tm // PREV_ROWS) - 1, 0), 0))
        shift_spec = pl.BlockSpec((None, 1, d), lambda i: (i // tiles_per_seq, 0, 0))
        rw = _rowwise_call(functools.partial(_rw_in_shift_kernel, tiles_per_seq=tiles_per_seq), [xn1], rw_in_c,
                           rw_outs, m, tile=tm, extra_ins=[(xn1, prev_spec), (shift.reshape(bsz, 1, d), shift_spec)])
    g = rw[6]
    rw_c = [p['rw_r_k'], p['rw_ln_w'], p['rw_ln_b'], p['seg'], p['segt']]
    wkv3 = wkv.reshape(bsz, RWKV_D, RWKV_HEAD_DIM)
    if step:
        y_rw, new_wkv = _step_call(_rwkv_step_kernel, list(rw[:6]), [wkv3], rw_c, [d], [wkv3.shape])
    else:
        y_rw, new_wkv = _scan_call(
            _rwkv_kernel, [a.reshape(bsz, t, d) for a in rw[:6]], [wkv3], rw_c,
            [d], [wkv3.shape], [pltpu.VMEM((RWKV_D, 2 * RWKV_HEAD_DIM), F32)])
        y_rw = y_rw.reshape(m, d)
    new_wkv = new_wkv.reshape(wkv.shape)
    (y,) = _rowwise_call(
        _mix1_ffn_kernel, [y_rw, g, x2b],
        [p['rw_wo'], p['norm_ffn1'], p['ffn_gu1'], p['ffn_d1'], p['norm_final']], [(d, F32)], m, tile=WIDE_ROW_TILE)

    states = (new_conv, new_ssm.reshape(ssm.shape), new_c.reshape(mc.shape), new_n.reshape(mn.shape),
              new_m[:, :MLSTM_HEADS], xn1.reshape(bsz, t, d)[:, -1].astype(F32), new_wkv)
    return y.reshape(bsz, t, d), states


def kernel(x_prompt, x_sample, state_conv, state_ssm, state_mlstm_c, state_mlstm_n, state_mlstm_m, state_shift, state_wkv, norm_mix, norm_ffn, norm_final, w_in0, conv_w, conv_b, ssd_dt_bias, ssd_a_log, ssd_d, ssd_norm, ml_wq, ml_wk, ml_wv, ml_i_bias, ml_f_bias, ml_norm, ml_skip, w_out0, rw_mu, rw_wr, rw_wk, rw_wv, rw_wo, rw_w0, rw_w1, rw_w2, rw_a0, rw_a1, rw_a2, rw_g1, rw_g2, rw_k_k, rw_k_a, rw_r_k, rw_ln_w, rw_ln_b, ffn_w_gate_up, ffn_w_down):
    assert norm_mix.shape[0] == 2 and w_in0.shape[0] == 1 and rw_wr.shape[0] == 1, "two layers: SSD|mLSTM then RWKV-7"
    s1 = SSD_D_INNER
    s2 = s1 + XBC_DIM + MLSTM_D_INNER
    s3 = s2 + SSD_HEADS
    s4 = s3 + MLSTM_D_INNER
    s5 = s4 + MLSTM_HEADS
    w0 = w_in0[0]
    small_w = _pad_cols(jnp.concatenate([w0[:, s2:s3], w0[:, s4:s5], w0[:, s5:]], axis=1), LANES)
    seg, segt = _head_indicator()
    lora = lambda w1, w2, width: (_pad_cols(w1, width).astype(BF16), _pad_rows_to(w2, width).astype(BF16))
    rw_w1p, rw_w2p = lora(rw_w1[0], rw_w2[0], LANES)
    rw_a1p, rw_a2p = lora(rw_a1[0], rw_a2[0], LANES)
    rw_g1p, rw_g2p = lora(rw_g1[0], rw_g2[0], 2 * LANES)
    p = dict(
        norm_mix0=_row(norm_mix[0]), norm_mix1=_row(norm_mix[1]), norm_ffn0=_row(norm_ffn[0]),
        norm_ffn1=_row(norm_ffn[1]), norm_final=_row(norm_final),
        w_in0=jnp.concatenate([w0[:, :s2], w0[:, s3:s4], small_w], axis=1).astype(BF16),
        conv_w=conv_w[0], conv_b=_row(conv_b[0]),
        small_bias=_pad_cols(jnp.concatenate([_row(ssd_dt_bias[0]), _row(ml_i_bias[0]), _row(ml_f_bias[0])], axis=1),
                             LANES),
        ssd_a_log=_pad_cols(_row(ssd_a_log[0]), LANES), ssd_d=_row(jnp.repeat(ssd_d[0], SSD_HEAD_DIM)),
        ssd_norm=_row(ssd_norm[0]),
        ml_wq=_blockdiag_slabs(ml_wq[0]), ml_wk=_blockdiag_slabs(ml_wk[0], MLSTM_HEAD_DIM ** -0.5),
        ml_wv=_blockdiag_slabs(ml_wv[0]), ml_norm=_row(ml_norm[0]), ml_skip=_row(ml_skip[0]),
        w_out0=w_out0[0].astype(BF16),
        rw_mu=_pad_rows_to(rw_mu[0], SUBLANES),
        rw_wr=rw_wr[0].astype(BF16), rw_wk=rw_wk[0].astype(BF16), rw_wv=rw_wv[0].astype(BF16),
        rw_wo=rw_wo[0].astype(BF16), rw_w0=_row(rw_w0[0]), rw_a0=_row(rw_a0[0]),
        rw_w1=rw_w1p, rw_w2=rw_w2p, rw_a1=rw_a1p, rw_a2=rw_a2p, rw_g1=rw_g1p, rw_g2=rw_g2p,
        rw_k_k=_row(rw_k_k[0]), rw_k_a=_row(rw_k_a[0]), rw_r_k=_row(rw_r_k[0]),
        rw_ln_w=_row(rw_ln_w[0]), rw_ln_b=_row(rw_ln_b[0]),
        ffn_gu0=ffn_w_gate_up[0].astype(BF16), ffn_gu1=ffn_w_gate_up[1].astype(BF16),
        ffn_d0=ffn_w_down[0].astype(BF16), ffn_d1=ffn_w_down[1].astype(BF16),
        seg=seg, segt=segt,
    )
    bp = x_prompt.shape[0]
    zeros = lambda s: jnp.zeros((bp,) + s.shape[2:], F32)
    y_p, st_p = _trunk(x_prompt, zeros(state_conv), zeros(state_ssm), zeros(state_mlstm_c),
                       zeros(state_mlstm_n), zeros(state_mlstm_m), zeros(state_shift), zeros(state_wkv), p)
    y_s, st_s = _trunk(x_sample, state_conv[0], state_ssm[0], state_mlstm_c[0], state_mlstm_n[0],
                       state_mlstm_m[0], state_shift[0], state_wkv[0], p)
    out = [y_p, y_s]
    for a, b in zip(st_p, st_s):
        out += [a[None], b[None]]
    return tuple(out)
```

```python
import functools

import jax
import jax.numpy as jnp
from jax import lax
from jax.experimental import pallas as pl
from jax.experimental.pallas import tpu as pltpu

F32 = jnp.float32
BF16 = jnp.bfloat16

D_MODEL = 1024
NORM_EPS = 1e-5
CONV_W = 4
SSD_HEADS = 16
SSD_HEAD_DIM = 64
SSD_GROUPS = 2
SSD_D_STATE = 128
SSD_D_INNER = SSD_HEADS * SSD_HEAD_DIM
SSD_GROUP_W = SSD_D_INNER // SSD_GROUPS
XBC_DIM = SSD_D_INNER + 2 * SSD_GROUPS * SSD_D_STATE
MLSTM_HEADS = 4
MLSTM_HEAD_DIM = 256
MLSTM_D_INNER = MLSTM_HEADS * MLSTM_HEAD_DIM
QKV_BLOCK = 4
RWKV_HEADS = 16
RWKV_HEAD_DIM = 64
RWKV_D = RWKV_HEADS * RWKV_HEAD_DIM
RWKV_LN_EPS = 64e-5
D_FF = 2816
FF_CHUNK = 256
LANES = 128
SUBLANES = 8
SMALL_DT, SMALL_I, SMALL_F = 0, 16, 20
NEG = -1e30
EXP_NEG_HALF = 0.6065306597126334
SCAN_CHUNK = 128
STEP_SEQS = SUBLANES
ROW_TILE = 512
WIDE_ROW_TILE = 512
PREV_ROWS = 16
VMEM_LIMIT = 56 * 1024 * 1024


def _dot(a, b):
    return jnp.dot(a.astype(BF16), b.astype(BF16), preferred_element_type=F32)


def _dot_nt(a, b):
    return lax.dot_general(a.astype(BF16), b.astype(BF16), (((1,), (1,)), ((), ())),
                           preferred_element_type=F32)


def _dot_tn(a, b):
    return lax.dot_general(a.astype(BF16), b.astype(BF16), (((0,), (0,)), ((), ())),
                           preferred_element_type=F32)


def _split(x, terms):
    parts = []
    rem = x
    for _ in range(terms):
        p = rem.astype(BF16)
        parts.append(p)
        rem = rem - p.astype(F32)
    return parts


def _dot_exact_rhs(x, m, terms=2):
    out = None
    for p in _split(x, terms):
        t = jnp.dot(p, m, preferred_element_type=F32)
        out = t if out is None else out + t
    return out


def _tri_cumsum(tri, x, terms=3):
    out = None
    for p in _split(x, terms):
        t = jnp.dot(tri, p, preferred_element_type=F32)
        out = t if out is None else out + t
    return out


def _rms_rows(x, g):
    return x * lax.rsqrt(jnp.mean(x * x, axis=-1, keepdims=True) + NORM_EPS) * g


def _group_rms(x, width):
    outs = []
    for lo in range(0, x.shape[1], width):
        grp = x[:, lo:lo + width]
        outs.append(grp * lax.rsqrt(jnp.mean(grp * grp, axis=-1, keepdims=True) + NORM_EPS))
    return jnp.concatenate(outs, axis=1)


def _sigmoid(x):
    return 0.5 * jnp.tanh(0.5 * x) + 0.5


def _silu(x):
    return x * _sigmoid(x)


def _softplus(x):
    return jnp.maximum(x, 0.0) + jnp.log(1.0 + jnp.exp(-jnp.abs(x)))


def _log_sigmoid(x):
    return jnp.minimum(x, 0.0) - jnp.log(1.0 + jnp.exp(-jnp.abs(x)))


def _tri_masks(n):
    row = lax.broadcasted_iota(jnp.int32, (n, n), 0)
    col = lax.broadcasted_iota(jnp.int32, (n, n), 1)
    return row >= col, row > col


def _chunk_conv(ext, u, conv0_ref, cw_ref, cb_ref, first):
    rows = u.shape[0]

    @pl.when(first)
    def _():
        ext[0:SUBLANES, :] = conv0_ref[...]

    @pl.when(jnp.logical_not(first))
    def _():
        ext[0:SUBLANES, :] = ext[rows:rows + SUBLANES, :]

    ext[SUBLANES:SUBLANES + rows, :] = u
    e = ext[...]
    conv = cb_ref[...] + u * cw_ref[CONV_W - 1:CONV_W, :]
    for j in range(CONV_W - 1):
        back = pltpu.roll(e, shift=CONV_W - 1 - j, axis=0)[SUBLANES:SUBLANES + rows, :]
        conv = conv + back * cw_ref[j:j + 1, :]
    return conv


def _step_conv(u_ref, tail_refs, cw_ref, cb_ref):
    conv = cb_ref[...] + u_ref[...] * cw_ref[CONV_W - 1:CONV_W, :]
    for j, t in enumerate(tail_refs):
        conv = conv + t[...] * cw_ref[j:j + 1, :]
    return conv


def _blockdiag_qkv(act, raw, wq_ref, wk_ref, wv_ref):
    hd = MLSTM_HEAD_DIM
    cat = lambda x, w: jnp.concatenate(
        [_dot(x[:, s * hd:(s + 1) * hd], w[s]) for s in range(MLSTM_HEADS)], axis=1)
    return cat(act, wq_ref), cat(act, wk_ref), cat(raw, wv_ref)


def _rwkv_head_norm(y, r, k, v, rk_ref, lnw_ref, lnb_ref, seg_ref, segt_ref):
    seg = seg_ref[...]
    segt = segt_ref[...]
    head_sum = lambda x: _dot_exact_rhs(_dot_exact_rhs(x, seg), segt, terms=1)
    inv_n = 1.0 / RWKV_HEAD_DIM
    dlt = y - head_sum(y) * inv_n
    yn = dlt * lax.rsqrt(head_sum(dlt * dlt) * inv_n + RWKV_LN_EPS)
    return yn * lnw_ref[...] + lnb_ref[...] + head_sum(r * k * rk_ref[...]) * v


def _in0_kernel(x_ref, g_ref, w_ref, z_ref, xbc_ref, xm_ref, o_ref, sm_ref):
    xn = _rms_rows(x_ref[...], g_ref[...]).astype(BF16)
    off = 0
    for ref in (z_ref, xbc_ref, xm_ref, o_ref, sm_ref):
        n = ref.shape[-1]
        ref[...] = jnp.dot(xn, w_ref[:, off:off + n], preferred_element_type=F32).astype(ref.dtype)
        off += n


def _swiglu_residual(x, g_ref, wgu_ref, wd_ref):
    xn = _rms_rows(x, g_ref[...]).astype(BF16)
    acc = None
    for c in range(D_FF // FF_CHUNK):
        lo = c * FF_CHUNK
        gate = jnp.dot(xn, wgu_ref[:, lo:lo + FF_CHUNK], preferred_element_type=F32)
        up = jnp.dot(xn, wgu_ref[:, D_FF + lo:D_FF + lo + FF_CHUNK], preferred_element_type=F32)
        part = _dot(_silu(gate) * up, wd_ref[lo:lo + FF_CHUNK, :])
        acc = part if acc is None else acc + part
    return x + acc


def _mix0_ffn_kernel(y_ref, h_ref, x_ref, wo_ref, g_ref, wgu_ref, wd_ref, pg_ref, xn_ref, xo_ref):
    mixed = jnp.concatenate([y_ref[...].astype(BF16), h_ref[...].astype(BF16)], axis=1)
    x1 = x_ref[...] + jnp.dot(mixed, wo_ref[...], preferred_element_type=F32)
    x2 = _swiglu_residual(x1, g_ref, wgu_ref, wd_ref)
    xo_ref[...] = x2
    xn_ref[...] = _rms_rows(x2, pg_ref[...]).astype(xn_ref.dtype)


def _mix1_ffn_kernel(y_ref, gate_ref, x_ref, wo_ref, g_ref, wgu_ref, wd_ref, pg_ref, o_ref):
    x3 = x_ref[...] + _dot(y_ref[...].astype(F32) * gate_ref[...].astype(F32), wo_ref[...])
    o_ref[...] = _rms_rows(_swiglu_residual(x3, g_ref, wgu_ref, wd_ref), pg_ref[...])


def _rw_in_body(xn, xp, mu_ref, wr_ref, wk_ref, wv_ref, w1_ref, w2_ref, a1_ref, a2_ref,
                g1_ref, g2_ref, w0_ref, a0_ref, kk_w_ref, ka_w_ref, seg_ref, segt_ref,
                r_ref, lw_ref, k_ref, v_ref, kk_ref, a_ref, g_ref):
    xx = xp - xn
    xr, xw, xk, xv, xa, xg = (xn + xx * mu_ref[c:c + 1, :] for c in range(6))
    r_ref[...] = _dot(xr, wr_ref[...]).astype(r_ref.dtype)
    k = _dot(xk, wk_ref[...])
    v_ref[...] = _dot(xv, wv_ref[...]).astype(v_ref.dtype)
    z = w0_ref[...] + _dot(jnp.tanh(_dot(xw, w1_ref[...])), w2_ref[...])
    lw_ref[...] = -EXP_NEG_HALF * _sigmoid(z)
    a = _sigmoid(a0_ref[...] + _dot(_dot(xa, a1_ref[...]), a2_ref[...]))
    a_ref[...] = a.astype(a_ref.dtype)
    g_ref[...] = _dot(_sigmoid(_dot(xg, g1_ref[...])), g2_ref[...]).astype(g_ref.dtype)
    kk = k * kk_w_ref[...]
    ss = _dot_exact_rhs(_dot_exact_rhs(kk * kk, seg_ref[...]), segt_ref[...])
    kk_ref[...] = (kk * lax.rsqrt(jnp.maximum(ss, 1e-24))).astype(kk_ref.dtype)
    k_ref[...] = (k * (1.0 + (a - 1.0) * ka_w_ref[...])).astype(k_ref.dtype)


def _rw_in_kernel(xn_ref, xp_ref, *refs):
    _rw_in_body(xn_ref[...].astype(F32), xp_ref[...].astype(F32), *refs)


def _rw_in_shift_kernel(xn_ref, prev_ref, shift_ref, *refs, tiles_per_seq):
    xn = xn_ref[...].astype(F32)
    starts_seq = pl.program_id(0) % tiles_per_seq == 0
    last = prev_ref.shape[0] - 1
    prev = jnp.where(starts_seq, shift_ref[...], prev_ref[last:last + 1, :].astype(F32))
    row = lax.broadcasted_iota(jnp.int32, xn.shape, 0)
    xp = jnp.where(row == 0, prev, pltpu.roll(xn, shift=1, axis=0))
    _rw_in_body(xn, xp, *refs)


def _ssd_kernel(xbc_ref, z_ref, sm_ref, conv0_ref, st0_ref, cw_ref, cb_ref, bias_ref, alog_ref,
                dexp_ref, nw_ref, segt_ref, y_ref, st_ref, ext, state):
    c = pl.program_id(1)
    first = c == 0
    rows = xbc_ref.shape[0]

    @pl.when(first)
    def _():
        state[...] = st0_ref[...]

    act = _silu(_chunk_conv(ext, xbc_ref[...].astype(F32), conv0_ref, cw_ref, cb_ref, first))
    xs = act[:, 0:SSD_D_INNER]
    gn = SSD_GROUPS * SSD_D_STATE
    bm = act[:, SSD_D_INNER:SSD_D_INNER + gn]
    cm = act[:, SSD_D_INNER + gn:SSD_D_INNER + 2 * gn]

    tril, _ = _tri_masks(rows)
    tri = jnp.where(tril, 1.0, 0.0).astype(BF16)
    lane = lax.broadcasted_iota(jnp.int32, (rows, LANES), 1)
    head_lane = (lane >= SMALL_DT) & (lane < SMALL_DT + SSD_HEADS)
    dt = jnp.where(head_lane, _softplus(sm_ref[...] + bias_ref[...]), 0.0)
    acum = _tri_cumsum(tri, dt * (-jnp.exp(alog_ref[...])))
    acum_t = acum.T
    dt_t = dt.T
    segt = segt_ref[...]
    eac_x = _dot_exact_rhs(jnp.exp(acum), segt)
    wend_x = _dot_exact_rhs(jnp.exp(acum[rows - 1:rows, :] - acum) * dt, segt)
    xw = xs * wend_x

    half = lane < SSD_HEAD_DIM
    heads_per_group = SSD_HEADS // SSD_GROUPS
    pairs = []
    for g in range(SSD_GROUPS):
        bg = bm[:, g * SSD_D_STATE:(g + 1) * SSD_D_STATE]
        cg = cm[:, g * SSD_D_STATE:(g + 1) * SSD_D_STATE]
        cb = _dot_nt(cg, bg)
        for q in range(heads_per_group // 2):
            wts = []
            for e in range(2):
                j = g * heads_per_group + 2 * q + e
                seg = acum[:, j:j + 1] - acum_t[j:j + 1, :]
                wts.append(jnp.exp(jnp.where(tril, seg, NEG)) * cb * dt_t[j:j + 1, :])
            p = (g * heads_per_group) // 2 + q
            xp = xs[:, p * LANES:(p + 1) * LANES]
            rhs = jnp.concatenate([jnp.where(half, xp, 0.0), jnp.where(half, 0.0, xp)], axis=0)
            pairs.append(_dot(jnp.concatenate(wts, axis=1), rhs))
    y = jnp.concatenate(pairs, axis=1)

    st = state[...]
    gw = SSD_GROUP_W
    y_state = jnp.concatenate(
        [_dot_nt(cm[:, g * SSD_D_STATE:(g + 1) * SSD_D_STATE], st[g * gw:(g + 1) * gw, :])
         for g in range(SSD_GROUPS)], axis=1)
    y = y + eac_x * y_state + dexp_ref[...] * xs
    y_ref[...] = (_group_rms(y * _silu(z_ref[...].astype(F32)), gw) * nw_ref[...]).astype(y_ref.dtype)

    dcol = jnp.exp(acum_t[:, rows - 1:rows])
    for g in range(SSD_GROUPS):
        upd = _dot_tn(xw[:, g * gw:(g + 1) * gw], bm[:, g * SSD_D_STATE:(g + 1) * SSD_D_STATE])
        for h in range(heads_per_group):
            j = g * heads_per_group + h
            lo = j * SSD_HEAD_DIM
            state[lo:lo + SSD_HEAD_DIM, :] = (st[lo:lo + SSD_HEAD_DIM, :] * dcol[j:j + 1, :]
                                              + upd[h * SSD_HEAD_DIM:(h + 1) * SSD_HEAD_DIM, :])

    @pl.when(c == pl.num_programs(1) - 1)
    def _():
        st_ref[...] = state[...]


def _mlstm_kernel(xm_ref, o_ref, sm_ref, conv0_ref, c0_ref, n0_ref, m0_ref, cw_ref, cb_ref, bias_ref,
                  wq_ref, wk_ref, wv_ref, nw_ref, skip_ref, h_ref, c_ref, n_ref, m_ref,
                  ext, cst, nst, mst):
    c = pl.program_id(1)
    first = c == 0
    rows = xm_ref.shape[0]

    @pl.when(first)
    def _():
        cst[...] = c0_ref[...]
        nst[...] = n0_ref[...]
        mst[...] = m0_ref[...]

    raw = xm_ref[...].astype(F32)
    act = _silu(_chunk_conv(ext, raw, conv0_ref, cw_ref, cb_ref, first))
    q, k, v = _blockdiag_qkv(act, raw, wq_ref, wk_ref, wv_ref)

    tril, _ = _tri_masks(rows)
    tri = jnp.where(tril, 1.0, 0.0).astype(BF16)
    logi = sm_ref[...] + bias_ref[...]
    bcum = _tri_cumsum(tri, _log_sigmoid(logi))
    bcum_t = bcum.T
    logi_t = logi.T
    m_all = mst[...]
    lane1 = lax.broadcasted_iota(jnp.int32, (1, LANES), 1)
    hd = MLSTM_HEAD_DIM

    hs = []
    for h in range(MLSTM_HEADS):
        qh, kh, vh = (t[:, h * hd:(h + 1) * hd] for t in (q, k, v))
        bc = bcum[:, SMALL_F + h:SMALL_F + h + 1]
        bct = bcum_t[SMALL_F + h:SMALL_F + h + 1, :]
        lit = logi_t[SMALL_I + h:SMALL_I + h + 1, :]
        li = logi[:, SMALL_I + h:SMALL_I + h + 1]
        m_h = m_all[:, h:h + 1]
        c_h = cst[h * hd:(h + 1) * hd, :]
        n_h = nst[:, h * hd:(h + 1) * hd]
        dlog = jnp.where(tril, bc - bct + lit, NEG)
        inter = bc + m_h
        m_t = jnp.maximum(inter, jnp.max(dlog, axis=1, keepdims=True))
        s = _dot_nt(qh, kh) * jnp.exp(dlog - m_t)
        w_inter = jnp.exp(inter - m_t)
        num = _dot(s, vh) + w_inter * _dot(qh, c_h)
        den = jnp.sum(s, axis=1, keepdims=True) + w_inter * jnp.sum(qh * n_h, axis=1, keepdims=True)
        hs.append(num / jnp.maximum(jnp.abs(den), jnp.exp(-m_t)))
        b_end = bc[rows - 1:rows, :]
        wlog = b_end - bc + li
        m_new = jnp.maximum(b_end + m_h, jnp.max(wlog, axis=0, keepdims=True))
        dc = jnp.exp(b_end + m_h - m_new)
        kws = kh * jnp.exp(wlog - m_new)
        cst[h * hd:(h + 1) * hd, :] = dc * c_h + _dot_tn(kws, vh)
        nst[:, h * hd:(h + 1) * hd] = dc * n_h + jnp.sum(kws, axis=0, keepdims=True)
        m_all = jnp.where(lane1 == h, m_new, m_all)
    mst[...] = m_all

    hm = _group_rms(jnp.concatenate(hs, axis=1), hd) * nw_ref[...]
    h_ref[...] = ((hm + skip_ref[...] * act) * _sigmoid(o_ref[...].astype(F32))).astype(h_ref.dtype)

    @pl.when(c == pl.num_programs(1) - 1)
    def _():
        c_ref[...] = cst[...]
        n_ref[...] = nst[...]
        m_ref[...] = mst[...]


def _rwkv_kernel(r_ref, lw_ref, k_ref, v_ref, kk_ref, a_ref, s0_ref, rk_ref, lnw_ref, lnb_ref,
                 seg_ref, segt_ref, y_ref, s_ref, sblk):
    c = pl.program_id(1)
    rows = r_ref.shape[0]

    @pl.when(c == 0)
    def _():
        s0 = s0_ref[...]
        shape = (RWKV_D, 2 * RWKV_HEAD_DIM)
        head_parity = (lax.broadcasted_iota(jnp.int32, shape, 0) // RWKV_HEAD_DIM) % 2
        lane_half = lax.broadcasted_iota(jnp.int32, shape, 1) // RWKV_HEAD_DIM
        sblk[...] = jnp.where(head_parity == lane_half, jnp.concatenate([s0, s0], axis=1), 0.0)

    r, lw, k, v, kk, a = (t[...].astype(F32) for t in (r_ref, lw_ref, k_ref, v_ref, kk_ref, a_ref))
    tril, strict = _tri_masks(rows)
    tri = jnp.where(tril, 1.0, 0.0).astype(BF16)
    cum = _tri_cumsum(tri, lw, terms=2)
    last = cum[rows - 1:rows, :]
    einv = jnp.exp(-cum)
    eend = jnp.exp(last - cum)
    ka = kk * a
    at = -(kk * jnp.exp(cum - lw))
    rt = r * jnp.exp(cum)
    bt = ka * einv
    kt = k * einv
    bh = ka * eend
    kh = k * eend
    wl = jnp.exp(last)
    half = lax.broadcasted_iota(jnp.int32, (rows, LANES), 1) < RWKV_HEAD_DIM
    steps = rows.bit_length() - 1

    def two(x):
        return [jnp.where(half, x, 0.0), jnp.where(half, 0.0, x)]

    npairs = RWKV_HEADS // 2
    tril4 = jnp.concatenate([tril] * 4, axis=1)
    strict4 = jnp.concatenate([strict] * 4, axis=1)
    lanes = [slice(p * LANES, (p + 1) * LANES) for p in range(npairs)]

    sps, ybase, us, pws, w4s, vss = [], [], [], [], [], []
    for sl in lanes:
        a2 = jnp.concatenate([at[:, sl], rt[:, sl]], axis=0).astype(BF16)
        rb = jnp.concatenate(two(bt[:, sl]) + two(kt[:, sl]), axis=0).astype(BF16)
        sp = sblk[sl, :]
        pbk = _dot_nt(a2, rb)
        xs = _dot_nt(a2, sp)
        vs = two(v[:, sl])
        top = jnp.where(strict4, pbk[0:rows, :], 0.0)
        us.append(xs[0:rows, :] + _dot(top[:, 2 * rows:4 * rows], jnp.concatenate(vs, axis=0)))
        pws.append([top[:, e * rows:(e + 1) * rows].astype(BF16) for e in range(2)])
        sps.append(sp)
        ybase.append(xs[rows:2 * rows, :])
        vss.append(vs)
        w4s.append(jnp.where(tril4, pbk[rows:2 * rows, :], 0.0).astype(BF16))

    for j in range(steps):
        for p in range(npairs):
            u2 = jnp.concatenate(two(us[p]), axis=0)
            us[p] = us[p] + _dot(jnp.concatenate(pws[p], axis=1), u2)
        if j < steps - 1:
            for p in range(npairs):
                pws[p] = [_dot(pw, pw).astype(BF16) for pw in pws[p]]

    ys = []
    for p, sl in enumerate(lanes):
        r4 = jnp.concatenate(two(us[p]) + vss[p], axis=0).astype(BF16)
        ys.append(ybase[p] + _dot(w4s[p], r4))
        rh = jnp.concatenate(two(bh[:, sl]) + two(kh[:, sl]), axis=0)
        sblk[sl, :] = sps[p] * wl[:, sl] + _dot_tn(r4, rh)
    y = jnp.concatenate(ys, axis=1)
    y_ref[...] = _rwkv_head_norm(y, r, k, v, rk_ref, lnw_ref, lnb_ref, seg_ref, segt_ref).astype(y_ref.dtype)

    @pl.when(c == pl.num_programs(1) - 1)
    def _():
        sb = sblk[...]
        s_ref[...] = sb[:, 0:RWKV_HEAD_DIM] + sb[:, RWKV_HEAD_DIM:2 * RWKV_HEAD_DIM]


def _ssd_step_kernel(xbc_ref, t0_ref, t1_ref, t2_ref, z_ref, sm_ref, st_ref, cw_ref, cb_ref, bias_ref,
                     alog_ref, dexp_ref, nw_ref, segt_ref, y_ref, sto_ref):
    nseq = xbc_ref.shape[0]
    act = _silu(_step_conv(xbc_ref, (t0_ref, t1_ref, t2_ref), cw_ref, cb_ref))
    xs = act[:, 0:SSD_D_INNER]
    gn = SSD_GROUPS * SSD_D_STATE
    bm = act[:, SSD_D_INNER:SSD_D_INNER + gn]
    cm = act[:, SSD_D_INNER + gn:SSD_D_INNER + 2 * gn]
    lane = lax.broadcasted_iota(jnp.int32, (nseq, LANES), 1)
    head_lane = (lane >= SMALL_DT) & (lane < SMALL_DT + SSD_HEADS)
    dt = jnp.where(head_lane, _softplus(sm_ref[...] + bias_ref[...]), 0.0)
    segt = segt_ref[...]
    dec_t = _dot_exact_rhs(jnp.exp(dt * (-jnp.exp(alog_ref[...]))), segt).T
    xdt_t = (xs * _dot_exact_rhs(dt, segt)).T
    gw = SSD_GROUP_W
    rowi = lax.broadcasted_iota(jnp.int32, (nseq, gw), 0)
    ys = [jnp.zeros((nseq, gw), F32) for _ in range(SSD_GROUPS)]
    for s in range(nseq):
        for g in range(SSD_GROUPS):
            rs = slice(g * gw, (g + 1) * gw)
            ns = slice(g * SSD_D_STATE, (g + 1) * SSD_D_STATE)
            new = st_ref[s, rs, :] * dec_t[rs, s:s + 1] + xdt_t[rs, s:s + 1] * bm[s:s + 1, ns]
            sto_ref[s, rs, :] = new
            ys[g] = jnp.where(rowi == s, _dot_nt(cm[:, ns], new), ys[g])
    y = jnp.concatenate(ys, axis=1) + dexp_ref[...] * xs
    y_ref[...] = _group_rms(y * _silu(z_ref[...]), gw) * nw_ref[...]


def _mlstm_step_kernel(xm_ref, t0_ref, t1_ref, t2_ref, o_ref, sm_ref, n_ref, m_ref, c_ref, cw_ref, cb_ref,
                       bias_ref, wq_ref, wk_ref, wv_ref, nw_ref, skip_ref, h_ref, no_ref, mo_ref, co_ref):
    nseq = xm_ref.shape[0]
    raw = xm_ref[...]
    act = _silu(_step_conv(xm_ref, (t0_ref, t1_ref, t2_ref), cw_ref, cb_ref))
    q, k, v = _blockdiag_qkv(act, raw, wq_ref, wk_ref, wv_ref)
    logi = sm_ref[...] + bias_ref[...]
    logf = _log_sigmoid(logi)
    m_old = m_ref[...]
    m_all = m_old
    n_old = n_ref[...]
    lane = lax.broadcasted_iota(jnp.int32, (nseq, LANES), 1)
    hd = MLSTM_HEAD_DIM
    rowi = lax.broadcasted_iota(jnp.int32, (nseq, hd), 0)
    hs, ns = [], []
    for h in range(MLSTM_HEADS):
        cols = slice(h * hd, (h + 1) * hd)
        qh, kh, vh = q[:, cols], k[:, cols], v[:, cols]
        lf = logf[:, SMALL_F + h:SMALL_F + h + 1] + m_old[:, h:h + 1]
        li = logi[:, SMALL_I + h:SMALL_I + h + 1]
        m_new = jnp.maximum(lf, li)
        dc = jnp.exp(lf - m_new)
        kws = kh * jnp.exp(li - m_new)
        n_new = dc * n_old[:, cols] + kws
        den = jnp.sum(qh * n_new, axis=1, keepdims=True)
        kws_t = kws.T
        q_t = qh.T
        num = jnp.zeros((nseq, hd), F32)
        for s in range(nseq):
            c_new = dc[s:s + 1, :] * c_ref[s, cols, :] + kws_t[:, s:s + 1] * vh[s:s + 1, :]
            co_ref[s, cols, :] = c_new
            num = jnp.where(rowi == s, jnp.sum(q_t[:, s:s + 1] * c_new, axis=0, keepdims=True), num)
        hs.append(num / jnp.maximum(jnp.abs(den), jnp.exp(-m_new)))
        ns.append(n_new)
        m_all = jnp.where(lane == h, m_new, m_all)
    no_ref[...] = jnp.concatenate(ns, axis=1)
    mo_ref[...] = m_all
    hm = _group_rms(jnp.concatenate(hs, axis=1), hd) * nw_ref[...]
    h_ref[...] = (hm + skip_ref[...] * act) * _sigmoid(o_ref[...])


def _rwkv_step_kernel(r_ref, lw_ref, k_ref, v_ref, kk_ref, a_ref, s_ref, rk_ref, lnw_ref, lnb_ref,
                      seg_ref, segt_ref, y_ref, so_ref, yt):
    r, k, v, kk = (t[...] for t in (r_ref, k_ref, v_ref, kk_ref))
    n = RWKV_HEAD_DIM
    w_t, kk_t, ka_t, k_t, r_t, v_t = (x.T for x in (jnp.exp(lw_ref[...]), kk, kk * a_ref[...], k, r, v))
    for e in range(2):
        hs = slice(e * n, (e + 1) * n)
        w_h, kk_h, ka_h, k_h, r_h = w_t[hs], kk_t[hs], ka_t[hs], k_t[hs], r_t[hs]
        for i in range(n):
            row = e * n + i
            st = s_ref[e, i]
            sa = jnp.sum(st * kk_h, axis=0, keepdims=True)
            new = st * w_h - sa * ka_h + v_t[row:row + 1, :] * k_h
            so_ref[e, i] = new
            yt[row:row + 1, :] = jnp.sum(new * r_h, axis=0, keepdims=True)
    y_ref[...] = _rwkv_head_norm(yt[...].T, r, k, v, rk_ref, lnw_ref, lnb_ref, seg_ref, segt_ref)


def _params(sem):
    return pltpu.CompilerParams(dimension_semantics=sem, vmem_limit_bytes=VMEM_LIMIT)


def _call_name(kernel, n):
    fn = getattr(kernel, "func", kernel)
    return f"{fn.__name__.strip('_')}_{n}"


def _const_spec(shape, grid_rank):
    nd = len(shape)
    if grid_rank == 1:
        return pl.BlockSpec(shape, lambda i: (0,) * nd, pipeline_mode=pl.Buffered(1))
    return pl.BlockSpec(shape, lambda b, c: (0,) * nd, pipeline_mode=pl.Buffered(1))


def _row_tile(m, tile):
    return tile if m % tile == 0 else m


def _layer_spec(stacked, layer):
    nd = stacked.ndim
    return pl.BlockSpec((None,) + stacked.shape[1:], lambda i: (layer,) + (0,) * (nd - 1),
                        pipeline_mode=pl.Buffered(1))


def _rowwise_call(kernel, row_ins, const_ins, outs, m, tile=ROW_TILE, extra_ins=()):
    tm = _row_tile(m, tile)
    rows = lambda n: pl.BlockSpec((tm, n), lambda i: (i, 0))
    consts = [c if isinstance(c, tuple) else (c, None) for c in const_ins]
    return pl.pallas_call(
        kernel,
        name=_call_name(kernel, m),
        grid=(m // tm,),
        in_specs=([rows(a.shape[1]) for a in row_ins] + [spec for _, spec in extra_ins]
                  + [_const_spec(a.shape, 1) if layer is None else _layer_spec(a, layer) for a, layer in consts]),
        out_specs=[rows(n) for n, _ in outs],
        out_shape=[jax.ShapeDtypeStruct((m, n), dt) for n, dt in outs],
        compiler_params=_params(("parallel",)),
    )(*row_ins, *[a for a, _ in extra_ins], *[a for a, _ in consts])


def _scan_call(kernel, seq_ins, state_ins, const_ins, seq_out_widths, state_out_shapes, scratch):
    bsz, t = seq_ins[0].shape[:2]
    seq = lambda n: pl.BlockSpec((None, SCAN_CHUNK, n), lambda b, c: (b, c, 0))
    per_seq = lambda shape: pl.BlockSpec((None,) + tuple(shape[1:]), lambda b, c: (b,) + (0,) * (len(shape) - 1))
    return pl.pallas_call(
        kernel,
        name=_call_name(kernel, bsz),
        grid=(bsz, t // SCAN_CHUNK),
        in_specs=([seq(a.shape[2]) for a in seq_ins] + [per_seq(a.shape) for a in state_ins]
                  + [_const_spec(a.shape, 2) for a in const_ins]),
        out_specs=[seq(n) for n in seq_out_widths] + [per_seq(s) for s in state_out_shapes],
        out_shape=([jax.ShapeDtypeStruct((bsz, t, n), seq_ins[0].dtype) for n in seq_out_widths]
                   + [jax.ShapeDtypeStruct(s, F32) for s in state_out_shapes]),
        scratch_shapes=scratch,
        compiler_params=_params(("parallel", "arbitrary")),
    )(*seq_ins, *state_ins, *const_ins)


def _step_call(kernel, row_ins, state_ins, const_ins, row_out_widths, state_out_shapes):
    bsz = row_ins[0].shape[0]
    rows = lambda n: pl.BlockSpec((STEP_SEQS, n), lambda i: (i, 0))
    state = lambda shape: pl.BlockSpec((STEP_SEQS,) + tuple(shape[1:]), lambda i: (i, 0, 0))
    return pl.pallas_call(
        kernel,
        name=_call_name(kernel, bsz),
        grid=(bsz // STEP_SEQS,),
        in_specs=([rows(a.shape[1]) for a in row_ins] + [state(a.shape) for a in state_ins]
                  + [_const_spec(a.shape, 1) for a in const_ins]),
        out_specs=[rows(n) for n in row_out_widths] + [state(s) for s in state_out_shapes],
        out_shape=([jax.ShapeDtypeStruct((bsz, n), F32) for n in row_out_widths]
                   + [jax.ShapeDtypeStruct(s, F32) for s in state_out_shapes]),
        compiler_params=_params(("parallel",)),
    )(*row_ins, *state_ins, *const_ins)


def _rwkv_step_call(rows, wkv, rk, lnw, lnb, seg, segt):
    bsz = wkv.shape[0]
    assert bsz == LANES, "the single-token RWKV kernel keeps the sequences on the lanes"
    n = RWKV_HEAD_DIM
    pair = pl.BlockSpec((bsz, LANES), lambda i: (0, i))
    vec = pl.BlockSpec((1, LANES), lambda i: (0, i))
    state = pl.BlockSpec((2, n, n, bsz), lambda i: (i, 0, 0, 0))
    seg2, segt2 = seg[:LANES], segt[:, :LANES]
    y, new = pl.pallas_call(
        _rwkv_step_kernel,
        name=_call_name(_rwkv_step_kernel, bsz),
        grid=(RWKV_HEADS // 2,),
        in_specs=[pair] * 6 + [state] + [vec] * 3 + [_const_spec(seg2.shape, 1), _const_spec(segt2.shape, 1)],
        out_specs=[pair, state],
        out_shape=[jax.ShapeDtypeStruct((bsz, RWKV_D), F32), jax.ShapeDtypeStruct((RWKV_HEADS, n, n, bsz), F32)],
        scratch_shapes=[pltpu.VMEM((LANES, bsz), F32)],
        compiler_params=_params(("parallel",)),
    )(*rows, jnp.transpose(wkv, (1, 2, 3, 0)), rk, lnw, lnb, seg2, segt2)
    return y, jnp.transpose(new, (3, 0, 1, 2))


def _row(v):
    return v.reshape(1, -1).astype(F32)


def _pad_cols(w, width):
    return jnp.pad(w, ((0, 0), (0, width - w.shape[1])))


def _pad_rows_to(w, rows):
    return jnp.pad(w, ((0, rows - w.shape[0]), (0, 0)))


def _blockdiag_slabs(w, scale=1.0):
    hd = MLSTM_HEAD_DIM
    rows = (w * scale).reshape(-1, hd, QKV_BLOCK)
    col = jnp.arange(hd)
    pick = (col[None, :] % QKV_BLOCK == jnp.arange(QKV_BLOCK)[:, None]).astype(w.dtype)
    tiled = jnp.einsum('srd,dc->src', rows, pick, precision=lax.Precision.HIGHEST)
    same_block = col[:, None] // QKV_BLOCK == col[None, :] // QKV_BLOCK
    return jnp.where(same_block, tiled, 0.0).astype(BF16)


def _head_indicator():
    ch = jnp.arange(RWKV_D) // RWKV_HEAD_DIM
    seg = (ch[:, None] == jnp.arange(LANES)[None, :]).astype(BF16)
    return seg, seg.T


def _mixer0_consts(p):
    ssd = [p['conv_w'][:, :XBC_DIM], p['conv_b'][:, :XBC_DIM], p['small_bias'], p['ssd_a_log'], p['ssd_d'],
           p['ssd_norm'], p['segt']]
    ml = [p['conv_w'][:, XBC_DIM:], p['conv_b'][:, XBC_DIM:], p['small_bias'], p['ml_wq'], p['ml_wk'],
          p['ml_wv'], p['ml_norm'], p['ml_skip']]
    return ssd, ml


def _mixer0_chunked(xbc, xm, z, o_pre, small, conv, ssm, mc, mn, mm, p, bsz, t):
    seq = lambda a: a.reshape(bsz, t, -1)
    tail = jnp.pad(conv, ((0, 0), (SUBLANES - (CONV_W - 1), 0), (0, 0)))
    ssd_c, ml_c = _mixer0_consts(p)
    y_ssd, new_ssm = _scan_call(
        _ssd_kernel, [seq(xbc), seq(z), seq(small)],
        [tail[:, :, :XBC_DIM], ssm.reshape(bsz, SSD_D_INNER, SSD_D_STATE)], ssd_c,
        [SSD_D_INNER], [(bsz, SSD_D_INNER, SSD_D_STATE)],
        [pltpu.VMEM((SCAN_CHUNK + SUBLANES, XBC_DIM), F32), pltpu.VMEM((SSD_D_INNER, SSD_D_STATE), F32)])
    hm, new_c, new_n, new_m = _scan_call(
        _mlstm_kernel, [seq(xm), seq(o_pre), seq(small)],
        [tail[:, :, XBC_DIM:], mc.reshape(bsz, MLSTM_D_INNER, MLSTM_HEAD_DIM), mn.reshape(bsz, 1, MLSTM_D_INNER),
         _pad_cols(mm, LANES).reshape(bsz, 1, LANES)], ml_c,
        [MLSTM_D_INNER], [(bsz, MLSTM_D_INNER, MLSTM_HEAD_DIM), (bsz, 1, MLSTM_D_INNER), (bsz, 1, LANES)],
        [pltpu.VMEM((SCAN_CHUNK + SUBLANES, MLSTM_D_INNER), F32), pltpu.VMEM((MLSTM_D_INNER, MLSTM_HEAD_DIM), F32),
         pltpu.VMEM((1, MLSTM_D_INNER), F32), pltpu.VMEM((1, LANES), F32)])
    flat = lambda a: a.reshape(bsz * t, -1)
    return flat(y_ssd), flat(hm), new_ssm, new_c, new_n, new_m.reshape(bsz, LANES)


def _mixer0_step(xbc, xm, z, o_pre, small, conv, ssm, mc, mn, mm, p, bsz):
    ssd_c, ml_c = _mixer0_consts(p)
    tails = [conv[:, j, :] for j in range(CONV_W - 1)]
    y_ssd, new_ssm = _step_call(
        _ssd_step_kernel, [xbc] + [tl[:, :XBC_DIM] for tl in tails] + [z, small],
        [ssm.reshape(bsz, SSD_D_INNER, SSD_D_STATE)], ssd_c, [SSD_D_INNER], [(bsz, SSD_D_INNER, SSD_D_STATE)])
    hm, new_n, new_m, new_c = _step_call(
        _mlstm_step_kernel, [xm] + [tl[:, XBC_DIM:] for tl in tails]
        + [o_pre, small, mn.reshape(bsz, MLSTM_D_INNER), _pad_cols(mm, LANES)],
        [mc.reshape(bsz, MLSTM_D_INNER, MLSTM_HEAD_DIM)], ml_c,
        [MLSTM_D_INNER, MLSTM_D_INNER, LANES], [(bsz, MLSTM_D_INNER, MLSTM_HEAD_DIM)])
    return y_ssd, hm, new_ssm, new_c, new_n, new_m


def _trunk(x, conv, ssm, mc, mn, mm, shift, wkv, p):
    bsz, t, d = x.shape
    m = bsz * t
    step = t == 1
    assert step or t % SCAN_CHUNK == 0, "a group is either single-token or a multiple of the scan chunk"
    assert not step or bsz % STEP_SEQS == 0
    act = F32 if step else BF16
    x2 = x.reshape(m, d)

    z, xbc, xm, o_pre, small = _rowwise_call(
        _in0_kernel, [x2], [p['norm_mix0'], p['w_in0']],
        [(SSD_D_INNER, act), (XBC_DIM, act), (MLSTM_D_INNER, act), (MLSTM_D_INNER, act), (LANES, F32)], m,
        tile=WIDE_ROW_TILE)
    last_rows = lambda a: a.reshape(bsz, t, -1)[:, -(CONV_W - 1):].astype(F32)
    conv_in_tail = jnp.concatenate([last_rows(xbc), last_rows(xm)], axis=-1)
    new_conv = jnp.concatenate([conv, conv_in_tail], axis=1)[:, -(CONV_W - 1):]
    if step:
        y_ssd, hm, new_ssm, new_c, new_n, new_m = _mixer0_step(xbc, xm, z, o_pre, small, conv, ssm, mc, mn, mm, p, bsz)
    else:
        y_ssd, hm, new_ssm, new_c, new_n, new_m = _mixer0_chunked(xbc, xm, z, o_pre, small, conv, ssm, mc, mn, mm,
                                                                  p, bsz, t)
    xn1, x2b = _rowwise_call(
        _mix0_ffn_kernel, [y_ssd, hm, x2],
        [p['w_out0'], p['norm_ffn0'], (p['ffn_gu'], 0), (p['ffn_d'], 0), p['norm_mix1']], [(d, act), (d, F32)], m,
        tile=WIDE_ROW_TILE)

    rw_in_c = [p['rw_mu'], p['rw_wr'], p['rw_wk'], p['rw_wv'], p['rw_w1'], p['rw_w2'], p['rw_a1'], p['rw_a2'],
               p['rw_g1'], p['rw_g2'], p['rw_w0'], p['rw_a0'], p['rw_k_k'], p['rw_k_a'], p['seg'], p['segt']]
    rw_outs = [(d, act), (d, F32)] + [(d, act)] * 5
    if step:
        rw = _rowwise_call(_rw_in_kernel, [xn1, shift], rw_in_c, rw_outs, m)
    else:
        tm = _row_tile(t, ROW_TILE)
        tiles_per_seq = t // tm
        prev_spec = pl.BlockSpec((PREV_ROWS, d), lambda i: (jnp.maximum(i * (tm // PREV_ROWS) - 1, 0), 0))
        shift_spec = pl.BlockSpec((None, 1, d), lambda i: (i // tiles_per_seq, 0, 0))
        rw = _rowwise_call(functools.partial(_rw_in_shift_kernel, tiles_per_seq=tiles_per_seq), [xn1], rw_in_c,
                           rw_outs, m, tile=tm, extra_ins=[(xn1, prev_spec), (shift.reshape(bsz, 1, d), shift_spec)])
    g = rw[6]
    rw_c = [p['rw_r_k'], p['rw_ln_w'], p['rw_ln_b'], p['seg'], p['segt']]
    if step:
        y_rw, new_wkv = _rwkv_step_call(list(rw[:6]), wkv, *rw_c)
    else:
        wkv3 = wkv.reshape(bsz, RWKV_D, RWKV_HEAD_DIM)
        y_rw, new_wkv = _scan_call(
            _rwkv_kernel, [a.reshape(bsz, t, d) for a in rw[:6]], [wkv3], rw_c,
            [d], [wkv3.shape], [pltpu.VMEM((RWKV_D, 2 * RWKV_HEAD_DIM), F32)])
        y_rw = y_rw.reshape(m, d)
    new_wkv = new_wkv.reshape(wkv.shape)
    (y,) = _rowwise_call(
        _mix1_ffn_kernel, [y_rw, g, x2b],
        [p['rw_wo'], p['norm_ffn1'], (p['ffn_gu'], 1), (p['ffn_d'], 1), p['norm_final']], [(d, F32)], m, tile=WIDE_ROW_TILE)

    states = (new_conv, new_ssm.reshape(ssm.shape), new_c.reshape(mc.shape), new_n.reshape(mn.shape),
              new_m[:, :MLSTM_HEADS], xn1.reshape(bsz, t, d)[:, -1].astype(F32), new_wkv)
    return y.reshape(bsz, t, d), states


def kernel(x_prompt, x_sample, state_conv, state_ssm, state_mlstm_c, state_mlstm_n, state_mlstm_m, state_shift, state_wkv, norm_mix, norm_ffn, norm_final, w_in0, conv_w, conv_b, ssd_dt_bias, ssd_a_log, ssd_d, ssd_norm, ml_wq, ml_wk, ml_wv, ml_i_bias, ml_f_bias, ml_norm, ml_skip, w_out0, rw_mu, rw_wr, rw_wk, rw_wv, rw_wo, rw_w0, rw_w1, rw_w2, rw_a0, rw_a1, rw_a2, rw_g1, rw_g2, rw_k_k, rw_k_a, rw_r_k, rw_ln_w, rw_ln_b, ffn_w_gate_up, ffn_w_down):
    assert norm_mix.shape[0] == 2 and w_in0.shape[0] == 1 and rw_wr.shape[0] == 1, "two layers: SSD|mLSTM then RWKV-7"
    s1 = SSD_D_INNER
    s2 = s1 + XBC_DIM + MLSTM_D_INNER
    s3 = s2 + SSD_HEADS
    s4 = s3 + MLSTM_D_INNER
    s5 = s4 + MLSTM_HEADS
    w0 = w_in0[0]
    small_w = _pad_cols(jnp.concatenate([w0[:, s2:s3], w0[:, s4:s5], w0[:, s5:]], axis=1), LANES)
    seg, segt = _head_indicator()
    lora = lambda w1, w2, width: (_pad_cols(w1, width).astype(BF16), _pad_rows_to(w2, width).astype(BF16))
    rw_w1p, rw_w2p = lora(rw_w1[0], rw_w2[0], LANES)
    rw_a1p, rw_a2p = lora(rw_a1[0], rw_a2[0], LANES)
    rw_g1p, rw_g2p = lora(rw_g1[0], rw_g2[0], 2 * LANES)
    p = dict(
        norm_mix0=_row(norm_mix[0]), norm_mix1=_row(norm_mix[1]), norm_ffn0=_row(norm_ffn[0]),
        norm_ffn1=_row(norm_ffn[1]), norm_final=_row(norm_final),
        w_in0=jnp.concatenate([w0[:, :s2], w0[:, s3:s4], small_w], axis=1).astype(BF16),
        conv_w=conv_w[0], conv_b=_row(conv_b[0]),
        small_bias=_pad_cols(jnp.concatenate([_row(ssd_dt_bias[0]), _row(ml_i_bias[0]), _row(ml_f_bias[0])], axis=1),
                             LANES),
        ssd_a_log=_pad_cols(_row(ssd_a_log[0]), LANES), ssd_d=_row(jnp.repeat(ssd_d[0], SSD_HEAD_DIM)),
        ssd_norm=_row(ssd_norm[0]),
        ml_wq=_blockdiag_slabs(ml_wq[0]), ml_wk=_blockdiag_slabs(ml_wk[0], MLSTM_HEAD_DIM ** -0.5),
        ml_wv=_blockdiag_slabs(ml_wv[0]), ml_norm=_row(ml_norm[0]), ml_skip=_row(ml_skip[0]),
        w_out0=w_out0[0].astype(BF16),
        rw_mu=_pad_rows_to(rw_mu[0], SUBLANES),
        rw_wr=rw_wr[0].astype(BF16), rw_wk=rw_wk[0].astype(BF16), rw_wv=rw_wv[0].astype(BF16),
        rw_wo=rw_wo[0].astype(BF16), rw_w0=_row(rw_w0[0]), rw_a0=_row(rw_a0[0]),
        rw_w1=rw_w1p, rw_w2=rw_w2p, rw_a1=rw_a1p, rw_a2=rw_a2p, rw_g1=rw_g1p, rw_g2=rw_g2p,
        rw_k_k=_row(rw_k_k[0]), rw_k_a=_row(rw_k_a[0]), rw_r_k=_row(rw_r_k[0]),
        rw_ln_w=_row(rw_ln_w[0]), rw_ln_b=_row(rw_ln_b[0]),
        ffn_gu=ffn_w_gate_up.astype(BF16), ffn_d=ffn_w_down.astype(BF16),
        seg=seg, segt=segt,
    )
    bp = x_prompt.shape[0]
    zeros = lambda s: jnp.zeros((bp,) + s.shape[2:], F32)
    y_p, st_p = _trunk(x_prompt, zeros(state_conv), zeros(state_ssm), zeros(state_mlstm_c),
                       zeros(state_mlstm_n), zeros(state_mlstm_m), zeros(state_shift), zeros(state_wkv), p)
    y_s, st_s = _trunk(x_sample, state_conv[0], state_ssm[0], state_mlstm_c[0], state_mlstm_n[0],
                       state_mlstm_m[0], state_shift[0], state_wkv[0], p)
    out = [y_p, y_s]
    for a, b in zip(st_p, st_s):
        out += [a[None], b[None]]
    return tuple(out)
```

```python
import functools

import jax
import jax.numpy as jnp
from jax import lax
from jax.experimental import pallas as pl
from jax.experimental.pallas import tpu as pltpu

F32 = jnp.float32
BF16 = jnp.bfloat16

D_MODEL = 1024
NORM_EPS = 1e-5
CONV_W = 4
SSD_HEADS = 16
SSD_HEAD_DIM = 64
SSD_GROUPS = 2
SSD_D_STATE = 128
SSD_D_INNER = SSD_HEADS * SSD_HEAD_DIM
SSD_GROUP_W = SSD_D_INNER // SSD_GROUPS
XBC_DIM = SSD_D_INNER + 2 * SSD_GROUPS * SSD_D_STATE
MLSTM_HEADS = 4
MLSTM_HEAD_DIM = 256
MLSTM_D_INNER = MLSTM_HEADS * MLSTM_HEAD_DIM
QKV_BLOCK = 4
RWKV_HEADS = 16
RWKV_HEAD_DIM = 64
RWKV_D = RWKV_HEADS * RWKV_HEAD_DIM
RWKV_LN_EPS = 64e-5
D_FF = 2816
FF_CHUNK = 256
LANES = 128
SUBLANES = 8
SMALL_DT, SMALL_I, SMALL_F = 0, 16, 20
NEG = -1e30
EXP_NEG_HALF = 0.6065306597126334
SCAN_CHUNK = 128
STEP_SEQS = SUBLANES
ROW_TILE = 512
WIDE_ROW_TILE = 512
PREV_ROWS = 16
VMEM_LIMIT = 56 * 1024 * 1024


def _dot(a, b):
    return jnp.dot(a.astype(BF16), b.astype(BF16), preferred_element_type=F32)


def _dot_nt(a, b):
    return lax.dot_general(a.astype(BF16), b.astype(BF16), (((1,), (1,)), ((), ())),
                           preferred_element_type=F32)


def _dot_tn(a, b):
    return lax.dot_general(a.astype(BF16), b.astype(BF16), (((0,), (0,)), ((), ())),
                           preferred_element_type=F32)


def _split(x, terms):
    parts = []
    rem = x
    for _ in range(terms):
        p = rem.astype(BF16)
        parts.append(p)
        rem = rem - p.astype(F32)
    return parts


def _dot_exact_rhs(x, m, terms=2):
    out = None
    for p in _split(x, terms):
        t = jnp.dot(p, m, preferred_element_type=F32)
        out = t if out is None else out + t
    return out


def _tri_cumsum(tri, x, terms=3):
    out = None
    for p in _split(x, terms):
        t = jnp.dot(tri, p, preferred_element_type=F32)
        out = t if out is None else out + t
    return out


def _rms_rows(x, g):
    return x * lax.rsqrt(jnp.mean(x * x, axis=-1, keepdims=True) + NORM_EPS) * g


def _group_rms(x, width):
    outs = []
    for lo in range(0, x.shape[1], width):
        grp = x[:, lo:lo + width]
        outs.append(grp * lax.rsqrt(jnp.mean(grp * grp, axis=-1, keepdims=True) + NORM_EPS))
    return jnp.concatenate(outs, axis=1)


def _sigmoid(x):
    return 0.5 * jnp.tanh(0.5 * x) + 0.5


def _silu(x):
    return x * _sigmoid(x)


def _softplus(x):
    return jnp.maximum(x, 0.0) + jnp.log(1.0 + jnp.exp(-jnp.abs(x)))


def _log_sigmoid(x):
    return jnp.minimum(x, 0.0) - jnp.log(1.0 + jnp.exp(-jnp.abs(x)))


def _tri_masks(n):
    row = lax.broadcasted_iota(jnp.int32, (n, n), 0)
    col = lax.broadcasted_iota(jnp.int32, (n, n), 1)
    return row >= col, row > col


def _chunk_conv(ext, u, conv0_ref, cw_ref, cb_ref, first):
    rows = u.shape[0]

    @pl.when(first)
    def _():
        ext[0:SUBLANES, :] = conv0_ref[...]

    @pl.when(jnp.logical_not(first))
    def _():
        ext[0:SUBLANES, :] = ext[rows:rows + SUBLANES, :]

    ext[SUBLANES:SUBLANES + rows, :] = u
    e = ext[...]
    conv = cb_ref[...] + u * cw_ref[CONV_W - 1:CONV_W, :]
    for j in range(CONV_W - 1):
        back = pltpu.roll(e, shift=CONV_W - 1 - j, axis=0)[SUBLANES:SUBLANES + rows, :]
        conv = conv + back * cw_ref[j:j + 1, :]
    return conv


def _step_conv(u_ref, tail_refs, cw_ref, cb_ref):
    conv = cb_ref[...] + u_ref[...] * cw_ref[CONV_W - 1:CONV_W, :]
    for j, t in enumerate(tail_refs):
        conv = conv + t[...] * cw_ref[j:j + 1, :]
    return conv


def _blockdiag_qkv(act, raw, wq_ref, wk_ref, wv_ref):
    hd = MLSTM_HEAD_DIM
    cat = lambda x, w: jnp.concatenate(
        [_dot(x[:, s * hd:(s + 1) * hd], w[s]) for s in range(MLSTM_HEADS)], axis=1)
    return cat(act, wq_ref), cat(act, wk_ref), cat(raw, wv_ref)


def _rwkv_head_norm(y, r, k, v, rk_ref, lnw_ref, lnb_ref, seg_ref, segt_ref):
    seg = seg_ref[...]
    segt = segt_ref[...]
    head_sum = lambda x: _dot_exact_rhs(_dot_exact_rhs(x, seg), segt, terms=1)
    inv_n = 1.0 / RWKV_HEAD_DIM
    dlt = y - head_sum(y) * inv_n
    yn = dlt * lax.rsqrt(head_sum(dlt * dlt) * inv_n + RWKV_LN_EPS)
    return yn * lnw_ref[...] + lnb_ref[...] + head_sum(r * k * rk_ref[...]) * v


def _in0_kernel(x_ref, g_ref, w_ref, z_ref, xbc_ref, xm_ref, o_ref, sm_ref):
    xn = _rms_rows(x_ref[...], g_ref[...]).astype(BF16)
    off = 0
    for ref in (z_ref, xbc_ref, xm_ref, o_ref, sm_ref):
        n = ref.shape[-1]
        ref[...] = jnp.dot(xn, w_ref[:, off:off + n], preferred_element_type=F32).astype(ref.dtype)
        off += n


def _swiglu_residual(x, g_ref, wgu_ref, wd_ref):
    xn = _rms_rows(x, g_ref[...]).astype(BF16)
    acc = None
    for c in range(D_FF // FF_CHUNK):
        lo = c * FF_CHUNK
        gate = jnp.dot(xn, wgu_ref[:, lo:lo + FF_CHUNK], preferred_element_type=F32)
        up = jnp.dot(xn, wgu_ref[:, D_FF + lo:D_FF + lo + FF_CHUNK], preferred_element_type=F32)
        part = _dot(_silu(gate) * up, wd_ref[lo:lo + FF_CHUNK, :])
        acc = part if acc is None else acc + part
    return x + acc


def _mix0_ffn_kernel(y_ref, h_ref, x_ref, wo_ref, g_ref, wgu_ref, wd_ref, pg_ref, xn_ref, xo_ref):
    mixed = jnp.concatenate([y_ref[...].astype(BF16), h_ref[...].astype(BF16)], axis=1)
    x1 = x_ref[...] + jnp.dot(mixed, wo_ref[...], preferred_element_type=F32)
    x2 = _swiglu_residual(x1, g_ref, wgu_ref, wd_ref)
    xo_ref[...] = x2
    xn_ref[...] = _rms_rows(x2, pg_ref[...]).astype(xn_ref.dtype)


def _mix1_ffn_kernel(y_ref, gate_ref, x_ref, wo_ref, g_ref, wgu_ref, wd_ref, pg_ref, o_ref):
    x3 = x_ref[...] + _dot(y_ref[...].astype(F32) * gate_ref[...].astype(F32), wo_ref[...])
    o_ref[...] = _rms_rows(_swiglu_residual(x3, g_ref, wgu_ref, wd_ref), pg_ref[...])


def _rw_in_body(xn, xp, mu_ref, wr_ref, wk_ref, wv_ref, w1_ref, w2_ref, a1_ref, a2_ref,
                g1_ref, g2_ref, w0_ref, a0_ref, kk_w_ref, ka_w_ref, seg_ref, segt_ref,
                r_ref, lw_ref, k_ref, v_ref, kk_ref, a_ref, g_ref):
    xx = xp - xn
    xr, xw, xk, xv, xa, xg = (xn + xx * mu_ref[c:c + 1, :] for c in range(6))
    r_ref[...] = _dot(xr, wr_ref[...]).astype(r_ref.dtype)
    k = _dot(xk, wk_ref[...])
    v_ref[...] = _dot(xv, wv_ref[...]).astype(v_ref.dtype)
    z = w0_ref[...] + _dot(jnp.tanh(_dot(xw, w1_ref[...])), w2_ref[...])
    lw_ref[...] = -EXP_NEG_HALF * _sigmoid(z)
    a = _sigmoid(a0_ref[...] + _dot(_dot(xa, a1_ref[...]), a2_ref[...]))
    a_ref[...] = a.astype(a_ref.dtype)
    g_ref[...] = _dot(_sigmoid(_dot(xg, g1_ref[...])), g2_ref[...]).astype(g_ref.dtype)
    kk = k * kk_w_ref[...]
    ss = _dot_exact_rhs(_dot_exact_rhs(kk * kk, seg_ref[...]), segt_ref[...])
    kk_ref[...] = (kk * lax.rsqrt(jnp.maximum(ss, 1e-24))).astype(kk_ref.dtype)
    k_ref[...] = (k * (1.0 + (a - 1.0) * ka_w_ref[...])).astype(k_ref.dtype)


def _rw_in_kernel(xn_ref, xp_ref, *refs):
    _rw_in_body(xn_ref[...].astype(F32), xp_ref[...].astype(F32), *refs)


def _rw_in_shift_kernel(xn_ref, prev_ref, shift_ref, *refs, tiles_per_seq):
    xn = xn_ref[...].astype(F32)
    starts_seq = pl.program_id(0) % tiles_per_seq == 0
    last = prev_ref.shape[0] - 1
    prev = jnp.where(starts_seq, shift_ref[...], prev_ref[last:last + 1, :].astype(F32))
    row = lax.broadcasted_iota(jnp.int32, xn.shape, 0)
    xp = jnp.where(row == 0, prev, pltpu.roll(xn, shift=1, axis=0))
    _rw_in_body(xn, xp, *refs)


def _ssd_kernel(xbc_ref, z_ref, sm_ref, conv0_ref, st0_ref, cw_ref, cb_ref, bias_ref, alog_ref,
                dexp_ref, nw_ref, segt_ref, y_ref, st_ref, ext, state):
    c = pl.program_id(1)
    first = c == 0
    rows = xbc_ref.shape[0]

    @pl.when(first)
    def _():
        state[...] = st0_ref[...]

    act = _silu(_chunk_conv(ext, xbc_ref[...].astype(F32), conv0_ref, cw_ref, cb_ref, first))
    xs = act[:, 0:SSD_D_INNER]
    gn = SSD_GROUPS * SSD_D_STATE
    bm = act[:, SSD_D_INNER:SSD_D_INNER + gn]
    cm = act[:, SSD_D_INNER + gn:SSD_D_INNER + 2 * gn]

    tril, _ = _tri_masks(rows)
    tri = jnp.where(tril, 1.0, 0.0).astype(BF16)
    lane = lax.broadcasted_iota(jnp.int32, (rows, LANES), 1)
    head_lane = (lane >= SMALL_DT) & (lane < SMALL_DT + SSD_HEADS)
    dt = jnp.where(head_lane, _softplus(sm_ref[...] + bias_ref[...]), 0.0)
    acum = _tri_cumsum(tri, dt * (-jnp.exp(alog_ref[...])))
    acum_t = acum.T
    dt_t = dt.T
    segt = segt_ref[...]
    eac_x = _dot_exact_rhs(jnp.exp(acum), segt, terms=1)
    wend_x = _dot_exact_rhs(jnp.exp(acum[rows - 1:rows, :] - acum) * dt, segt, terms=1)
    xw = xs * wend_x

    half = lane < SSD_HEAD_DIM
    heads_per_group = SSD_HEADS // SSD_GROUPS
    pairs = []
    for g in range(SSD_GROUPS):
        bg = bm[:, g * SSD_D_STATE:(g + 1) * SSD_D_STATE]
        cg = cm[:, g * SSD_D_STATE:(g + 1) * SSD_D_STATE]
        cb = _dot_nt(cg, bg)
        for q in range(heads_per_group // 2):
            wts = []
            for e in range(2):
                j = g * heads_per_group + 2 * q + e
                seg = acum[:, j:j + 1] - acum_t[j:j + 1, :]
                wts.append(jnp.exp(jnp.where(tril, seg, NEG)) * cb * dt_t[j:j + 1, :])
            p = (g * heads_per_group) // 2 + q
            xp = xs[:, p * LANES:(p + 1) * LANES]
            rhs = jnp.concatenate([jnp.where(half, xp, 0.0), jnp.where(half, 0.0, xp)], axis=0)
            pairs.append(_dot(jnp.concatenate(wts, axis=1), rhs))
    y = jnp.concatenate(pairs, axis=1)

    st = state[...]
    gw = SSD_GROUP_W
    y_state = jnp.concatenate(
        [_dot_nt(cm[:, g * SSD_D_STATE:(g + 1) * SSD_D_STATE], st[g * gw:(g + 1) * gw, :])
         for g in range(SSD_GROUPS)], axis=1)
    y = y + eac_x * y_state + dexp_ref[...] * xs
    y_ref[...] = (_group_rms(y * _silu(z_ref[...].astype(F32)), gw) * nw_ref[...]).astype(y_ref.dtype)

    dcol = jnp.exp(acum_t[:, rows - 1:rows])
    for g in range(SSD_GROUPS):
        upd = _dot_tn(xw[:, g * gw:(g + 1) * gw], bm[:, g * SSD_D_STATE:(g + 1) * SSD_D_STATE])
        for h in range(heads_per_group):
            j = g * heads_per_group + h
            lo = j * SSD_HEAD_DIM
            state[lo:lo + SSD_HEAD_DIM, :] = (st[lo:lo + SSD_HEAD_DIM, :] * dcol[j:j + 1, :]
                                              + upd[h * SSD_HEAD_DIM:(h + 1) * SSD_HEAD_DIM, :])

    @pl.when(c == pl.num_programs(1) - 1)
    def _():
        st_ref[...] = state[...]


def _mlstm_kernel(xm_ref, o_ref, sm_ref, conv0_ref, c0_ref, n0_ref, m0_ref, cw_ref, cb_ref, bias_ref,
                  wq_ref, wk_ref, wv_ref, nw_ref, skip_ref, h_ref, c_ref, n_ref, m_ref,
                  ext, cst, nst, mst):
    c = pl.program_id(1)
    first = c == 0
    rows = xm_ref.shape[0]

    @pl.when(first)
    def _():
        cst[...] = c0_ref[...]
        nst[...] = n0_ref[...]
        mst[...] = m0_ref[...]

    raw = xm_ref[...].astype(F32)
    act = _silu(_chunk_conv(ext, raw, conv0_ref, cw_ref, cb_ref, first))
    q, k, v = _blockdiag_qkv(act, raw, wq_ref, wk_ref, wv_ref)

    tril, _ = _tri_masks(rows)
    tri = jnp.where(tril, 1.0, 0.0).astype(BF16)
    logi = sm_ref[...] + bias_ref[...]
    bcum = _tri_cumsum(tri, _log_sigmoid(logi))
    bcum_t = bcum.T
    logi_t = logi.T
    m_all = mst[...]
    lane1 = lax.broadcasted_iota(jnp.int32, (1, LANES), 1)
    hd = MLSTM_HEAD_DIM

    hs = []
    for h in range(MLSTM_HEADS):
        qh, kh, vh = (t[:, h * hd:(h + 1) * hd] for t in (q, k, v))
        qb, vb = qh.astype(BF16), vh.astype(BF16)
        bc = bcum[:, SMALL_F + h:SMALL_F + h + 1]
        bct = bcum_t[SMALL_F + h:SMALL_F + h + 1, :]
        lit = logi_t[SMALL_I + h:SMALL_I + h + 1, :]
        li = logi[:, SMALL_I + h:SMALL_I + h + 1]
        m_h = m_all[:, h:h + 1]
        c_h = cst[h * hd:(h + 1) * hd, :]
        n_h = nst[:, h * hd:(h + 1) * hd]
        dlog = jnp.where(tril, bc - bct + lit, NEG)
        inter = bc + m_h
        m_t = jnp.maximum(inter, jnp.max(dlog, axis=1, keepdims=True))
        s = _dot_nt(qb, kh) * jnp.exp(dlog - m_t)
        w_inter = jnp.exp(inter - m_t)
        num = _dot(s, vb) + w_inter * _dot(qb, c_h)
        den = jnp.sum(s, axis=1, keepdims=True) + w_inter * jnp.sum(qh * n_h, axis=1, keepdims=True)
        hs.append(num * (1.0 / jnp.maximum(jnp.abs(den), jnp.exp(-m_t))))
        b_end = bc[rows - 1:rows, :]
        wlog = b_end - bc + li
        m_new = jnp.maximum(b_end + m_h, jnp.max(wlog, axis=0, keepdims=True))
        dc = jnp.exp(b_end + m_h - m_new)
        kws = kh * jnp.exp(wlog - m_new)
        cst[h * hd:(h + 1) * hd, :] = dc * c_h + _dot_tn(kws, vb)
        nst[:, h * hd:(h + 1) * hd] = dc * n_h + jnp.sum(kws, axis=0, keepdims=True)
        m_all = jnp.where(lane1 == h, m_new, m_all)
    mst[...] = m_all

    hm = _group_rms(jnp.concatenate(hs, axis=1), hd) * nw_ref[...]
    h_ref[...] = ((hm + skip_ref[...] * act) * _sigmoid(o_ref[...].astype(F32))).astype(h_ref.dtype)

    @pl.when(c == pl.num_programs(1) - 1)
    def _():
        c_ref[...] = cst[...]
        n_ref[...] = nst[...]
        m_ref[...] = mst[...]


def _rwkv_kernel(r_ref, lw_ref, k_ref, v_ref, kk_ref, a_ref, s0_ref, rk_ref, lnw_ref, lnb_ref,
                 seg_ref, segt_ref, y_ref, s_ref, sblk):
    c = pl.program_id(1)
    rows = r_ref.shape[0]

    @pl.when(c == 0)
    def _():
        s0 = s0_ref[...]
        shape = (RWKV_D, 2 * RWKV_HEAD_DIM)
        head_parity = (lax.broadcasted_iota(jnp.int32, shape, 0) // RWKV_HEAD_DIM) % 2
        lane_half = lax.broadcasted_iota(jnp.int32, shape, 1) // RWKV_HEAD_DIM
        sblk[...] = jnp.where(head_parity == lane_half, jnp.concatenate([s0, s0], axis=1), 0.0)

    r, lw, k, v, kk, a = (t[...].astype(F32) for t in (r_ref, lw_ref, k_ref, v_ref, kk_ref, a_ref))
    tril, strict = _tri_masks(rows)
    tri = jnp.where(tril, 1.0, 0.0).astype(BF16)
    cum = _tri_cumsum(tri, lw, terms=2)
    last = cum[rows - 1:rows, :]
    einv = jnp.exp(-cum)
    eend = jnp.exp(last - cum)
    ka = kk * a
    at = -(kk * jnp.exp(cum - lw))
    rt = r * jnp.exp(cum)
    bt = ka * einv
    kt = k * einv
    bh = ka * eend
    kh = k * eend
    wl = jnp.exp(last)
    half = lax.broadcasted_iota(jnp.int32, (rows, LANES), 1) < RWKV_HEAD_DIM
    steps = rows.bit_length() - 1

    def two(x):
        return [jnp.where(half, x, 0.0), jnp.where(half, 0.0, x)]

    npairs = RWKV_HEADS // 2
    tril4 = jnp.concatenate([tril] * 4, axis=1)
    strict4 = jnp.concatenate([strict] * 4, axis=1)
    lanes = [slice(p * LANES, (p + 1) * LANES) for p in range(npairs)]

    sps, ybase, us, pws, w4s, vss = [], [], [], [], [], []
    for sl in lanes:
        a2 = jnp.concatenate([at[:, sl], rt[:, sl]], axis=0).astype(BF16)
        rb = jnp.concatenate(two(bt[:, sl]) + two(kt[:, sl]), axis=0).astype(BF16)
        sp = sblk[sl, :]
        pbk = _dot_nt(a2, rb)
        xs = _dot_nt(a2, sp)
        vs = two(v[:, sl])
        top = jnp.where(strict4, pbk[0:rows, :], 0.0)
        us.append(xs[0:rows, :] + _dot(top[:, 2 * rows:4 * rows], jnp.concatenate(vs, axis=0)))
        pws.append([top[:, e * rows:(e + 1) * rows].astype(BF16) for e in range(2)])
        sps.append(sp)
        ybase.append(xs[rows:2 * rows, :])
        vss.append(vs)
        w4s.append(jnp.where(tril4, pbk[rows:2 * rows, :], 0.0).astype(BF16))

    for j in range(steps):
        for p in range(npairs):
            u2 = jnp.concatenate(two(us[p]), axis=0)
            us[p] = us[p] + _dot(jnp.concatenate(pws[p], axis=1), u2)
        if j < steps - 1:
            for p in range(npairs):
                pws[p] = [_dot(pw, pw).astype(BF16) for pw in pws[p]]

    ys = []
    for p, sl in enumerate(lanes):
        r4 = jnp.concatenate(two(us[p]) + vss[p], axis=0).astype(BF16)
        ys.append(ybase[p] + _dot(w4s[p], r4))
        rh = jnp.concatenate(two(bh[:, sl]) + two(kh[:, sl]), axis=0)
        sblk[sl, :] = sps[p] * wl[:, sl] + _dot_tn(r4, rh)
    y = jnp.concatenate(ys, axis=1)
    y_ref[...] = _rwkv_head_norm(y, r, k, v, rk_ref, lnw_ref, lnb_ref, seg_ref, segt_ref).astype(y_ref.dtype)

    @pl.when(c == pl.num_programs(1) - 1)
    def _():
        sb = sblk[...]
        s_ref[...] = sb[:, 0:RWKV_HEAD_DIM] + sb[:, RWKV_HEAD_DIM:2 * RWKV_HEAD_DIM]


def _ssd_step_kernel(xbc_ref, t0_ref, t1_ref, t2_ref, z_ref, sm_ref, st_ref, cw_ref, cb_ref, bias_ref,
                     alog_ref, dexp_ref, nw_ref, segt_ref, y_ref, sto_ref):
    nseq = xbc_ref.shape[0]
    act = _silu(_step_conv(xbc_ref, (t0_ref, t1_ref, t2_ref), cw_ref, cb_ref))
    xs = act[:, 0:SSD_D_INNER]
    gn = SSD_GROUPS * SSD_D_STATE
    bm = act[:, SSD_D_INNER:SSD_D_INNER + gn]
    cm = act[:, SSD_D_INNER + gn:SSD_D_INNER + 2 * gn]
    lane = lax.broadcasted_iota(jnp.int32, (nseq, LANES), 1)
    head_lane = (lane >= SMALL_DT) & (lane < SMALL_DT + SSD_HEADS)
    dt = jnp.where(head_lane, _softplus(sm_ref[...] + bias_ref[...]), 0.0)
    segt = segt_ref[...]
    dec_t = _dot_exact_rhs(jnp.exp(dt * (-jnp.exp(alog_ref[...]))), segt).T
    xdt_t = (xs * _dot_exact_rhs(dt, segt)).T
    gw = SSD_GROUP_W
    rowi = lax.broadcasted_iota(jnp.int32, (nseq, gw), 0)
    ys = [jnp.zeros((nseq, gw), F32) for _ in range(SSD_GROUPS)]
    for s in range(nseq):
        for g in range(SSD_GROUPS):
            rs = slice(g * gw, (g + 1) * gw)
            ns = slice(g * SSD_D_STATE, (g + 1) * SSD_D_STATE)
            new = st_ref[s, rs, :] * dec_t[rs, s:s + 1] + xdt_t[rs, s:s + 1] * bm[s:s + 1, ns]
            sto_ref[s, rs, :] = new
            ys[g] = jnp.where(rowi == s, _dot_nt(cm[:, ns], new), ys[g])
    y = jnp.concatenate(ys, axis=1) + dexp_ref[...] * xs
    y_ref[...] = _group_rms(y * _silu(z_ref[...]), gw) * nw_ref[...]


def _mlstm_step_kernel(xm_ref, t0_ref, t1_ref, t2_ref, o_ref, sm_ref, n_ref, m_ref, c_ref, cw_ref, cb_ref,
                       bias_ref, wq_ref, wk_ref, wv_ref, nw_ref, skip_ref, h_ref, no_ref, mo_ref, co_ref):
    nseq = xm_ref.shape[0]
    raw = xm_ref[...]
    act = _silu(_step_conv(xm_ref, (t0_ref, t1_ref, t2_ref), cw_ref, cb_ref))
    q, k, v = _blockdiag_qkv(act, raw, wq_ref, wk_ref, wv_ref)
    logi = sm_ref[...] + bias_ref[...]
    logf = _log_sigmoid(logi)
    m_old = m_ref[...]
    m_all = m_old
    n_old = n_ref[...]
    lane = lax.broadcasted_iota(jnp.int32, (nseq, LANES), 1)
    hd = MLSTM_HEAD_DIM
    rowi = lax.broadcasted_iota(jnp.int32, (nseq, hd), 0)
    hs, ns = [], []
    for h in range(MLSTM_HEADS):
        cols = slice(h * hd, (h + 1) * hd)
        qh, kh, vh = q[:, cols], k[:, cols], v[:, cols]
        lf = logf[:, SMALL_F + h:SMALL_F + h + 1] + m_old[:, h:h + 1]
        li = logi[:, SMALL_I + h:SMALL_I + h + 1]
        m_new = jnp.maximum(lf, li)
        dc = jnp.exp(lf - m_new)
        kws = kh * jnp.exp(li - m_new)
        n_new = dc * n_old[:, cols] + kws
        den = jnp.sum(qh * n_new, axis=1, keepdims=True)
        kws_t = kws.T
        q_t = qh.T
        num = jnp.zeros((nseq, hd), F32)
        for s in range(nseq):
            c_new = dc[s:s + 1, :] * c_ref[s, cols, :] + kws_t[:, s:s + 1] * vh[s:s + 1, :]
            co_ref[s, cols, :] = c_new
            num = jnp.where(rowi == s, jnp.sum(q_t[:, s:s + 1] * c_new, axis=0, keepdims=True), num)
        hs.append(num / jnp.maximum(jnp.abs(den), jnp.exp(-m_new)))
        ns.append(n_new)
        m_all = jnp.where(lane == h, m_new, m_all)
    no_ref[...] = jnp.concatenate(ns, axis=1)
    mo_ref[...] = m_all
    hm = _group_rms(jnp.concatenate(hs, axis=1), hd) * nw_ref[...]
    h_ref[...] = (hm + skip_ref[...] * act) * _sigmoid(o_ref[...])


def _rwkv_step_kernel(r_ref, lw_ref, k_ref, v_ref, kk_ref, a_ref, s_ref, rk_ref, lnw_ref, lnb_ref,
                      seg_ref, segt_ref, y_ref, so_ref, yt):
    r, k, v, kk = (t[...] for t in (r_ref, k_ref, v_ref, kk_ref))
    n = RWKV_HEAD_DIM
    w_t, kk_t, ka_t, k_t, r_t, v_t = (x.T for x in (jnp.exp(lw_ref[...]), kk, kk * a_ref[...], k, r, v))
    for e in range(2):
        hs = slice(e * n, (e + 1) * n)
        w_h, kk_h, ka_h, k_h, r_h = w_t[hs], kk_t[hs], ka_t[hs], k_t[hs], r_t[hs]
        for i in range(n):
            row = e * n + i
            st = s_ref[e, i]
            sa = jnp.sum(st * kk_h, axis=0, keepdims=True)
            new = st * w_h - sa * ka_h + v_t[row:row + 1, :] * k_h
            so_ref[e, i] = new
            yt[row:row + 1, :] = jnp.sum(new * r_h, axis=0, keepdims=True)
    y_ref[...] = _rwkv_head_norm(yt[...].T, r, k, v, rk_ref, lnw_ref, lnb_ref, seg_ref, segt_ref)


def _params(sem):
    return pltpu.CompilerParams(dimension_semantics=sem, vmem_limit_bytes=VMEM_LIMIT)


def _call_name(kernel, n):
    fn = getattr(kernel, "func", kernel)
    return f"{fn.__name__.strip('_')}_{n}"


def _const_spec(shape, grid_rank):
    nd = len(shape)
    if grid_rank == 1:
        return pl.BlockSpec(shape, lambda i: (0,) * nd, pipeline_mode=pl.Buffered(1))
    return pl.BlockSpec(shape, lambda b, c: (0,) * nd, pipeline_mode=pl.Buffered(1))


def _row_tile(m, tile):
    return tile if m % tile == 0 else m


def _layer_spec(stacked, layer):
    nd = stacked.ndim
    return pl.BlockSpec((None,) + stacked.shape[1:], lambda i: (layer,) + (0,) * (nd - 1),
                        pipeline_mode=pl.Buffered(1))


def _rowwise_call(kernel, row_ins, const_ins, outs, m, tile=ROW_TILE, extra_ins=()):
    tm = _row_tile(m, tile)
    rows = lambda n: pl.BlockSpec((tm, n), lambda i: (i, 0))
    consts = [c if isinstance(c, tuple) else (c, None) for c in const_ins]
    return pl.pallas_call(
        kernel,
        name=_call_name(kernel, m),
        grid=(m // tm,),
        in_specs=([rows(a.shape[1]) for a in row_ins] + [spec for _, spec in extra_ins]
                  + [_const_spec(a.shape, 1) if layer is None else _layer_spec(a, layer) for a, layer in consts]),
        out_specs=[rows(n) for n, _ in outs],
        out_shape=[jax.ShapeDtypeStruct((m, n), dt) for n, dt in outs],
        compiler_params=_params(("parallel",)),
    )(*row_ins, *[a for a, _ in extra_ins], *[a for a, _ in consts])


def _scan_call(kernel, seq_ins, state_ins, const_ins, seq_out_widths, state_out_shapes, scratch):
    bsz, t = seq_ins[0].shape[:2]
    seq = lambda n: pl.BlockSpec((None, SCAN_CHUNK, n), lambda b, c: (b, c, 0))
    per_seq = lambda shape: pl.BlockSpec((None,) + tuple(shape[1:]), lambda b, c: (b,) + (0,) * (len(shape) - 1))
    return pl.pallas_call(
        kernel,
        name=_call_name(kernel, bsz),
        grid=(bsz, t // SCAN_CHUNK),
        in_specs=([seq(a.shape[2]) for a in seq_ins] + [per_seq(a.shape) for a in state_ins]
                  + [_const_spec(a.shape, 2) for a in const_ins]),
        out_specs=[seq(n) for n in seq_out_widths] + [per_seq(s) for s in state_out_shapes],
        out_shape=([jax.ShapeDtypeStruct((bsz, t, n), seq_ins[0].dtype) for n in seq_out_widths]
                   + [jax.ShapeDtypeStruct(s, F32) for s in state_out_shapes]),
        scratch_shapes=[pltpu.VMEM(shape, dt) for shape, dt in scratch],
        compiler_params=_params(("parallel", "arbitrary")),
    )(*seq_ins, *state_ins, *const_ins)


def _step_call(kernel, row_ins, state_ins, const_ins, row_out_widths, state_out_shapes):
    bsz = row_ins[0].shape[0]
    rows = lambda n: pl.BlockSpec((STEP_SEQS, n), lambda i: (i, 0))
    state = lambda shape: pl.BlockSpec((STEP_SEQS,) + tuple(shape[1:]), lambda i: (i, 0, 0))
    return pl.pallas_call(
        kernel,
        name=_call_name(kernel, bsz),
        grid=(bsz // STEP_SEQS,),
        in_specs=([rows(a.shape[1]) for a in row_ins] + [state(a.shape) for a in state_ins]
                  + [_const_spec(a.shape, 1) for a in const_ins]),
        out_specs=[rows(n) for n in row_out_widths] + [state(s) for s in state_out_shapes],
        out_shape=([jax.ShapeDtypeStruct((bsz, n), F32) for n in row_out_widths]
                   + [jax.ShapeDtypeStruct(s, F32) for s in state_out_shapes]),
        compiler_params=_params(("parallel",)),
    )(*row_ins, *state_ins, *const_ins)


def _rwkv_step_call(rows, wkv, rk, lnw, lnb, seg, segt):
    bsz = wkv.shape[0]
    assert bsz == LANES, "the single-token RWKV kernel keeps the sequences on the lanes"
    n = RWKV_HEAD_DIM
    pair = pl.BlockSpec((bsz, LANES), lambda i: (0, i))
    vec = pl.BlockSpec((1, LANES), lambda i: (0, i))
    state = pl.BlockSpec((2, n, n, bsz), lambda i: (i, 0, 0, 0))
    seg2, segt2 = seg[:LANES], segt[:, :LANES]
    y, new = pl.pallas_call(
        _rwkv_step_kernel,
        name=_call_name(_rwkv_step_kernel, bsz),
        grid=(RWKV_HEADS // 2,),
        in_specs=[pair] * 6 + [state] + [vec] * 3 + [_const_spec(seg2.shape, 1), _const_spec(segt2.shape, 1)],
        out_specs=[pair, state],
        out_shape=[jax.ShapeDtypeStruct((bsz, RWKV_D), F32), jax.ShapeDtypeStruct((RWKV_HEADS, n, n, bsz), F32)],
        scratch_shapes=[pltpu.VMEM((LANES, bsz), F32)],
        compiler_params=_params(("parallel",)),
    )(*rows, jnp.transpose(wkv, (1, 2, 3, 0)), rk, lnw, lnb, seg2, segt2)
    return y, jnp.transpose(new, (3, 0, 1, 2))


def _row(v):
    return v.reshape(1, -1).astype(F32)


def _pad_cols(w, width):
    return jnp.pad(w, ((0, 0), (0, width - w.shape[1])))


def _pad_rows_to(w, rows):
    return jnp.pad(w, ((0, rows - w.shape[0]), (0, 0)))


def _blockdiag_slabs(w, scale=1.0):
    hd = MLSTM_HEAD_DIM
    rows = (w * scale).reshape(-1, hd, QKV_BLOCK)
    col = jnp.arange(hd)
    pick = (col[None, :] % QKV_BLOCK == jnp.arange(QKV_BLOCK)[:, None]).astype(w.dtype)
    tiled = jnp.einsum('srd,dc->src', rows, pick, precision=lax.Precision.HIGHEST)
    same_block = col[:, None] // QKV_BLOCK == col[None, :] // QKV_BLOCK
    return jnp.where(same_block, tiled, 0.0).astype(BF16)


def _head_indicator():
    ch = jnp.arange(RWKV_D) // RWKV_HEAD_DIM
    seg = (ch[:, None] == jnp.arange(LANES)[None, :]).astype(BF16)
    return seg, seg.T


def _mixer0_consts(p):
    ssd = [p['conv_w'][:, :XBC_DIM], p['conv_b'][:, :XBC_DIM], p['small_bias'], p['ssd_a_log'], p['ssd_d'],
           p['ssd_norm'], p['segt']]
    ml = [p['conv_w'][:, XBC_DIM:], p['conv_b'][:, XBC_DIM:], p['small_bias'], p['ml_wq'], p['ml_wk'],
          p['ml_wv'], p['ml_norm'], p['ml_skip']]
    return ssd, ml


def _mixer0_chunked(xbc, xm, z, o_pre, small, conv, ssm, mc, mn, mm, p, bsz, t):
    seq = lambda a: a.reshape(bsz, t, -1)
    tail = jnp.pad(conv, ((0, 0), (SUBLANES - (CONV_W - 1), 0), (0, 0)))
    ssd_c, ml_c = _mixer0_consts(p)
    y_ssd, new_ssm = _scan_call(
        _ssd_kernel, [seq(xbc), seq(z), seq(small)],
        [tail[:, :, :XBC_DIM], ssm.reshape(bsz, SSD_D_INNER, SSD_D_STATE)], ssd_c,
        [SSD_D_INNER], [(bsz, SSD_D_INNER, SSD_D_STATE)],
        [((SCAN_CHUNK + SUBLANES, XBC_DIM), F32), ((SSD_D_INNER, SSD_D_STATE), F32)])
    hm, new_c, new_n, new_m = _scan_call(
        _mlstm_kernel, [seq(xm), seq(o_pre), seq(small)],
        [tail[:, :, XBC_DIM:], mc.reshape(bsz, MLSTM_D_INNER, MLSTM_HEAD_DIM), mn.reshape(bsz, 1, MLSTM_D_INNER),
         _pad_cols(mm, LANES).reshape(bsz, 1, LANES)], ml_c,
        [MLSTM_D_INNER], [(bsz, MLSTM_D_INNER, MLSTM_HEAD_DIM), (bsz, 1, MLSTM_D_INNER), (bsz, 1, LANES)],
        [((SCAN_CHUNK + SUBLANES, MLSTM_D_INNER), F32), ((MLSTM_D_INNER, MLSTM_HEAD_DIM), F32),
         ((1, MLSTM_D_INNER), F32), ((1, LANES), F32)])
    flat = lambda a: a.reshape(bsz * t, -1)
    return flat(y_ssd), flat(hm), new_ssm, new_c, new_n, new_m.reshape(bsz, LANES)


def _mixer0_step(xbc, xm, z, o_pre, small, conv, ssm, mc, mn, mm, p, bsz):
    ssd_c, ml_c = _mixer0_consts(p)
    tails = [conv[:, j, :] for j in range(CONV_W - 1)]
    y_ssd, new_ssm = _step_call(
        _ssd_step_kernel, [xbc] + [tl[:, :XBC_DIM] for tl in tails] + [z, small],
        [ssm.reshape(bsz, SSD_D_INNER, SSD_D_STATE)], ssd_c, [SSD_D_INNER], [(bsz, SSD_D_INNER, SSD_D_STATE)])
    hm, new_n, new_m, new_c = _step_call(
        _mlstm_step_kernel, [xm] + [tl[:, XBC_DIM:] for tl in tails]
        + [o_pre, small, mn.reshape(bsz, MLSTM_D_INNER), _pad_cols(mm, LANES)],
        [mc.reshape(bsz, MLSTM_D_INNER, MLSTM_HEAD_DIM)], ml_c,
        [MLSTM_D_INNER, MLSTM_D_INNER, LANES], [(bsz, MLSTM_D_INNER, MLSTM_HEAD_DIM)])
    return y_ssd, hm, new_ssm, new_c, new_n, new_m


def _trunk(x, conv, ssm, mc, mn, mm, shift, wkv, p):
    bsz, t, d = x.shape
    m = bsz * t
    step = t == 1
    assert step or t % SCAN_CHUNK == 0, "a group is either single-token or a multiple of the scan chunk"
    assert not step or bsz % STEP_SEQS == 0
    act = F32 if step else BF16
    x2 = x.reshape(m, d)

    z, xbc, xm, o_pre, small = _rowwise_call(
        _in0_kernel, [x2], [p['norm_mix0'], p['w_in0']],
        [(SSD_D_INNER, act), (XBC_DIM, act), (MLSTM_D_INNER, act), (MLSTM_D_INNER, act), (LANES, F32)], m,
        tile=WIDE_ROW_TILE)
    last_rows = lambda a: a.reshape(bsz, t, -1)[:, -(CONV_W - 1):].astype(F32)
    conv_in_tail = jnp.concatenate([last_rows(xbc), last_rows(xm)], axis=-1)
    new_conv = jnp.concatenate([conv, conv_in_tail], axis=1)[:, -(CONV_W - 1):]
    if step:
        y_ssd, hm, new_ssm, new_c, new_n, new_m = _mixer0_step(xbc, xm, z, o_pre, small, conv, ssm, mc, mn, mm, p, bsz)
    else:
        y_ssd, hm, new_ssm, new_c, new_n, new_m = _mixer0_chunked(xbc, xm, z, o_pre, small, conv, ssm, mc, mn, mm,
                                                                  p, bsz, t)
    xn1, x2b = _rowwise_call(
        _mix0_ffn_kernel, [y_ssd, hm, x2],
        [p['w_out0'], p['norm_ffn0'], (p['ffn_gu'], 0), (p['ffn_d'], 0), p['norm_mix1']], [(d, act), (d, F32)], m,
        tile=WIDE_ROW_TILE)

    rw_in_c = [p['rw_mu'], p['rw_wr'], p['rw_wk'], p['rw_wv'], p['rw_w1'], p['rw_w2'], p['rw_a1'], p['rw_a2'],
               p['rw_g1'], p['rw_g2'], p['rw_w0'], p['rw_a0'], p['rw_k_k'], p['rw_k_a'], p['seg'], p['segt']]
    rw_outs = [(d, act), (d, F32)] + [(d, act)] * 5
    if step:
        rw = _rowwise_call(_rw_in_kernel, [xn1, shift], rw_in_c, rw_outs, m)
    else:
        tm = _row_tile(t, ROW_TILE)
        tiles_per_seq = t // tm
        prev_spec = pl.BlockSpec((PREV_ROWS, d), lambda i: (jnp.maximum(i * (tm // PREV_ROWS) - 1, 0), 0))
        shift_spec = pl.BlockSpec((None, 1, d), lambda i: (i // tiles_per_seq, 0, 0))
        rw = _rowwise_call(functools.partial(_rw_in_shift_kernel, tiles_per_seq=tiles_per_seq), [xn1], rw_in_c,
                           rw_outs, m, tile=tm, extra_ins=[(xn1, prev_spec), (shift.reshape(bsz, 1, d), shift_spec)])
    g = rw[6]
    rw_c = [p['rw_r_k'], p['rw_ln_w'], p['rw_ln_b'], p['seg'], p['segt']]
    if step:
        y_rw, new_wkv = _rwkv_step_call(list(rw[:6]), wkv, *rw_c)
    else:
        wkv3 = wkv.reshape(bsz, RWKV_D, RWKV_HEAD_DIM)
        y_rw, new_wkv = _scan_call(
            _rwkv_kernel, [a.reshape(bsz, t, d) for a in rw[:6]], [wkv3], rw_c,
            [d], [wkv3.shape], [((RWKV_D, 2 * RWKV_HEAD_DIM), F32)])
        y_rw = y_rw.reshape(m, d)
    new_wkv = new_wkv.reshape(wkv.shape)
    (y,) = _rowwise_call(
        _mix1_ffn_kernel, [y_rw, g, x2b],
        [p['rw_wo'], p['norm_ffn1'], (p['ffn_gu'], 1), (p['ffn_d'], 1), p['norm_final']], [(d, F32)], m, tile=WIDE_ROW_TILE)

    states = (new_conv, new_ssm.reshape(ssm.shape), new_c.reshape(mc.shape), new_n.reshape(mn.shape),
              new_m[:, :MLSTM_HEADS], xn1.reshape(bsz, t, d)[:, -1].astype(F32), new_wkv)
    return y.reshape(bsz, t, d), states


def kernel(x_prompt, x_sample, state_conv, state_ssm, state_mlstm_c, state_mlstm_n, state_mlstm_m, state_shift, state_wkv, norm_mix, norm_ffn, norm_final, w_in0, conv_w, conv_b, ssd_dt_bias, ssd_a_log, ssd_d, ssd_norm, ml_wq, ml_wk, ml_wv, ml_i_bias, ml_f_bias, ml_norm, ml_skip, w_out0, rw_mu, rw_wr, rw_wk, rw_wv, rw_wo, rw_w0, rw_w1, rw_w2, rw_a0, rw_a1, rw_a2, rw_g1, rw_g2, rw_k_k, rw_k_a, rw_r_k, rw_ln_w, rw_ln_b, ffn_w_gate_up, ffn_w_down):
    assert norm_mix.shape[0] == 2 and w_in0.shape[0] == 1 and rw_wr.shape[0] == 1, "two layers: SSD|mLSTM then RWKV-7"
    s1 = SSD_D_INNER
    s2 = s1 + XBC_DIM + MLSTM_D_INNER
    s3 = s2 + SSD_HEADS
    s4 = s3 + MLSTM_D_INNER
    s5 = s4 + MLSTM_HEADS
    w0 = w_in0[0]
    small_w = _pad_cols(jnp.concatenate([w0[:, s2:s3], w0[:, s4:s5], w0[:, s5:]], axis=1), LANES)
    seg, segt = _head_indicator()
    lora = lambda w1, w2, width: (_pad_cols(w1, width).astype(BF16), _pad_rows_to(w2, width).astype(BF16))
    rw_w1p, rw_w2p = lora(rw_w1[0], rw_w2[0], LANES)
    rw_a1p, rw_a2p = lora(rw_a1[0], rw_a2[0], LANES)
    rw_g1p, rw_g2p = lora(rw_g1[0], rw_g2[0], 2 * LANES)
    p = dict(
        norm_mix0=_row(norm_mix[0]), norm_mix1=_row(norm_mix[1]), norm_ffn0=_row(norm_ffn[0]),
        norm_ffn1=_row(norm_ffn[1]), norm_final=_row(norm_final),
        w_in0=jnp.concatenate([w0[:, :s2], w0[:, s3:s4], small_w], axis=1).astype(BF16),
        conv_w=conv_w[0], conv_b=_row(conv_b[0]),
        small_bias=_pad_cols(jnp.concatenate([_row(ssd_dt_bias[0]), _row(ml_i_bias[0]), _row(ml_f_bias[0])], axis=1),
                             LANES),
        ssd_a_log=_pad_cols(_row(ssd_a_log[0]), LANES), ssd_d=_row(jnp.repeat(ssd_d[0], SSD_HEAD_DIM)),
        ssd_norm=_row(ssd_norm[0]),
        ml_wq=_blockdiag_slabs(ml_wq[0]), ml_wk=_blockdiag_slabs(ml_wk[0], MLSTM_HEAD_DIM ** -0.5),
        ml_wv=_blockdiag_slabs(ml_wv[0]), ml_norm=_row(ml_norm[0]), ml_skip=_row(ml_skip[0]),
        w_out0=w_out0[0].astype(BF16),
        rw_mu=_pad_rows_to(rw_mu[0], SUBLANES),
        rw_wr=rw_wr[0].astype(BF16), rw_wk=rw_wk[0].astype(BF16), rw_wv=rw_wv[0].astype(BF16),
        rw_wo=rw_wo[0].astype(BF16), rw_w0=_row(rw_w0[0]), rw_a0=_row(rw_a0[0]),
        rw_w1=rw_w1p, rw_w2=rw_w2p, rw_a1=rw_a1p, rw_a2=rw_a2p, rw_g1=rw_g1p, rw_g2=rw_g2p,
        rw_k_k=_row(rw_k_k[0]), rw_k_a=_row(rw_k_a[0]), rw_r_k=_row(rw_r_k[0]),
        rw_ln_w=_row(rw_ln_w[0]), rw_ln_b=_row(rw_ln_b[0]),
        ffn_gu=ffn_w_gate_up.astype(BF16), ffn_d=ffn_w_down.astype(BF16),
        seg=seg, segt=segt,
    )
    bp = x_prompt.shape[0]
    zeros = lambda s: jnp.zeros((bp,) + s.shape[2:], F32)
    y_p, st_p = _trunk(x_prompt, zeros(state_conv), zeros(state_ssm), zeros(state_mlstm_c),
                       zeros(state_mlstm_n), zeros(state_mlstm_m), zeros(state_shift), zeros(state_wkv), p)
    y_s, st_s = _trunk(x_sample, state_conv[0], state_ssm[0], state_mlstm_c[0], state_mlstm_n[0],
                       state_mlstm_m[0], state_shift[0], state_wkv[0], p)
    out = [y_p, y_s]
    for a, b in zip(st_p, st_s):
        out += [a[None], b[None]]
    return tuple(out)
```

```python
import functools

import jax
import jax.numpy as jnp
from jax import lax
from jax.experimental import pallas as pl
from jax.experimental.pallas import tpu as pltpu

F32 = jnp.float32
BF16 = jnp.bfloat16

D_MODEL = 1024
NORM_EPS = 1e-5
CONV_W = 4
SSD_HEADS = 16
SSD_HEAD_DIM = 64
SSD_GROUPS = 2
SSD_D_STATE = 128
SSD_D_INNER = SSD_HEADS * SSD_HEAD_DIM
SSD_GROUP_W = SSD_D_INNER // SSD_GROUPS
XBC_DIM = SSD_D_INNER + 2 * SSD_GROUPS * SSD_D_STATE
MLSTM_HEADS = 4
MLSTM_HEAD_DIM = 256
MLSTM_D_INNER = MLSTM_HEADS * MLSTM_HEAD_DIM
QKV_BLOCK = 4
RWKV_HEADS = 16
RWKV_HEAD_DIM = 64
RWKV_D = RWKV_HEADS * RWKV_HEAD_DIM
RWKV_LN_EPS = 64e-5
D_FF = 2816
FF_CHUNK = 256
LANES = 128
SUBLANES = 8
SMALL_DT, SMALL_I, SMALL_F = 0, 16, 20
NEG = -1e30
EXP_NEG_HALF = 0.6065306597126334
SCAN_CHUNK = 128
STEP_SEQS = SUBLANES
ROW_TILE = 512
PREV_ROWS = 16
VMEM_LIMIT = 56 * 1024 * 1024


def _dot(a, b):
    return jnp.dot(a.astype(BF16), b.astype(BF16), preferred_element_type=F32)


def _dot_nt(a, b):
    return lax.dot_general(a.astype(BF16), b.astype(BF16), (((1,), (1,)), ((), ())),
                           preferred_element_type=F32)


def _dot_tn(a, b):
    return lax.dot_general(a.astype(BF16), b.astype(BF16), (((0,), (0,)), ((), ())),
                           preferred_element_type=F32)


def _split(x, terms):
    parts = []
    rem = x
    for _ in range(terms):
        p = rem.astype(BF16)
        parts.append(p)
        rem = rem - p.astype(F32)
    return parts


def _dot_exact_rhs(x, m, terms=2):
    out = None
    for p in _split(x, terms):
        t = jnp.dot(p, m, preferred_element_type=F32)
        out = t if out is None else out + t
    return out


def _tri_cumsum(tri, x, terms=3):
    out = None
    for p in _split(x, terms):
        t = jnp.dot(tri, p, preferred_element_type=F32)
        out = t if out is None else out + t
    return out


def _rms_rows(x, g):
    return x * lax.rsqrt(jnp.mean(x * x, axis=-1, keepdims=True) + NORM_EPS) * g


def _group_rms(x, width):
    outs = []
    for lo in range(0, x.shape[1], width):
        grp = x[:, lo:lo + width]
        outs.append(grp * lax.rsqrt(jnp.mean(grp * grp, axis=-1, keepdims=True) + NORM_EPS))
    return jnp.concatenate(outs, axis=1)


def _sigmoid(x):
    return 0.5 * jnp.tanh(0.5 * x) + 0.5


def _silu(x):
    h = 0.5 * x
    return h * jnp.tanh(h) + h


def _softplus(x):
    return jnp.maximum(x, 0.0) + jnp.log(1.0 + jnp.exp(-jnp.abs(x)))


def _log_sigmoid(x):
    return jnp.minimum(x, 0.0) - jnp.log(1.0 + jnp.exp(-jnp.abs(x)))


def _tri_masks(n):
    row = lax.broadcasted_iota(jnp.int32, (n, n), 0)
    col = lax.broadcasted_iota(jnp.int32, (n, n), 1)
    return row >= col, row > col


def _chunk_conv(ext, u, conv0_ref, cw_ref, cb_ref, first):
    rows = u.shape[0]

    @pl.when(first)
    def _():
        ext[0:SUBLANES, :] = conv0_ref[...]

    @pl.when(jnp.logical_not(first))
    def _():
        ext[0:SUBLANES, :] = ext[rows:rows + SUBLANES, :]

    ext[SUBLANES:SUBLANES + rows, :] = u
    e = ext[...]
    conv = cb_ref[...] + u * cw_ref[CONV_W - 1:CONV_W, :]
    for j in range(CONV_W - 1):
        back = pltpu.roll(e, shift=CONV_W - 1 - j, axis=0)[SUBLANES:SUBLANES + rows, :]
        conv = conv + back * cw_ref[j:j + 1, :]
    return conv


def _step_conv(u_ref, tail_refs, cw_ref, cb_ref):
    conv = cb_ref[...] + u_ref[...] * cw_ref[CONV_W - 1:CONV_W, :]
    for j, t in enumerate(tail_refs):
        conv = conv + t[...] * cw_ref[j:j + 1, :]
    return conv


def _blockdiag_qkv(act, raw, wq_ref, wk_ref, wv_ref):
    hd = MLSTM_HEAD_DIM
    cat = lambda x, w: jnp.concatenate(
        [_dot(x[:, s * hd:(s + 1) * hd], w[s]) for s in range(MLSTM_HEADS)], axis=1)
    return cat(act, wq_ref), cat(act, wk_ref), cat(raw, wv_ref)


def _head_sum(x):
    half = lax.broadcasted_iota(jnp.int32, (x.shape[0], LANES), 1) < RWKV_HEAD_DIM
    outs = []
    for lo in range(0, x.shape[1], LANES):
        xp = x[:, lo:lo + LANES]
        s0 = jnp.sum(jnp.where(half, xp, 0.0), axis=1, keepdims=True)
        s1 = jnp.sum(jnp.where(half, 0.0, xp), axis=1, keepdims=True)
        outs.append(jnp.where(half, s0, s1))
    return jnp.concatenate(outs, axis=1)


def _rwkv_head_norm(y, r, k, v, rk_ref, lnw_ref, lnb_ref):
    inv_n = 1.0 / RWKV_HEAD_DIM
    dlt = y - _head_sum(y) * inv_n
    yn = dlt * lax.rsqrt(_head_sum(dlt * dlt) * inv_n + RWKV_LN_EPS)
    return yn * lnw_ref[...] + lnb_ref[...] + _head_sum(r * k * rk_ref[...]) * v


def _in0_kernel(x_ref, g_ref, w_ref, z_ref, xbc_ref, xm_ref, o_ref, sm_ref):
    xn = _rms_rows(x_ref[...], g_ref[...]).astype(BF16)
    off = 0
    for ref in (z_ref, xbc_ref, xm_ref, o_ref, sm_ref):
        n = ref.shape[-1]
        ref[...] = jnp.dot(xn, w_ref[:, off:off + n], preferred_element_type=F32).astype(ref.dtype)
        off += n


def _swiglu_residual(x, g_ref, wgu_ref, wd_ref):
    xn = _rms_rows(x, g_ref[...]).astype(BF16)
    acc = None
    for c in range(D_FF // FF_CHUNK):
        lo = c * FF_CHUNK
        gate = jnp.dot(xn, wgu_ref[:, lo:lo + FF_CHUNK], preferred_element_type=F32)
        up = jnp.dot(xn, wgu_ref[:, D_FF + lo:D_FF + lo + FF_CHUNK], preferred_element_type=F32)
        part = _dot(_silu(gate) * up, wd_ref[lo:lo + FF_CHUNK, :])
        acc = part if acc is None else acc + part
    return x + acc


def _mix0_ffn_kernel(y_ref, h_ref, x_ref, wo_ref, g_ref, wgu_ref, wd_ref, pg_ref, xn_ref, xo_ref):
    mixed = jnp.concatenate([y_ref[...].astype(BF16), h_ref[...].astype(BF16)], axis=1)
    x1 = x_ref[...] + jnp.dot(mixed, wo_ref[...], preferred_element_type=F32)
    x2 = _swiglu_residual(x1, g_ref, wgu_ref, wd_ref)
    xo_ref[...] = x2
    xn_ref[...] = _rms_rows(x2, pg_ref[...]).astype(xn_ref.dtype)


def _mix1_ffn_kernel(y_ref, gate_ref, x_ref, wo_ref, g_ref, wgu_ref, wd_ref, pg_ref, o_ref):
    x3 = x_ref[...] + _dot(y_ref[...].astype(F32) * gate_ref[...].astype(F32), wo_ref[...])
    o_ref[...] = _rms_rows(_swiglu_residual(x3, g_ref, wgu_ref, wd_ref), pg_ref[...])


def _rw_in_body(xn, xp, mu_ref, wr_ref, wk_ref, wv_ref, w1_ref, w2_ref, a1_ref, a2_ref,
                g1_ref, g2_ref, w0_ref, a0_ref, kk_w_ref, ka_w_ref,
                r_ref, lw_ref, k_ref, v_ref, kk_ref, a_ref, g_ref):
    xx = xp - xn
    xr, xw, xk, xv, xa, xg = (xn + xx * mu_ref[c:c + 1, :] for c in range(6))
    r_ref[...] = _dot(xr, wr_ref[...]).astype(r_ref.dtype)
    k = _dot(xk, wk_ref[...])
    v_ref[...] = _dot(xv, wv_ref[...]).astype(v_ref.dtype)
    z = w0_ref[...] + _dot(jnp.tanh(_dot(xw, w1_ref[...])), w2_ref[...])
    lw_ref[...] = -EXP_NEG_HALF * _sigmoid(z)
    a = _sigmoid(a0_ref[...] + _dot(_dot(xa, a1_ref[...]), a2_ref[...]))
    a_ref[...] = a.astype(a_ref.dtype)
    g_ref[...] = _dot(_sigmoid(_dot(xg, g1_ref[...])), g2_ref[...]).astype(g_ref.dtype)
    kk = k * kk_w_ref[...]
    ss = _head_sum(kk * kk)
    kk_ref[...] = (kk * lax.rsqrt(jnp.maximum(ss, 1e-24))).astype(kk_ref.dtype)
    k_ref[...] = (k * (1.0 + (a - 1.0) * ka_w_ref[...])).astype(k_ref.dtype)


def _rw_in_kernel(xn_ref, xp_ref, *refs):
    _rw_in_body(xn_ref[...].astype(F32), xp_ref[...].astype(F32), *refs)


def _rw_in_shift_kernel(xn_ref, prev_ref, shift_ref, *refs, tiles_per_seq):
    xn = xn_ref[...].astype(F32)
    starts_seq = pl.program_id(0) % tiles_per_seq == 0
    last = prev_ref.shape[0] - 1
    prev = jnp.where(starts_seq, shift_ref[...], prev_ref[last:last + 1, :].astype(F32))
    row = lax.broadcasted_iota(jnp.int32, xn.shape, 0)
    xp = jnp.where(row == 0, prev, pltpu.roll(xn, shift=1, axis=0))
    _rw_in_body(xn, xp, *refs)


def _ssd_kernel(xbc_ref, z_ref, sm_ref, conv0_ref, st0_ref, cw_ref, cb_ref, bias_ref, alog_ref,
                dexp_ref, nw_ref, segt_ref, y_ref, st_ref, ext, state):
    c = pl.program_id(1)
    first = c == 0
    rows = xbc_ref.shape[0]

    @pl.when(first)
    def _():
        state[...] = st0_ref[...]

    act = _silu(_chunk_conv(ext, xbc_ref[...].astype(F32), conv0_ref, cw_ref, cb_ref, first))
    xs = act[:, 0:SSD_D_INNER]
    gn = SSD_GROUPS * SSD_D_STATE
    bm = act[:, SSD_D_INNER:SSD_D_INNER + gn]
    cm = act[:, SSD_D_INNER + gn:SSD_D_INNER + 2 * gn]

    tril, _ = _tri_masks(rows)
    tri = jnp.where(tril, 1.0, 0.0).astype(BF16)
    lane = lax.broadcasted_iota(jnp.int32, (rows, LANES), 1)
    head_lane = (lane >= SMALL_DT) & (lane < SMALL_DT + SSD_HEADS)
    dt = jnp.where(head_lane, _softplus(sm_ref[...] + bias_ref[...]), 0.0)
    acum = _tri_cumsum(tri, dt * (-jnp.exp(alog_ref[...])))
    acum_t = acum.T
    dt_t = dt.T
    segt = segt_ref[...]
    eac_x = _dot_exact_rhs(jnp.exp(acum), segt, terms=1)
    wend_x = _dot_exact_rhs(jnp.exp(acum[rows - 1:rows, :] - acum) * dt, segt, terms=1)
    xw = xs * wend_x

    half = lane < SSD_HEAD_DIM
    heads_per_group = SSD_HEADS // SSD_GROUPS
    grp = lambda x, g: x[:, g * SSD_D_STATE:(g + 1) * SSD_D_STATE]
    cbs = [_dot_nt(grp(cm, g), grp(bm, g)) for g in range(SSD_GROUPS)]
    decays = [jnp.exp(jnp.where(tril, acum[:, j:j + 1] - acum_t[j:j + 1, :], NEG)) for j in range(SSD_HEADS)]
    wts = [decays[j] * cbs[j // heads_per_group] * dt_t[j:j + 1, :] for j in range(SSD_HEADS)]
    lhs = [jnp.concatenate([wts[2 * p], wts[2 * p + 1]], axis=1).astype(BF16) for p in range(SSD_HEADS // 2)]
    rhs = []
    for p in range(SSD_HEADS // 2):
        xp = xs[:, p * LANES:(p + 1) * LANES]
        rhs.append(jnp.concatenate([jnp.where(half, xp, 0.0), jnp.where(half, 0.0, xp)], axis=0).astype(BF16))
    pairs = [_dot(l, r) for l, r in zip(lhs, rhs)]
    y = jnp.concatenate(pairs, axis=1)

    st = state[...]
    gw = SSD_GROUP_W
    y_state = jnp.concatenate(
        [_dot_nt(cm[:, g * SSD_D_STATE:(g + 1) * SSD_D_STATE], st[g * gw:(g + 1) * gw, :])
         for g in range(SSD_GROUPS)], axis=1)
    y = y + eac_x * y_state + dexp_ref[...] * xs
    y_ref[...] = (_group_rms(y * _silu(z_ref[...].astype(F32)), gw) * nw_ref[...]).astype(y_ref.dtype)

    dcol = jnp.exp(acum_t[:, rows - 1:rows])
    for g in range(SSD_GROUPS):
        upd = _dot_tn(xw[:, g * gw:(g + 1) * gw], bm[:, g * SSD_D_STATE:(g + 1) * SSD_D_STATE])
        for h in range(heads_per_group):
            j = g * heads_per_group + h
            lo = j * SSD_HEAD_DIM
            state[lo:lo + SSD_HEAD_DIM, :] = (st[lo:lo + SSD_HEAD_DIM, :] * dcol[j:j + 1, :]
                                              + upd[h * SSD_HEAD_DIM:(h + 1) * SSD_HEAD_DIM, :])

    @pl.when(c == pl.num_programs(1) - 1)
    def _():
        st_ref[...] = state[...]


def _mlstm_kernel(xm_ref, o_ref, sm_ref, conv0_ref, c0_ref, n0_ref, m0_ref, cw_ref, cb_ref, bias_ref,
                  wq_ref, wk_ref, wv_ref, nw_ref, skip_ref, h_ref, c_ref, n_ref, m_ref,
                  ext, cst, nst, mst):
    c = pl.program_id(1)
    first = c == 0
    rows = xm_ref.shape[0]

    @pl.when(first)
    def _():
        cst[...] = c0_ref[...]
        nst[...] = n0_ref[...]
        mst[...] = m0_ref[...]

    raw = xm_ref[...].astype(F32)
    act = _silu(_chunk_conv(ext, raw, conv0_ref, cw_ref, cb_ref, first))
    q, k, v = _blockdiag_qkv(act, raw, wq_ref, wk_ref, wv_ref)

    tril, _ = _tri_masks(rows)
    tri = jnp.where(tril, 1.0, 0.0).astype(BF16)
    logi = sm_ref[...] + bias_ref[...]
    bcum = _tri_cumsum(tri, _log_sigmoid(logi))
    bcum_t = bcum.T
    logi_t = logi.T
    m_all = mst[...]
    lane1 = lax.broadcasted_iota(jnp.int32, (1, LANES), 1)
    hd = MLSTM_HEAD_DIM

    heads = range(MLSTM_HEADS)
    cols = [slice(h * hd, (h + 1) * hd) for h in heads]
    qs, ks, vs = ([t[:, c] for c in cols] for t in (q, k, v))
    qbs, vbs = [x.astype(BF16) for x in qs], [x.astype(BF16) for x in vs]
    bcs = [bcum[:, SMALL_F + h:SMALL_F + h + 1] for h in heads]
    lis = [logi[:, SMALL_I + h:SMALL_I + h + 1] for h in heads]
    m_hs = [m_all[:, h:h + 1] for h in heads]
    c_hs = [cst[c, :] for c in cols]
    n_hs = [nst[:, c] for c in cols]
    dlogs = [jnp.where(tril, bcs[h] - bcum_t[SMALL_F + h:SMALL_F + h + 1, :]
                       + logi_t[SMALL_I + h:SMALL_I + h + 1, :], NEG) for h in heads]
    inters = [bcs[h] + m_hs[h] for h in heads]
    m_ts = [jnp.maximum(inters[h], jnp.max(dlogs[h], axis=1, keepdims=True)) for h in heads]
    qks = [_dot_nt(qbs[h], ks[h]) for h in heads]
    qcs = [_dot(qbs[h], c_hs[h]) for h in heads]
    ss = [qks[h] * jnp.exp(dlogs[h] - m_ts[h]) for h in heads]
    w_inters = [jnp.exp(inters[h] - m_ts[h]) for h in heads]
    nums = [_dot(ss[h], vbs[h]) + w_inters[h] * qcs[h] for h in heads]
    dens = [jnp.sum(ss[h], axis=1, keepdims=True)
            + w_inters[h] * jnp.sum(qs[h] * n_hs[h], axis=1, keepdims=True) for h in heads]
    hs = [nums[h] * (1.0 / jnp.maximum(jnp.abs(dens[h]), jnp.exp(-m_ts[h]))) for h in heads]
    b_ends = [bcs[h][rows - 1:rows, :] for h in heads]
    wlogs = [b_ends[h] - bcs[h] + lis[h] for h in heads]
    m_news = [jnp.maximum(b_ends[h] + m_hs[h], jnp.max(wlogs[h], axis=0, keepdims=True)) for h in heads]
    dcs = [jnp.exp(b_ends[h] + m_hs[h] - m_news[h]) for h in heads]
    kwss = [ks[h] * jnp.exp(wlogs[h] - m_news[h]) for h in heads]
    upds = [_dot_tn(kwss[h], vbs[h]) for h in heads]
    for h in heads:
        cst[cols[h], :] = dcs[h] * c_hs[h] + upds[h]
        nst[:, cols[h]] = dcs[h] * n_hs[h] + jnp.sum(kwss[h], axis=0, keepdims=True)
        m_all = jnp.where(lane1 == h, m_news[h], m_all)
    mst[...] = m_all

    hm = _group_rms(jnp.concatenate(hs, axis=1), hd) * nw_ref[...]
    h_ref[...] = ((hm + skip_ref[...] * act) * _sigmoid(o_ref[...].astype(F32))).astype(h_ref.dtype)

    @pl.when(c == pl.num_programs(1) - 1)
    def _():
        c_ref[...] = cst[...]
        n_ref[...] = nst[...]
        m_ref[...] = mst[...]


def _rwkv_kernel(r_ref, lw_ref, k_ref, v_ref, kk_ref, a_ref, s0_ref, rk_ref, lnw_ref, lnb_ref,
                 y_ref, s_ref, sblk):
    c = pl.program_id(1)
    rows = r_ref.shape[0]

    @pl.when(c == 0)
    def _():
        s0 = s0_ref[...]
        shape = (RWKV_D, 2 * RWKV_HEAD_DIM)
        head_parity = (lax.broadcasted_iota(jnp.int32, shape, 0) // RWKV_HEAD_DIM) % 2
        lane_half = lax.broadcasted_iota(jnp.int32, shape, 1) // RWKV_HEAD_DIM
        sblk[...] = jnp.where(head_parity == lane_half, jnp.concatenate([s0, s0], axis=1), 0.0)

    r, lw, k, v, kk, a = (t[...].astype(F32) for t in (r_ref, lw_ref, k_ref, v_ref, kk_ref, a_ref))
    tril, strict = _tri_masks(rows)
    tri = jnp.where(tril, 1.0, 0.0).astype(BF16)
    cum = _tri_cumsum(tri, lw, terms=2)
    last = cum[rows - 1:rows, :]
    einv = jnp.exp(-cum)
    eend = jnp.exp(last - cum)
    ka = kk * a
    at = -(kk * jnp.exp(cum - lw))
    rt = r * jnp.exp(cum)
    bt = ka * einv
    kt = k * einv
    bh = ka * eend
    kh = k * eend
    wl = jnp.exp(last)
    half = lax.broadcasted_iota(jnp.int32, (rows, LANES), 1) < RWKV_HEAD_DIM
    steps = rows.bit_length() - 1

    def two(x):
        return [jnp.where(half, x, 0.0), jnp.where(half, 0.0, x)]

    npairs = RWKV_HEADS // 2
    tril4 = jnp.concatenate([tril] * 4, axis=1)
    strict4 = jnp.concatenate([strict] * 4, axis=1)
    lanes = [slice(p * LANES, (p + 1) * LANES) for p in range(npairs)]

    a2s = [jnp.concatenate([at[:, sl], rt[:, sl]], axis=0).astype(BF16) for sl in lanes]
    rbs = [jnp.concatenate(two(bt[:, sl]) + two(kt[:, sl]), axis=0).astype(BF16) for sl in lanes]
    sps = [sblk[sl, :] for sl in lanes]
    vss = [two(v[:, sl]) for sl in lanes]
    pbks = [_dot_nt(a2, rb) for a2, rb in zip(a2s, rbs)]
    xss = [_dot_nt(a2, sp) for a2, sp in zip(a2s, sps)]
    tops = [jnp.where(strict4, pbk[0:rows, :], 0.0) for pbk in pbks]
    us = [xs[0:rows, :] + _dot(top[:, 2 * rows:4 * rows], jnp.concatenate(vs, axis=0))
          for xs, top, vs in zip(xss, tops, vss)]
    pws = [[top[:, e * rows:(e + 1) * rows].astype(BF16) for e in range(2)] for top in tops]
    ybase = [xs[rows:2 * rows, :] for xs in xss]
    w4s = [jnp.where(tril4, pbk[rows:2 * rows, :], 0.0).astype(BF16) for pbk in pbks]

    for j in range(steps):
        for p in range(npairs):
            u2 = jnp.concatenate(two(us[p]), axis=0)
            us[p] = us[p] + _dot(jnp.concatenate(pws[p], axis=1), u2)
        if j < steps - 1:
            for p in range(npairs):
                pws[p] = [_dot(pw, pw).astype(BF16) for pw in pws[p]]

    ys = []
    for p, sl in enumerate(lanes):
        r4 = jnp.concatenate(two(us[p]) + vss[p], axis=0).astype(BF16)
        ys.append(ybase[p] + _dot(w4s[p], r4))
        rh = jnp.concatenate(two(bh[:, sl]) + two(kh[:, sl]), axis=0)
        sblk[sl, :] = sps[p] * wl[:, sl] + _dot_tn(r4, rh)
    y = jnp.concatenate(ys, axis=1)
    y_ref[...] = _rwkv_head_norm(y, r, k, v, rk_ref, lnw_ref, lnb_ref).astype(y_ref.dtype)

    @pl.when(c == pl.num_programs(1) - 1)
    def _():
        sb = sblk[...]
        s_ref[...] = sb[:, 0:RWKV_HEAD_DIM] + sb[:, RWKV_HEAD_DIM:2 * RWKV_HEAD_DIM]


def _ssd_step_kernel(xbc_ref, t0_ref, t1_ref, t2_ref, z_ref, sm_ref, st_ref, cw_ref, cb_ref, bias_ref,
                     alog_ref, dexp_ref, nw_ref, segt_ref, y_ref, sto_ref):
    nseq = xbc_ref.shape[0]
    act = _silu(_step_conv(xbc_ref, (t0_ref, t1_ref, t2_ref), cw_ref, cb_ref))
    xs = act[:, 0:SSD_D_INNER]
    gn = SSD_GROUPS * SSD_D_STATE
    bm = act[:, SSD_D_INNER:SSD_D_INNER + gn]
    cm = act[:, SSD_D_INNER + gn:SSD_D_INNER + 2 * gn]
    lane = lax.broadcasted_iota(jnp.int32, (nseq, LANES), 1)
    head_lane = (lane >= SMALL_DT) & (lane < SMALL_DT + SSD_HEADS)
    dt = jnp.where(head_lane, _softplus(sm_ref[...] + bias_ref[...]), 0.0)
    segt = segt_ref[...]
    dec_t = _dot_exact_rhs(jnp.exp(dt * (-jnp.exp(alog_ref[...]))), segt).T
    xdt_t = (xs * _dot_exact_rhs(dt, segt)).T
    gw = SSD_GROUP_W
    rowi = lax.broadcasted_iota(jnp.int32, (nseq, gw), 0)
    ys = [jnp.zeros((nseq, gw), F32) for _ in range(SSD_GROUPS)]
    for s in range(nseq):
        for g in range(SSD_GROUPS):
            rs = slice(g * gw, (g + 1) * gw)
            ns = slice(g * SSD_D_STATE, (g + 1) * SSD_D_STATE)
            new = st_ref[s, rs, :] * dec_t[rs, s:s + 1] + xdt_t[rs, s:s + 1] * bm[s:s + 1, ns]
            sto_ref[s, rs, :] = new
            ys[g] = jnp.where(rowi == s, _dot_nt(cm[:, ns], new), ys[g])
    y = jnp.concatenate(ys, axis=1) + dexp_ref[...] * xs
    y_ref[...] = _group_rms(y * _silu(z_ref[...]), gw) * nw_ref[...]


def _mlstm_step_kernel(xm_ref, t0_ref, t1_ref, t2_ref, o_ref, sm_ref, n_ref, m_ref, c_ref, cw_ref, cb_ref,
                       bias_ref, wq_ref, wk_ref, wv_ref, nw_ref, skip_ref, h_ref, no_ref, mo_ref, co_ref):
    nseq = xm_ref.shape[0]
    raw = xm_ref[...]
    act = _silu(_step_conv(xm_ref, (t0_ref, t1_ref, t2_ref), cw_ref, cb_ref))
    q, k, v = _blockdiag_qkv(act, raw, wq_ref, wk_ref, wv_ref)
    logi = sm_ref[...] + bias_ref[...]
    logf = _log_sigmoid(logi)
    m_old = m_ref[...]
    m_all = m_old
    n_old = n_ref[...]
    lane = lax.broadcasted_iota(jnp.int32, (nseq, LANES), 1)
    hd = MLSTM_HEAD_DIM
    rowi = lax.broadcasted_iota(jnp.int32, (nseq, hd), 0)
    hs, ns = [], []
    for h in range(MLSTM_HEADS):
        cols = slice(h * hd, (h + 1) * hd)
        qh, kh, vh = q[:, cols], k[:, cols], v[:, cols]
        lf = logf[:, SMALL_F + h:SMALL_F + h + 1] + m_old[:, h:h + 1]
        li = logi[:, SMALL_I + h:SMALL_I + h + 1]
        m_new = jnp.maximum(lf, li)
        dc = jnp.exp(lf - m_new)
        kws = kh * jnp.exp(li - m_new)
        n_new = dc * n_old[:, cols] + kws
        den = jnp.sum(qh * n_new, axis=1, keepdims=True)
        kws_t = kws.T
        q_t = qh.T
        num = jnp.zeros((nseq, hd), F32)
        for s in range(nseq):
            c_new = dc[s:s + 1, :] * c_ref[s, cols, :] + kws_t[:, s:s + 1] * vh[s:s + 1, :]
            co_ref[s, cols, :] = c_new
            num = jnp.where(rowi == s, jnp.sum(q_t[:, s:s + 1] * c_new, axis=0, keepdims=True), num)
        hs.append(num / jnp.maximum(jnp.abs(den), jnp.exp(-m_new)))
        ns.append(n_new)
        m_all = jnp.where(lane == h, m_new, m_all)
    no_ref[...] = jnp.concatenate(ns, axis=1)
    mo_ref[...] = m_all
    hm = _group_rms(jnp.concatenate(hs, axis=1), hd) * nw_ref[...]
    h_ref[...] = (hm + skip_ref[...] * act) * _sigmoid(o_ref[...])


def _rwkv_step_kernel(r_ref, lw_ref, k_ref, v_ref, kk_ref, a_ref, s_ref, rk_ref, lnw_ref, lnb_ref,
                      y_ref, so_ref, yt):
    r, k, v, kk = (t[...] for t in (r_ref, k_ref, v_ref, kk_ref))
    n = RWKV_HEAD_DIM
    w_t, kk_t, ka_t, k_t, r_t, v_t = (x.T for x in (jnp.exp(lw_ref[...]), kk, kk * a_ref[...], k, r, v))
    for e in range(2):
        hs = slice(e * n, (e + 1) * n)
        w_h, kk_h, ka_h, k_h, r_h = w_t[hs], kk_t[hs], ka_t[hs], k_t[hs], r_t[hs]
        for i in range(n):
            row = e * n + i
            st = s_ref[e, i]
            sa = jnp.sum(st * kk_h, axis=0, keepdims=True)
            new = st * w_h - sa * ka_h + v_t[row:row + 1, :] * k_h
            so_ref[e, i] = new
            yt[row:row + 1, :] = jnp.sum(new * r_h, axis=0, keepdims=True)
    y_ref[...] = _rwkv_head_norm(yt[...].T, r, k, v, rk_ref, lnw_ref, lnb_ref)


def _params(sem):
    return pltpu.CompilerParams(dimension_semantics=sem, vmem_limit_bytes=VMEM_LIMIT)


def _call_name(kernel, n):
    fn = getattr(kernel, "func", kernel)
    return f"{fn.__name__.strip('_')}_{n}"


def _const_spec(shape, grid_rank):
    nd = len(shape)
    if grid_rank == 1:
        return pl.BlockSpec(shape, lambda i: (0,) * nd, pipeline_mode=pl.Buffered(1))
    return pl.BlockSpec(shape, lambda b, c: (0,) * nd, pipeline_mode=pl.Buffered(1))


def _row_tile(m, tile):
    return tile if m % tile == 0 else m


def _layer_spec(stacked, layer):
    nd = stacked.ndim
    return pl.BlockSpec((None,) + stacked.shape[1:], lambda i: (layer,) + (0,) * (nd - 1),
                        pipeline_mode=pl.Buffered(1))


def _rowwise_call(kernel, row_ins, const_ins, outs, m, tile=ROW_TILE, extra_ins=()):
    tm = _row_tile(m, tile)
    rows = lambda n: pl.BlockSpec((tm, n), lambda i: (i, 0))
    consts = [c if isinstance(c, tuple) else (c, None) for c in const_ins]
    return pl.pallas_call(
        kernel,
        name=_call_name(kernel, m),
        grid=(m // tm,),
        in_specs=([rows(a.shape[1]) for a in row_ins] + [spec for _, spec in extra_ins]
                  + [_const_spec(a.shape, 1) if layer is None else _layer_spec(a, layer) for a, layer in consts]),
        out_specs=[rows(n) for n, _ in outs],
        out_shape=[jax.ShapeDtypeStruct((m, n), dt) for n, dt in outs],
        compiler_params=_params(("parallel",)),
    )(*row_ins, *[a for a, _ in extra_ins], *[a for a, _ in consts])


def _scan_call(kernel, seq_ins, state_ins, const_ins, seq_out_widths, state_out_shapes, scratch):
    bsz, t = seq_ins[0].shape[:2]
    seq = lambda n: pl.BlockSpec((None, SCAN_CHUNK, n), lambda b, c: (b, c, 0))
    per_seq = lambda shape: pl.BlockSpec((None,) + tuple(shape[1:]), lambda b, c: (b,) + (0,) * (len(shape) - 1))
    return pl.pallas_call(
        kernel,
        name=_call_name(kernel, bsz),
        grid=(bsz, t // SCAN_CHUNK),
        in_specs=([seq(a.shape[2]) for a in seq_ins] + [per_seq(a.shape) for a in state_ins]
                  + [_const_spec(a.shape, 2) for a in const_ins]),
        out_specs=[seq(n) for n in seq_out_widths] + [per_seq(s) for s in state_out_shapes],
        out_shape=([jax.ShapeDtypeStruct((bsz, t, n), seq_ins[0].dtype) for n in seq_out_widths]
                   + [jax.ShapeDtypeStruct(s, F32) for s in state_out_shapes]),
        scratch_shapes=[pltpu.VMEM(shape, dt) for shape, dt in scratch],
        compiler_params=_params(("parallel", "arbitrary")),
    )(*seq_ins, *state_ins, *const_ins)


def _step_call(kernel, row_ins, state_ins, const_ins, row_out_widths, state_out_shapes):
    bsz = row_ins[0].shape[0]
    rows = lambda n: pl.BlockSpec((STEP_SEQS, n), lambda i: (i, 0))
    state = lambda shape: pl.BlockSpec((STEP_SEQS,) + tuple(shape[1:]), lambda i: (i, 0, 0))
    return pl.pallas_call(
        kernel,
        name=_call_name(kernel, bsz),
        grid=(bsz // STEP_SEQS,),
        in_specs=([rows(a.shape[1]) for a in row_ins] + [state(a.shape) for a in state_ins]
                  + [_const_spec(a.shape, 1) for a in const_ins]),
        out_specs=[rows(n) for n in row_out_widths] + [state(s) for s in state_out_shapes],
        out_shape=([jax.ShapeDtypeStruct((bsz, n), F32) for n in row_out_widths]
                   + [jax.ShapeDtypeStruct(s, F32) for s in state_out_shapes]),
        compiler_params=_params(("parallel",)),
    )(*row_ins, *state_ins, *const_ins)


def _rwkv_step_call(rows, wkv, rk, lnw, lnb):
    bsz = wkv.shape[0]
    assert bsz == LANES, "the single-token RWKV kernel keeps the sequences on the lanes"
    n = RWKV_HEAD_DIM
    pair = pl.BlockSpec((bsz, LANES), lambda i: (0, i))
    vec = pl.BlockSpec((1, LANES), lambda i: (0, i))
    state = pl.BlockSpec((2, n, n, bsz), lambda i: (i, 0, 0, 0))
    y, new = pl.pallas_call(
        _rwkv_step_kernel,
        name=_call_name(_rwkv_step_kernel, bsz),
        grid=(RWKV_HEADS // 2,),
        in_specs=[pair] * 6 + [state] + [vec] * 3,
        out_specs=[pair, state],
        out_shape=[jax.ShapeDtypeStruct((bsz, RWKV_D), F32), jax.ShapeDtypeStruct((RWKV_HEADS, n, n, bsz), F32)],
        scratch_shapes=[pltpu.VMEM((LANES, bsz), F32)],
        compiler_params=_params(("parallel",)),
    )(*rows, jnp.transpose(wkv, (1, 2, 3, 0)), rk, lnw, lnb)
    return y, jnp.transpose(new, (3, 0, 1, 2))


def _row(v):
    return v.reshape(1, -1).astype(F32)


def _pad_cols(w, width):
    return jnp.pad(w, ((0, 0), (0, width - w.shape[1])))


def _pad_rows_to(w, rows):
    return jnp.pad(w, ((0, rows - w.shape[0]), (0, 0)))


def _blockdiag_slabs(w, scale=1.0):
    hd = MLSTM_HEAD_DIM
    rows = (w * scale).reshape(-1, hd, QKV_BLOCK)
    col = jnp.arange(hd)
    pick = (col[None, :] % QKV_BLOCK == jnp.arange(QKV_BLOCK)[:, None]).astype(w.dtype)
    tiled = jnp.einsum('srd,dc->src', rows, pick, precision=lax.Precision.HIGHEST)
    same_block = col[:, None] // QKV_BLOCK == col[None, :] // QKV_BLOCK
    return jnp.where(same_block, tiled, 0.0).astype(BF16)


def _head_spread():
    ch = jnp.arange(SSD_D_INNER) // SSD_HEAD_DIM
    return (jnp.arange(LANES)[:, None] == ch[None, :]).astype(BF16)


def _mixer0_consts(p):
    ssd = [p['conv_w'][:, :XBC_DIM], p['conv_b'][:, :XBC_DIM], p['small_bias'], p['ssd_a_log'], p['ssd_d'],
           p['ssd_norm'], p['segt']]
    ml = [p['conv_w'][:, XBC_DIM:], p['conv_b'][:, XBC_DIM:], p['small_bias'], p['ml_wq'], p['ml_wk'],
          p['ml_wv'], p['ml_norm'], p['ml_skip']]
    return ssd, ml


def _mixer0_chunked(xbc, xm, z, o_pre, small, conv, ssm, mc, mn, mm, p, bsz, t):
    seq = lambda a: a.reshape(bsz, t, -1)
    tail = jnp.pad(conv, ((0, 0), (SUBLANES - (CONV_W - 1), 0), (0, 0)))
    ssd_c, ml_c = _mixer0_consts(p)
    y_ssd, new_ssm = _scan_call(
        _ssd_kernel, [seq(xbc), seq(z), seq(small)],
        [tail[:, :, :XBC_DIM], ssm.reshape(bsz, SSD_D_INNER, SSD_D_STATE)], ssd_c,
        [SSD_D_INNER], [(bsz, SSD_D_INNER, SSD_D_STATE)],
        [((SCAN_CHUNK + SUBLANES, XBC_DIM), F32), ((SSD_D_INNER, SSD_D_STATE), F32)])
    hm, new_c, new_n, new_m = _scan_call(
        _mlstm_kernel, [seq(xm), seq(o_pre), seq(small)],
        [tail[:, :, XBC_DIM:], mc.reshape(bsz, MLSTM_D_INNER, MLSTM_HEAD_DIM), mn.reshape(bsz, 1, MLSTM_D_INNER),
         _pad_cols(mm, LANES).reshape(bsz, 1, LANES)], ml_c,
        [MLSTM_D_INNER], [(bsz, MLSTM_D_INNER, MLSTM_HEAD_DIM), (bsz, 1, MLSTM_D_INNER), (bsz, 1, LANES)],
        [((SCAN_CHUNK + SUBLANES, MLSTM_D_INNER), F32), ((MLSTM_D_INNER, MLSTM_HEAD_DIM), F32),
         ((1, MLSTM_D_INNER), F32), ((1, LANES), F32)])
    flat = lambda a: a.reshape(bsz * t, -1)
    return flat(y_ssd), flat(hm), new_ssm, new_c, new_n, new_m.reshape(bsz, LANES)


def _mixer0_step(xbc, xm, z, o_pre, small, conv, ssm, mc, mn, mm, p, bsz):
    ssd_c, ml_c = _mixer0_consts(p)
    tails = [conv[:, j, :] for j in range(CONV_W - 1)]
    y_ssd, new_ssm = _step_call(
        _ssd_step_kernel, [xbc] + [tl[:, :XBC_DIM] for tl in tails] + [z, small],
        [ssm.reshape(bsz, SSD_D_INNER, SSD_D_STATE)], ssd_c, [SSD_D_INNER], [(bsz, SSD_D_INNER, SSD_D_STATE)])
    hm, new_n, new_m, new_c = _step_call(
        _mlstm_step_kernel, [xm] + [tl[:, XBC_DIM:] for tl in tails]
        + [o_pre, small, mn.reshape(bsz, MLSTM_D_INNER), _pad_cols(mm, LANES)],
        [mc.reshape(bsz, MLSTM_D_INNER, MLSTM_HEAD_DIM)], ml_c,
        [MLSTM_D_INNER, MLSTM_D_INNER, LANES], [(bsz, MLSTM_D_INNER, MLSTM_HEAD_DIM)])
    return y_ssd, hm, new_ssm, new_c, new_n, new_m


def _trunk(x, conv, ssm, mc, mn, mm, shift, wkv, p):
    bsz, t, d = x.shape
    m = bsz * t
    step = t == 1
    assert step or t % SCAN_CHUNK == 0, "a group is either single-token or a multiple of the scan chunk"
    assert not step or bsz % STEP_SEQS == 0
    act = F32 if step else BF16
    x2 = x.reshape(m, d)

    z, xbc, xm, o_pre, small = _rowwise_call(
        _in0_kernel, [x2], [p['norm_mix0'], p['w_in0']],
        [(SSD_D_INNER, act), (XBC_DIM, act), (MLSTM_D_INNER, act), (MLSTM_D_INNER, act), (LANES, F32)], m,
        tile=ROW_TILE)
    last_rows = lambda a: a.reshape(bsz, t, -1)[:, -(CONV_W - 1):].astype(F32)
    conv_in_tail = jnp.concatenate([last_rows(xbc), last_rows(xm)], axis=-1)
    new_conv = jnp.concatenate([conv, conv_in_tail], axis=1)[:, -(CONV_W - 1):]
    if step:
        y_ssd, hm, new_ssm, new_c, new_n, new_m = _mixer0_step(xbc, xm, z, o_pre, small, conv, ssm, mc, mn, mm, p, bsz)
    else:
        y_ssd, hm, new_ssm, new_c, new_n, new_m = _mixer0_chunked(xbc, xm, z, o_pre, small, conv, ssm, mc, mn, mm,
                                                                  p, bsz, t)
    xn1, x2b = _rowwise_call(
        _mix0_ffn_kernel, [y_ssd, hm, x2],
        [p['w_out0'], p['norm_ffn0'], (p['ffn_gu'], 0), (p['ffn_d'], 0), p['norm_mix1']], [(d, act), (d, F32)], m,
        tile=ROW_TILE)

    rw_in_c = [p['rw_mu'], p['rw_wr'], p['rw_wk'], p['rw_wv'], p['rw_w1'], p['rw_w2'], p['rw_a1'], p['rw_a2'],
               p['rw_g1'], p['rw_g2'], p['rw_w0'], p['rw_a0'], p['rw_k_k'], p['rw_k_a']]
    rw_outs = [(d, act), (d, F32)] + [(d, act)] * 5
    if step:
        rw = _rowwise_call(_rw_in_kernel, [xn1, shift], rw_in_c, rw_outs, m)
    else:
        tm = _row_tile(t, ROW_TILE)
        tiles_per_seq = t // tm
        prev_spec = pl.BlockSpec((PREV_ROWS, d), lambda i: (jnp.maximum(i * (tm // PREV_ROWS) - 1, 0), 0))
        shift_spec = pl.BlockSpec((None, 1, d), lambda i: (i // tiles_per_seq, 0, 0))
        rw = _rowwise_call(functools.partial(_rw_in_shift_kernel, tiles_per_seq=tiles_per_seq), [xn1], rw_in_c,
                           rw_outs, m, tile=tm, extra_ins=[(xn1, prev_spec), (shift.reshape(bsz, 1, d), shift_spec)])
    g = rw[6]
    rw_c = [p['rw_r_k'], p['rw_ln_w'], p['rw_ln_b']]
    if step:
        y_rw, new_wkv = _rwkv_step_call(list(rw[:6]), wkv, *rw_c)
    else:
        wkv3 = wkv.reshape(bsz, RWKV_D, RWKV_HEAD_DIM)
        y_rw, new_wkv = _scan_call(
            _rwkv_kernel, [a.reshape(bsz, t, d) for a in rw[:6]], [wkv3], rw_c,
            [d], [wkv3.shape], [((RWKV_D, 2 * RWKV_HEAD_DIM), F32)])
        y_rw = y_rw.reshape(m, d)
    new_wkv = new_wkv.reshape(wkv.shape)
    (y,) = _rowwise_call(
        _mix1_ffn_kernel, [y_rw, g, x2b],
        [p['rw_wo'], p['norm_ffn1'], (p['ffn_gu'], 1), (p['ffn_d'], 1), p['norm_final']], [(d, F32)], m, tile=ROW_TILE)

    states = (new_conv, new_ssm.reshape(ssm.shape), new_c.reshape(mc.shape), new_n.reshape(mn.shape),
              new_m[:, :MLSTM_HEADS], xn1.reshape(bsz, t, d)[:, -1].astype(F32), new_wkv)
    return y.reshape(bsz, t, d), states


def kernel(x_prompt, x_sample, state_conv, state_ssm, state_mlstm_c, state_mlstm_n, state_mlstm_m, state_shift, state_wkv, norm_mix, norm_ffn, norm_final, w_in0, conv_w, conv_b, ssd_dt_bias, ssd_a_log, ssd_d, ssd_norm, ml_wq, ml_wk, ml_wv, ml_i_bias, ml_f_bias, ml_norm, ml_skip, w_out0, rw_mu, rw_wr, rw_wk, rw_wv, rw_wo, rw_w0, rw_w1, rw_w2, rw_a0, rw_a1, rw_a2, rw_g1, rw_g2, rw_k_k, rw_k_a, rw_r_k, rw_ln_w, rw_ln_b, ffn_w_gate_up, ffn_w_down):
    assert norm_mix.shape[0] == 2 and w_in0.shape[0] == 1 and rw_wr.shape[0] == 1, "two layers: SSD|mLSTM then RWKV-7"
    s1 = SSD_D_INNER
    s2 = s1 + XBC_DIM + MLSTM_D_INNER
    s3 = s2 + SSD_HEADS
    s4 = s3 + MLSTM_D_INNER
    s5 = s4 + MLSTM_HEADS
    w0 = w_in0[0]
    small_w = _pad_cols(jnp.concatenate([w0[:, s2:s3], w0[:, s4:s5], w0[:, s5:]], axis=1), LANES)
    lora = lambda w1, w2, width: (_pad_cols(w1, width).astype(BF16), _pad_rows_to(w2, width).astype(BF16))
    rw_w1p, rw_w2p = lora(rw_w1[0], rw_w2[0], LANES)
    rw_a1p, rw_a2p = lora(rw_a1[0], rw_a2[0], LANES)
    rw_g1p, rw_g2p = lora(rw_g1[0], rw_g2[0], 2 * LANES)
    p = dict(
        norm_mix0=_row(norm_mix[0]), norm_mix1=_row(norm_mix[1]), norm_ffn0=_row(norm_ffn[0]),
        norm_ffn1=_row(norm_ffn[1]), norm_final=_row(norm_final),
        w_in0=jnp.concatenate([w0[:, :s2], w0[:, s3:s4], small_w], axis=1).astype(BF16),
        conv_w=conv_w[0], conv_b=_row(conv_b[0]),
        small_bias=_pad_cols(jnp.concatenate([_row(ssd_dt_bias[0]), _row(ml_i_bias[0]), _row(ml_f_bias[0])], axis=1),
                             LANES),
        ssd_a_log=_pad_cols(_row(ssd_a_log[0]), LANES), ssd_d=_row(jnp.repeat(ssd_d[0], SSD_HEAD_DIM)),
        ssd_norm=_row(ssd_norm[0]),
        ml_wq=_blockdiag_slabs(ml_wq[0]), ml_wk=_blockdiag_slabs(ml_wk[0], MLSTM_HEAD_DIM ** -0.5),
        ml_wv=_blockdiag_slabs(ml_wv[0]), ml_norm=_row(ml_norm[0]), ml_skip=_row(ml_skip[0]),
        w_out0=w_out0[0].astype(BF16),
        rw_mu=_pad_rows_to(rw_mu[0], SUBLANES),
        rw_wr=rw_wr[0].astype(BF16), rw_wk=rw_wk[0].astype(BF16), rw_wv=rw_wv[0].astype(BF16),
        rw_wo=rw_wo[0].astype(BF16), rw_w0=_row(rw_w0[0]), rw_a0=_row(rw_a0[0]),
        rw_w1=rw_w1p, rw_w2=rw_w2p, rw_a1=rw_a1p, rw_a2=rw_a2p, rw_g1=rw_g1p, rw_g2=rw_g2p,
        rw_k_k=_row(rw_k_k[0]), rw_k_a=_row(rw_k_a[0]), rw_r_k=_row(rw_r_k[0]),
        rw_ln_w=_row(rw_ln_w[0]), rw_ln_b=_row(rw_ln_b[0]),
        ffn_gu=ffn_w_gate_up.astype(BF16), ffn_d=ffn_w_down.astype(BF16),
        segt=_head_spread(),
    )
    bp = x_prompt.shape[0]
    zeros = lambda s: jnp.zeros((bp,) + s.shape[2:], F32)
    y_p, st_p = _trunk(x_prompt, zeros(state_conv), zeros(state_ssm), zeros(state_mlstm_c),
                       zeros(state_mlstm_n), zeros(state_mlstm_m), zeros(state_shift), zeros(state_wkv), p)
    y_s, st_s = _trunk(x_sample, state_conv[0], state_ssm[0], state_mlstm_c[0], state_mlstm_n[0],
                       state_mlstm_m[0], state_shift[0], state_wkv[0], p)
    out = [y_p, y_s]
    for a, b in zip(st_p, st_s):
        out += [a[None], b[None]]
    return tuple(out)
```

```python
import functools

import jax
import jax.numpy as jnp
from jax import lax
from jax.experimental import pallas as pl
from jax.experimental.pallas import tpu as pltpu

F32 = jnp.float32
BF16 = jnp.bfloat16

D_MODEL = 1024
NORM_EPS = 1e-5
CONV_W = 4
SSD_HEADS = 16
SSD_HEAD_DIM = 64
SSD_GROUPS = 2
SSD_D_STATE = 128
SSD_D_INNER = SSD_HEADS * SSD_HEAD_DIM
SSD_GROUP_W = SSD_D_INNER // SSD_GROUPS
XBC_DIM = SSD_D_INNER + 2 * SSD_GROUPS * SSD_D_STATE
MLSTM_HEADS = 4
MLSTM_HEAD_DIM = 256
MLSTM_D_INNER = MLSTM_HEADS * MLSTM_HEAD_DIM
QKV_BLOCK = 4
RWKV_HEADS = 16
RWKV_HEAD_DIM = 64
RWKV_D = RWKV_HEADS * RWKV_HEAD_DIM
RWKV_LN_EPS = 64e-5
D_FF = 2816
FF_CHUNK = 256
LANES = 128
SUBLANES = 8
SMALL_DT, SMALL_I, SMALL_F = 0, 16, 20
NEG = -1e30
EXP_NEG_HALF = 0.6065306597126334
SCAN_CHUNK = 128
STEP_SEQS = SUBLANES
ROW_TILE = 512
PREV_ROWS = 16
VMEM_LIMIT = 56 * 1024 * 1024


def _dot(a, b):
    return jnp.dot(a.astype(BF16), b.astype(BF16), preferred_element_type=F32)


def _dot_nt(a, b):
    return lax.dot_general(a.astype(BF16), b.astype(BF16), (((1,), (1,)), ((), ())),
                           preferred_element_type=F32)


def _dot_tn(a, b):
    return lax.dot_general(a.astype(BF16), b.astype(BF16), (((0,), (0,)), ((), ())),
                           preferred_element_type=F32)


def _split(x, terms):
    parts = []
    rem = x
    for _ in range(terms):
        p = rem.astype(BF16)
        parts.append(p)
        rem = rem - p.astype(F32)
    return parts


def _dot_exact_rhs(x, m, terms=2):
    out = None
    for p in _split(x, terms):
        t = jnp.dot(p, m, preferred_element_type=F32)
        out = t if out is None else out + t
    return out


def _tri_cumsum(tri, x, terms=3):
    out = None
    for p in _split(x, terms):
        t = jnp.dot(tri, p, preferred_element_type=F32)
        out = t if out is None else out + t
    return out


def _rms_rows(x, g):
    return x * lax.rsqrt(jnp.mean(x * x, axis=-1, keepdims=True) + NORM_EPS) * g


def _group_rms(x, width):
    outs = []
    for lo in range(0, x.shape[1], width):
        grp = x[:, lo:lo + width]
        outs.append(grp * lax.rsqrt(jnp.mean(grp * grp, axis=-1, keepdims=True) + NORM_EPS))
    return jnp.concatenate(outs, axis=1)


def _sigmoid(x):
    return 0.5 * jnp.tanh(0.5 * x) + 0.5


def _silu(x):
    h = 0.5 * x
    return h * jnp.tanh(h) + h


def _softplus(x):
    return jnp.maximum(x, 0.0) + jnp.log(1.0 + jnp.exp(-jnp.abs(x)))


def _log_sigmoid(x):
    return jnp.minimum(x, 0.0) - jnp.log(1.0 + jnp.exp(-jnp.abs(x)))


def _tri_masks(n):
    row = lax.broadcasted_iota(jnp.int32, (n, n), 0)
    col = lax.broadcasted_iota(jnp.int32, (n, n), 1)
    return row >= col, row > col


def _chunk_conv(tail, u_ref, conv0_ref, cw_ref, cb_ref, first):
    assert u_ref.dtype == BF16
    rows = u_ref.shape[0]
    taps = CONV_W - 1

    @pl.when(first)
    def _():
        tail[...] = conv0_ref[...]

    ub = u_ref[...]
    u = ub.astype(F32)
    out_row = lax.broadcasted_iota(jnp.int32, (taps * rows, rows), 0)
    src_row = lax.broadcasted_iota(jnp.int32, (taps * rows, rows), 1)
    back = taps - out_row // rows
    shift = jnp.where(src_row == out_row % rows - back, 1.0, 0.0).astype(BF16)
    shifted = jnp.dot(shift, ub, preferred_element_type=F32)
    conv = cb_ref[...] + u * cw_ref[taps:CONV_W, :]
    window = jnp.concatenate([tail[...], jnp.zeros((SUBLANES, u.shape[1]), F32)], axis=0)
    edge = jnp.zeros((SUBLANES, u.shape[1]), F32)
    for j in range(taps):
        conv = conv + shifted[j * rows:(j + 1) * rows, :] * cw_ref[j:j + 1, :]
        edge = edge + pltpu.roll(window, shift=taps - j, axis=0)[SUBLANES:2 * SUBLANES, :] * cw_ref[j:j + 1, :]
    tail[...] = u[rows - SUBLANES:rows, :]
    return jnp.concatenate([conv[0:SUBLANES, :] + edge, conv[SUBLANES:rows, :]], axis=0), u


def _step_conv(u_ref, tail_refs, cw_ref, cb_ref):
    conv = cb_ref[...] + u_ref[...] * cw_ref[CONV_W - 1:CONV_W, :]
    for j, t in enumerate(tail_refs):
        conv = conv + t[...] * cw_ref[j:j + 1, :]
    return conv


def _blockdiag_qkv(act, raw, wq_ref, wk_ref, wv_ref):
    hd = MLSTM_HEAD_DIM
    cat = lambda x, w: jnp.concatenate(
        [_dot(x[:, s * hd:(s + 1) * hd], w[s]) for s in range(MLSTM_HEADS)], axis=1)
    return cat(act, wq_ref), cat(act, wk_ref), cat(raw, wv_ref)


def _head_sum(x):
    half = lax.broadcasted_iota(jnp.int32, (x.shape[0], LANES), 1) < RWKV_HEAD_DIM
    outs = []
    for lo in range(0, x.shape[1], LANES):
        xp = x[:, lo:lo + LANES]
        s0 = jnp.sum(jnp.where(half, xp, 0.0), axis=1, keepdims=True)
        s1 = jnp.sum(jnp.where(half, 0.0, xp), axis=1, keepdims=True)
        outs.append(jnp.where(half, s0, s1))
    return jnp.concatenate(outs, axis=1)


def _rwkv_head_norm(y, r, k, v, rk_ref, lnw_ref, lnb_ref):
    inv_n = 1.0 / RWKV_HEAD_DIM
    dlt = y - _head_sum(y) * inv_n
    yn = dlt * lax.rsqrt(_head_sum(dlt * dlt) * inv_n + RWKV_LN_EPS)
    return yn * lnw_ref[...] + lnb_ref[...] + _head_sum(r * k * rk_ref[...]) * v


def _in0_kernel(x_ref, g_ref, w_ref, z_ref, xbc_ref, xm_ref, o_ref, sm_ref):
    xn = _rms_rows(x_ref[...], g_ref[...]).astype(BF16)
    off = 0
    for ref in (z_ref, xbc_ref, xm_ref, o_ref, sm_ref):
        n = ref.shape[-1]
        ref[...] = jnp.dot(xn, w_ref[:, off:off + n], preferred_element_type=F32).astype(ref.dtype)
        off += n


def _swiglu_residual(x, g_ref, wgu_ref, wd_ref):
    xn = _rms_rows(x, g_ref[...]).astype(BF16)
    acc = None
    for c in range(D_FF // FF_CHUNK):
        lo = c * FF_CHUNK
        gate = jnp.dot(xn, wgu_ref[:, lo:lo + FF_CHUNK], preferred_element_type=F32)
        up = jnp.dot(xn, wgu_ref[:, D_FF + lo:D_FF + lo + FF_CHUNK], preferred_element_type=F32)
        part = _dot(_silu(gate) * up, wd_ref[lo:lo + FF_CHUNK, :])
        acc = part if acc is None else acc + part
    return x + acc


def _mix0_ffn_kernel(y_ref, h_ref, x_ref, wo_ref, g_ref, wgu_ref, wd_ref, pg_ref, xn_ref, xo_ref):
    mixed = jnp.concatenate([y_ref[...].astype(BF16), h_ref[...].astype(BF16)], axis=1)
    x1 = x_ref[...] + jnp.dot(mixed, wo_ref[...], preferred_element_type=F32)
    x2 = _swiglu_residual(x1, g_ref, wgu_ref, wd_ref)
    xo_ref[...] = x2
    xn_ref[...] = _rms_rows(x2, pg_ref[...]).astype(xn_ref.dtype)


def _mix1_ffn_kernel(y_ref, gate_ref, x_ref, wo_ref, g_ref, wgu_ref, wd_ref, pg_ref, o_ref):
    x3 = x_ref[...] + _dot(y_ref[...].astype(F32) * gate_ref[...].astype(F32), wo_ref[...])
    o_ref[...] = _rms_rows(_swiglu_residual(x3, g_ref, wgu_ref, wd_ref), pg_ref[...])


def _rw_in_body(xn, xp, mu_ref, wr_ref, wk_ref, wv_ref, w1_ref, w2_ref, a1_ref, a2_ref,
                g1_ref, g2_ref, w0_ref, a0_ref, kk_w_ref, ka_w_ref,
                r_ref, lw_ref, k_ref, v_ref, kk_ref, a_ref, g_ref):
    xx = xp - xn
    xr, xw, xk, xv, xa, xg = (xn + xx * mu_ref[c:c + 1, :] for c in range(6))
    r_ref[...] = _dot(xr, wr_ref[...]).astype(r_ref.dtype)
    k = _dot(xk, wk_ref[...])
    v_ref[...] = _dot(xv, wv_ref[...]).astype(v_ref.dtype)
    z = w0_ref[...] + _dot(jnp.tanh(_dot(xw, w1_ref[...])), w2_ref[...])
    lw_ref[...] = -EXP_NEG_HALF * _sigmoid(z)
    a = _sigmoid(a0_ref[...] + _dot(_dot(xa, a1_ref[...]), a2_ref[...]))
    a_ref[...] = a.astype(a_ref.dtype)
    g_ref[...] = _dot(_sigmoid(_dot(xg, g1_ref[...])), g2_ref[...]).astype(g_ref.dtype)
    kk = k * kk_w_ref[...]
    ss = _head_sum(kk * kk)
    kk_ref[...] = (kk * lax.rsqrt(jnp.maximum(ss, 1e-24))).astype(kk_ref.dtype)
    k_ref[...] = (k * (1.0 + (a - 1.0) * ka_w_ref[...])).astype(k_ref.dtype)


def _rw_in_kernel(xn_ref, xp_ref, *refs):
    _rw_in_body(xn_ref[...].astype(F32), xp_ref[...].astype(F32), *refs)


def _rw_in_shift_kernel(xn_ref, prev_ref, shift_ref, *refs, tiles_per_seq):
    xn = xn_ref[...].astype(F32)
    starts_seq = pl.program_id(0) % tiles_per_seq == 0
    last = prev_ref.shape[0] - 1
    prev = jnp.where(starts_seq, shift_ref[...], prev_ref[last:last + 1, :].astype(F32))
    row = lax.broadcasted_iota(jnp.int32, xn.shape, 0)
    xp = jnp.where(row == 0, prev, pltpu.roll(xn, shift=1, axis=0))
    _rw_in_body(xn, xp, *refs)


def _ssd_kernel(xbc_ref, z_ref, sm_ref, conv0_ref, st0_ref, cw_ref, cb_ref, bias_ref, alog_ref,
                dexp_ref, nw_ref, segt_ref, y_ref, st_ref, ext, state):
    c = pl.program_id(1)
    first = c == 0
    rows = xbc_ref.shape[0]

    @pl.when(first)
    def _():
        state[...] = st0_ref[...]

    conv, _ = _chunk_conv(ext, xbc_ref, conv0_ref, cw_ref, cb_ref, first)
    act = _silu(conv)
    xs = act[:, 0:SSD_D_INNER]
    gn = SSD_GROUPS * SSD_D_STATE
    bm = act[:, SSD_D_INNER:SSD_D_INNER + gn]
    cm = act[:, SSD_D_INNER + gn:SSD_D_INNER + 2 * gn]

    tril, _ = _tri_masks(rows)
    tri = jnp.where(tril, 1.0, 0.0).astype(BF16)
    lane = lax.broadcasted_iota(jnp.int32, (rows, LANES), 1)
    head_lane = (lane >= SMALL_DT) & (lane < SMALL_DT + SSD_HEADS)
    dt = jnp.where(head_lane, _softplus(sm_ref[...] + bias_ref[...]), 0.0)
    acum = _tri_cumsum(tri, dt * (-jnp.exp(alog_ref[...])))
    acum_t = acum.T
    dt_t = dt.T
    segt = segt_ref[...]
    eac_x = _dot_exact_rhs(jnp.exp(acum), segt, terms=1)
    wend_x = _dot_exact_rhs(jnp.exp(acum[rows - 1:rows, :] - acum) * dt, segt, terms=1)
    xw = xs * wend_x

    half = lane < SSD_HEAD_DIM
    heads_per_group = SSD_HEADS // SSD_GROUPS
    grp = lambda x, g: x[:, g * SSD_D_STATE:(g + 1) * SSD_D_STATE]
    cbs = [_dot_nt(grp(cm, g), grp(bm, g)) for g in range(SSD_GROUPS)]
    decays = [jnp.exp(jnp.where(tril, acum[:, j:j + 1] - acum_t[j:j + 1, :], NEG)) for j in range(SSD_HEADS)]
    wts = [decays[j] * cbs[j // heads_per_group] * dt_t[j:j + 1, :] for j in range(SSD_HEADS)]
    lhs = [jnp.concatenate([wts[2 * p], wts[2 * p + 1]], axis=1).astype(BF16) for p in range(SSD_HEADS // 2)]
    rhs = []
    for p in range(SSD_HEADS // 2):
        xp = xs[:, p * LANES:(p + 1) * LANES]
        rhs.append(jnp.concatenate([jnp.where(half, xp, 0.0), jnp.where(half, 0.0, xp)], axis=0).astype(BF16))
    pairs = [_dot(l, r) for l, r in zip(lhs, rhs)]
    y = jnp.concatenate(pairs, axis=1)

    st = state[...]
    gw = SSD_GROUP_W
    y_state = jnp.concatenate(
        [_dot_nt(cm[:, g * SSD_D_STATE:(g + 1) * SSD_D_STATE], st[g * gw:(g + 1) * gw, :])
         for g in range(SSD_GROUPS)], axis=1)
    y = y + eac_x * y_state + dexp_ref[...] * xs
    y_ref[...] = (_group_rms(y * _silu(z_ref[...].astype(F32)), gw) * nw_ref[...]).astype(y_ref.dtype)

    dcol = jnp.exp(acum_t[:, rows - 1:rows])
    for g in range(SSD_GROUPS):
        upd = _dot_tn(xw[:, g * gw:(g + 1) * gw], bm[:, g * SSD_D_STATE:(g + 1) * SSD_D_STATE])
        for h in range(heads_per_group):
            j = g * heads_per_group + h
            lo = j * SSD_HEAD_DIM
            state[lo:lo + SSD_HEAD_DIM, :] = (st[lo:lo + SSD_HEAD_DIM, :] * dcol[j:j + 1, :]
                                              + upd[h * SSD_HEAD_DIM:(h + 1) * SSD_HEAD_DIM, :])

    @pl.when(c == pl.num_programs(1) - 1)
    def _():
        st_ref[...] = state[...]


def _mlstm_kernel(xm_ref, o_ref, sm_ref, conv0_ref, c0_ref, n0_ref, m0_ref, cw_ref, cb_ref, bias_ref,
                  wq_ref, wk_ref, wv_ref, nw_ref, skip_ref, h_ref, c_ref, n_ref, m_ref,
                  ext, cst, nst, mst):
    c = pl.program_id(1)
    first = c == 0
    rows = xm_ref.shape[0]

    @pl.when(first)
    def _():
        cst[...] = c0_ref[...]
        nst[...] = n0_ref[...]
        mst[...] = m0_ref[...]

    conv, raw = _chunk_conv(ext, xm_ref, conv0_ref, cw_ref, cb_ref, first)
    act = _silu(conv)
    q, k, v = _blockdiag_qkv(act, raw, wq_ref, wk_ref, wv_ref)

    tril, _ = _tri_masks(rows)
    tri = jnp.where(tril, 1.0, 0.0).astype(BF16)
    logi = sm_ref[...] + bias_ref[...]
    bcum = _tri_cumsum(tri, _log_sigmoid(logi))
    bcum_t = bcum.T
    logi_t = logi.T
    m_all = mst[...]
    lane1 = lax.broadcasted_iota(jnp.int32, (1, LANES), 1)
    hd = MLSTM_HEAD_DIM

    heads = range(MLSTM_HEADS)
    cols = [slice(h * hd, (h + 1) * hd) for h in heads]
    qs, ks, vs = ([t[:, c] for c in cols] for t in (q, k, v))
    qbs, vbs = [x.astype(BF16) for x in qs], [x.astype(BF16) for x in vs]
    bcs = [bcum[:, SMALL_F + h:SMALL_F + h + 1] for h in heads]
    lis = [logi[:, SMALL_I + h:SMALL_I + h + 1] for h in heads]
    m_hs = [m_all[:, h:h + 1] for h in heads]
    c_hs = [cst[c, :] for c in cols]
    n_hs = [nst[:, c] for c in cols]
    dlogs = [jnp.where(tril, bcs[h] - bcum_t[SMALL_F + h:SMALL_F + h + 1, :]
                       + logi_t[SMALL_I + h:SMALL_I + h + 1, :], NEG) for h in heads]
    inters = [bcs[h] + m_hs[h] for h in heads]
    m_ts = [jnp.maximum(inters[h], jnp.max(dlogs[h], axis=1, keepdims=True)) for h in heads]
    qks = [_dot_nt(qbs[h], ks[h]) for h in heads]
    qcs = [_dot(qbs[h], c_hs[h]) for h in heads]
    ss = [qks[h] * jnp.exp(dlogs[h] - m_ts[h]) for h in heads]
    w_inters = [jnp.exp(inters[h] - m_ts[h]) for h in heads]
    nums = [_dot(ss[h], vbs[h]) + w_inters[h] * qcs[h] for h in heads]
    dens = [jnp.sum(ss[h], axis=1, keepdims=True)
            + w_inters[h] * jnp.sum(qs[h] * n_hs[h], axis=1, keepdims=True) for h in heads]
    hs = [nums[h] * (1.0 / jnp.maximum(jnp.abs(dens[h]), jnp.exp(-m_ts[h]))) for h in heads]
    b_ends = [bcs[h][rows - 1:rows, :] for h in heads]
    wlogs = [b_ends[h] - bcs[h] + lis[h] for h in heads]
    m_news = [jnp.maximum(b_ends[h] + m_hs[h], jnp.max(wlogs[h], axis=0, keepdims=True)) for h in heads]
    dcs = [jnp.exp(b_ends[h] + m_hs[h] - m_news[h]) for h in heads]
    kwss = [ks[h] * jnp.exp(wlogs[h] - m_news[h]) for h in heads]
    upds = [_dot_tn(kwss[h], vbs[h]) for h in heads]
    for h in heads:
        cst[cols[h], :] = dcs[h] * c_hs[h] + upds[h]
        nst[:, cols[h]] = dcs[h] * n_hs[h] + jnp.sum(kwss[h], axis=0, keepdims=True)
        m_all = jnp.where(lane1 == h, m_news[h], m_all)
    mst[...] = m_all

    hm = _group_rms(jnp.concatenate(hs, axis=1), hd) * nw_ref[...]
    h_ref[...] = ((hm + skip_ref[...] * act) * _sigmoid(o_ref[...].astype(F32))).astype(h_ref.dtype)

    @pl.when(c == pl.num_programs(1) - 1)
    def _():
        c_ref[...] = cst[...]
        n_ref[...] = nst[...]
        m_ref[...] = mst[...]


def _rwkv_kernel(r_ref, lw_ref, k_ref, v_ref, kk_ref, a_ref, s0_ref, rk_ref, lnw_ref, lnb_ref,
                 y_ref, s_ref, sblk):
    c = pl.program_id(1)
    rows = r_ref.shape[0]

    @pl.when(c == 0)
    def _():
        s0 = s0_ref[...]
        shape = (RWKV_D, 2 * RWKV_HEAD_DIM)
        head_parity = (lax.broadcasted_iota(jnp.int32, shape, 0) // RWKV_HEAD_DIM) % 2
        lane_half = lax.broadcasted_iota(jnp.int32, shape, 1) // RWKV_HEAD_DIM
        sblk[...] = jnp.where(head_parity == lane_half, jnp.concatenate([s0, s0], axis=1), 0.0)

    r, lw, k, v, kk, a = (t[...].astype(F32) for t in (r_ref, lw_ref, k_ref, v_ref, kk_ref, a_ref))
    tril, strict = _tri_masks(rows)
    tri = jnp.where(tril, 1.0, 0.0).astype(BF16)
    cum = _tri_cumsum(tri, lw, terms=2)
    last = cum[rows - 1:rows, :]
    einv = jnp.exp(-cum)
    eend = jnp.exp(last - cum)
    ka = kk * a
    at = -(kk * jnp.exp(cum - lw))
    rt = r * jnp.exp(cum)
    bt = ka * einv
    kt = k * einv
    bh = ka * eend
    kh = k * eend
    wl = jnp.exp(last)
    half = lax.broadcasted_iota(jnp.int32, (rows, LANES), 1) < RWKV_HEAD_DIM
    steps = rows.bit_length() - 1

    def two(x):
        return [jnp.where(half, x, 0.0), jnp.where(half, 0.0, x)]

    npairs = RWKV_HEADS // 2
    tril4 = jnp.concatenate([tril] * 4, axis=1)
    strict4 = jnp.concatenate([strict] * 4, axis=1)
    lanes = [slice(p * LANES, (p + 1) * LANES) for p in range(npairs)]

    a2s = [jnp.concatenate([at[:, sl], rt[:, sl]], axis=0).astype(BF16) for sl in lanes]
    rbs = [jnp.concatenate(two(bt[:, sl]) + two(kt[:, sl]), axis=0).astype(BF16) for sl in lanes]
    sps = [sblk[sl, :] for sl in lanes]
    vss = [two(v[:, sl]) for sl in lanes]
    pbks = [_dot_nt(a2, rb) for a2, rb in zip(a2s, rbs)]
    xss = [_dot_nt(a2, sp) for a2, sp in zip(a2s, sps)]
    tops = [jnp.where(strict4, pbk[0:rows, :], 0.0) for pbk in pbks]
    us = [xs[0:rows, :] + _dot(top[:, 2 * rows:4 * rows], jnp.concatenate(vs, axis=0))
          for xs, top, vs in zip(xss, tops, vss)]
    pws = [[top[:, e * rows:(e + 1) * rows].astype(BF16) for e in range(2)] for top in tops]
    ybase = [xs[rows:2 * rows, :] for xs in xss]
    w4s = [jnp.where(tril4, pbk[rows:2 * rows, :], 0.0).astype(BF16) for pbk in pbks]

    for j in range(steps):
        for p in range(npairs):
            u2 = jnp.concatenate(two(us[p]), axis=0)
            us[p] = us[p] + _dot(jnp.concatenate(pws[p], axis=1), u2)
        if j < steps - 1:
            for p in range(npairs):
                pws[p] = [_dot(pw, pw).astype(BF16) for pw in pws[p]]

    ys = []
    for p, sl in enumerate(lanes):
        r4 = jnp.concatenate(two(us[p]) + vss[p], axis=0).astype(BF16)
        ys.append(ybase[p] + _dot(w4s[p], r4))
        rh = jnp.concatenate(two(bh[:, sl]) + two(kh[:, sl]), axis=0)
        sblk[sl, :] = sps[p] * wl[:, sl] + _dot_tn(r4, rh)
    y = jnp.concatenate(ys, axis=1)
    y_ref[...] = _rwkv_head_norm(y, r, k, v, rk_ref, lnw_ref, lnb_ref).astype(y_ref.dtype)

    @pl.when(c == pl.num_programs(1) - 1)
    def _():
        sb = sblk[...]
        s_ref[...] = sb[:, 0:RWKV_HEAD_DIM] + sb[:, RWKV_HEAD_DIM:2 * RWKV_HEAD_DIM]


def _ssd_step_kernel(xbc_ref, t0_ref, t1_ref, t2_ref, z_ref, sm_ref, st_ref, cw_ref, cb_ref, bias_ref,
                     alog_ref, dexp_ref, nw_ref, segt_ref, y_ref, sto_ref):
    nseq = xbc_ref.shape[0]
    act = _silu(_step_conv(xbc_ref, (t0_ref, t1_ref, t2_ref), cw_ref, cb_ref))
    xs = act[:, 0:SSD_D_INNER]
    gn = SSD_GROUPS * SSD_D_STATE
    bm = act[:, SSD_D_INNER:SSD_D_INNER + gn]
    cm = act[:, SSD_D_INNER + gn:SSD_D_INNER + 2 * gn]
    lane = lax.broadcasted_iota(jnp.int32, (nseq, LANES), 1)
    head_lane = (lane >= SMALL_DT) & (lane < SMALL_DT + SSD_HEADS)
    dt = jnp.where(head_lane, _softplus(sm_ref[...] + bias_ref[...]), 0.0)
    segt = segt_ref[...]
    dec_t = _dot_exact_rhs(jnp.exp(dt * (-jnp.exp(alog_ref[...]))), segt).T
    xdt = xs * _dot_exact_rhs(dt, segt)
    gw = SSD_GROUP_W
    rowi = lax.broadcasted_iota(jnp.int32, (nseq, gw), 0)
    ys = [jnp.zeros((nseq, gw), F32) for _ in range(SSD_GROUPS)]
    for s in range(nseq):
        for g in range(SSD_GROUPS):
            rs = slice(g * gw, (g + 1) * gw)
            ns = slice(g * SSD_D_STATE, (g + 1) * SSD_D_STATE)
            outer = _dot_tn(jnp.where(rowi == s, xdt[:, rs], 0.0), bm[:, ns])
            new = st_ref[s, rs, :] * dec_t[rs, s:s + 1] + outer
            sto_ref[s, rs, :] = new
            ys[g] = jnp.where(rowi == s, _dot_nt(cm[:, ns], new), ys[g])
    y = jnp.concatenate(ys, axis=1) + dexp_ref[...] * xs
    y_ref[...] = _group_rms(y * _silu(z_ref[...]), gw) * nw_ref[...]


def _mlstm_step_kernel(xm_ref, t0_ref, t1_ref, t2_ref, o_ref, sm_ref, n_ref, m_ref, c_ref, cw_ref, cb_ref,
                       bias_ref, wq_ref, wk_ref, wv_ref, nw_ref, skip_ref, h_ref, no_ref, mo_ref, co_ref):
    nseq = xm_ref.shape[0]
    raw = xm_ref[...]
    act = _silu(_step_conv(xm_ref, (t0_ref, t1_ref, t2_ref), cw_ref, cb_ref))
    q, k, v = _blockdiag_qkv(act, raw, wq_ref, wk_ref, wv_ref)
    logi = sm_ref[...] + bias_ref[...]
    logf = _log_sigmoid(logi)
    m_old = m_ref[...]
    m_all = m_old
    n_old = n_ref[...]
    lane = lax.broadcasted_iota(jnp.int32, (nseq, LANES), 1)
    hd = MLSTM_HEAD_DIM
    rowi = lax.broadcasted_iota(jnp.int32, (nseq, hd), 0)
    hs, ns = [], []
    for h in range(MLSTM_HEADS):
        cols = slice(h * hd, (h + 1) * hd)
        qh, kh, vh = q[:, cols], k[:, cols], v[:, cols]
        lf = logf[:, SMALL_F + h:SMALL_F + h + 1] + m_old[:, h:h + 1]
        li = logi[:, SMALL_I + h:SMALL_I + h + 1]
        m_new = jnp.maximum(lf, li)
        dc = jnp.exp(lf - m_new)
        kws = kh * jnp.exp(li - m_new)
        n_new = dc * n_old[:, cols] + kws
        den = jnp.sum(qh * n_new, axis=1, keepdims=True)
        q_t = qh.T
        num = jnp.zeros((nseq, hd), F32)
        for s in range(nseq):
            outer = _dot_tn(jnp.where(rowi == s, kws, 0.0), vh)
            c_new = dc[s:s + 1, :] * c_ref[s, cols, :] + outer
            co_ref[s, cols, :] = c_new
            num = jnp.where(rowi == s, jnp.sum(q_t[:, s:s + 1] * c_new, axis=0, keepdims=True), num)
        hs.append(num / jnp.maximum(jnp.abs(den), jnp.exp(-m_new)))
        ns.append(n_new)
        m_all = jnp.where(lane == h, m_new, m_all)
    no_ref[...] = jnp.concatenate(ns, axis=1)
    mo_ref[...] = m_all
    hm = _group_rms(jnp.concatenate(hs, axis=1), hd) * nw_ref[...]
    h_ref[...] = (hm + skip_ref[...] * act) * _sigmoid(o_ref[...])


def _rwkv_step_kernel(r_ref, lw_ref, k_ref, v_ref, kk_ref, a_ref, s_ref, rk_ref, lnw_ref, lnb_ref,
                      y_ref, so_ref, yt):
    r, k, v, kk = (t[...] for t in (r_ref, k_ref, v_ref, kk_ref))
    n = RWKV_HEAD_DIM
    w_t, kk_t, ka_t, k_t, r_t, v_t = (x.T for x in (jnp.exp(lw_ref[...]), kk, kk * a_ref[...], k, r, v))
    for e in range(2):
        hs = slice(e * n, (e + 1) * n)
        w_h, kk_h, ka_h, k_h, r_h = w_t[hs], kk_t[hs], ka_t[hs], k_t[hs], r_t[hs]
        for i in range(n):
            row = e * n + i
            st = s_ref[e, i]
            sa = jnp.sum(st * kk_h, axis=0, keepdims=True)
            new = st * w_h - sa * ka_h + v_t[row:row + 1, :] * k_h
            so_ref[e, i] = new
            yt[row:row + 1, :] = jnp.sum(new * r_h, axis=0, keepdims=True)
    y_ref[...] = _rwkv_head_norm(yt[...].T, r, k, v, rk_ref, lnw_ref, lnb_ref)


def _params(sem):
    return pltpu.CompilerParams(dimension_semantics=sem, vmem_limit_bytes=VMEM_LIMIT)


def _call_name(kernel, n):
    fn = getattr(kernel, "func", kernel)
    return f"{fn.__name__.strip('_')}_{n}"


def _const_spec(shape, grid_rank):
    nd = len(shape)
    if grid_rank == 1:
        return pl.BlockSpec(shape, lambda i: (0,) * nd, pipeline_mode=pl.Buffered(1))
    return pl.BlockSpec(shape, lambda b, c: (0,) * nd, pipeline_mode=pl.Buffered(1))


def _row_tile(m, tile):
    return tile if m % tile == 0 else m


def _layer_spec(stacked, layer):
    nd = stacked.ndim
    return pl.BlockSpec((None,) + stacked.shape[1:], lambda i: (layer,) + (0,) * (nd - 1),
                        pipeline_mode=pl.Buffered(1))


def _rowwise_call(kernel, row_ins, const_ins, outs, m, tile=ROW_TILE, extra_ins=()):
    tm = _row_tile(m, tile)
    rows = lambda n: pl.BlockSpec((tm, n), lambda i: (i, 0))
    consts = [c if isinstance(c, tuple) else (c, None) for c in const_ins]
    return pl.pallas_call(
        kernel,
        name=_call_name(kernel, m),
        grid=(m // tm,),
        in_specs=([rows(a.shape[1]) for a in row_ins] + [spec for _, spec in extra_ins]
                  + [_const_spec(a.shape, 1) if layer is None else _layer_spec(a, layer) for a, layer in consts]),
        out_specs=[rows(n) for n, _ in outs],
        out_shape=[jax.ShapeDtypeStruct((m, n), dt) for n, dt in outs],
        compiler_params=_params(("parallel",)),
    )(*row_ins, *[a for a, _ in extra_ins], *[a for a, _ in consts])


def _scan_call(kernel, seq_ins, state_ins, const_ins, seq_out_widths, state_out_shapes, scratch):
    bsz, t = seq_ins[0].shape[:2]
    seq = lambda n: pl.BlockSpec((None, SCAN_CHUNK, n), lambda b, c: (b, c, 0))
    per_seq = lambda shape: pl.BlockSpec((None,) + tuple(shape[1:]), lambda b, c: (b,) + (0,) * (len(shape) - 1))
    return pl.pallas_call(
        kernel,
        name=_call_name(kernel, bsz),
        grid=(bsz, t // SCAN_CHUNK),
        in_specs=([seq(a.shape[2]) for a in seq_ins] + [per_seq(a.shape) for a in state_ins]
                  + [_const_spec(a.shape, 2) for a in const_ins]),
        out_specs=[seq(n) for n in seq_out_widths] + [per_seq(s) for s in state_out_shapes],
        out_shape=([jax.ShapeDtypeStruct((bsz, t, n), seq_ins[0].dtype) for n in seq_out_widths]
                   + [jax.ShapeDtypeStruct(s, F32) for s in state_out_shapes]),
        scratch_shapes=[pltpu.VMEM(shape, dt) for shape, dt in scratch],
        compiler_params=_params(("parallel", "arbitrary")),
    )(*seq_ins, *state_ins, *const_ins)


def _step_call(kernel, row_ins, state_ins, const_ins, row_out_widths, state_out_shapes):
    bsz = row_ins[0].shape[0]
    rows = lambda n: pl.BlockSpec((STEP_SEQS, n), lambda i: (i, 0))
    state = lambda shape: pl.BlockSpec((STEP_SEQS,) + tuple(shape[1:]), lambda i: (i, 0, 0))
    return pl.pallas_call(
        kernel,
        name=_call_name(kernel, bsz),
        grid=(bsz // STEP_SEQS,),
        in_specs=([rows(a.shape[1]) for a in row_ins] + [state(a.shape) for a in state_ins]
                  + [_const_spec(a.shape, 1) for a in const_ins]),
        out_specs=[rows(n) for n in row_out_widths] + [state(s) for s in state_out_shapes],
        out_shape=([jax.ShapeDtypeStruct((bsz, n), F32) for n in row_out_widths]
                   + [jax.ShapeDtypeStruct(s, F32) for s in state_out_shapes]),
        compiler_params=_params(("parallel",)),
    )(*row_ins, *state_ins, *const_ins)


def _rwkv_step_call(rows, wkv, rk, lnw, lnb):
    bsz = wkv.shape[0]
    assert bsz == LANES, "the single-token RWKV kernel keeps the sequences on the lanes"
    n = RWKV_HEAD_DIM
    pair = pl.BlockSpec((bsz, LANES), lambda i: (0, i))
    vec = pl.BlockSpec((1, LANES), lambda i: (0, i))
    state = pl.BlockSpec((2, n, n, bsz), lambda i: (i, 0, 0, 0))
    y, new = pl.pallas_call(
        _rwkv_step_kernel,
        name=_call_name(_rwkv_step_kernel, bsz),
        grid=(RWKV_HEADS // 2,),
        in_specs=[pair] * 6 + [state] + [vec] * 3,
        out_specs=[pair, state],
        out_shape=[jax.ShapeDtypeStruct((bsz, RWKV_D), F32), jax.ShapeDtypeStruct((RWKV_HEADS, n, n, bsz), F32)],
        scratch_shapes=[pltpu.VMEM((LANES, bsz), F32)],
        compiler_params=_params(("parallel",)),
    )(*rows, jnp.transpose(wkv, (1, 2, 3, 0)), rk, lnw, lnb)
    return y, jnp.transpose(new, (3, 0, 1, 2))


def _row(v):
    return v.reshape(1, -1).astype(F32)


def _pad_cols(w, width):
    return jnp.pad(w, ((0, 0), (0, width - w.shape[1])))


def _pad_rows_to(w, rows):
    return jnp.pad(w, ((0, rows - w.shape[0]), (0, 0)))


def _blockdiag_slabs(w, scale=1.0):
    hd = MLSTM_HEAD_DIM
    rows = (w * scale).reshape(-1, hd, QKV_BLOCK)
    col = jnp.arange(hd)
    pick = (col[None, :] % QKV_BLOCK == jnp.arange(QKV_BLOCK)[:, None]).astype(w.dtype)
    tiled = jnp.einsum('srd,dc->src', rows, pick, precision=lax.Precision.HIGHEST)
    same_block = col[:, None] // QKV_BLOCK == col[None, :] // QKV_BLOCK
    return jnp.where(same_block, tiled, 0.0).astype(BF16)


def _head_spread():
    ch = jnp.arange(SSD_D_INNER) // SSD_HEAD_DIM
    return (jnp.arange(LANES)[:, None] == ch[None, :]).astype(BF16)


def _mixer0_consts(p):
    ssd = [p['conv_w'][:, :XBC_DIM], p['conv_b'][:, :XBC_DIM], p['small_bias'], p['ssd_a_log'], p['ssd_d'],
           p['ssd_norm'], p['segt']]
    ml = [p['conv_w'][:, XBC_DIM:], p['conv_b'][:, XBC_DIM:], p['small_bias'], p['ml_wq'], p['ml_wk'],
          p['ml_wv'], p['ml_norm'], p['ml_skip']]
    return ssd, ml


def _mixer0_chunked(xbc, xm, z, o_pre, small, conv, ssm, mc, mn, mm, p, bsz, t):
    seq = lambda a: a.reshape(bsz, t, -1)
    tail = jnp.pad(conv, ((0, 0), (SUBLANES - (CONV_W - 1), 0), (0, 0)))
    ssd_c, ml_c = _mixer0_consts(p)
    y_ssd, new_ssm = _scan_call(
        _ssd_kernel, [seq(xbc), seq(z), seq(small)],
        [tail[:, :, :XBC_DIM], ssm.reshape(bsz, SSD_D_INNER, SSD_D_STATE)], ssd_c,
        [SSD_D_INNER], [(bsz, SSD_D_INNER, SSD_D_STATE)],
        [((SUBLANES, XBC_DIM), F32), ((SSD_D_INNER, SSD_D_STATE), F32)])
    hm, new_c, new_n, new_m = _scan_call(
        _mlstm_kernel, [seq(xm), seq(o_pre), seq(small)],
        [tail[:, :, XBC_DIM:], mc.reshape(bsz, MLSTM_D_INNER, MLSTM_HEAD_DIM), mn.reshape(bsz, 1, MLSTM_D_INNER),
         _pad_cols(mm, LANES).reshape(bsz, 1, LANES)], ml_c,
        [MLSTM_D_INNER], [(bsz, MLSTM_D_INNER, MLSTM_HEAD_DIM), (bsz, 1, MLSTM_D_INNER), (bsz, 1, LANES)],
        [((SUBLANES, MLSTM_D_INNER), F32), ((MLSTM_D_INNER, MLSTM_HEAD_DIM), F32),
         ((1, MLSTM_D_INNER), F32), ((1, LANES), F32)])
    flat = lambda a: a.reshape(bsz * t, -1)
    return flat(y_ssd), flat(hm), new_ssm, new_c, new_n, new_m.reshape(bsz, LANES)


def _mixer0_step(xbc, xm, z, o_pre, small, conv, ssm, mc, mn, mm, p, bsz):
    ssd_c, ml_c = _mixer0_consts(p)
    tails = [conv[:, j, :] for j in range(CONV_W - 1)]
    y_ssd, new_ssm = _step_call(
        _ssd_step_kernel, [xbc] + [tl[:, :XBC_DIM] for tl in tails] + [z, small],
        [ssm.reshape(bsz, SSD_D_INNER, SSD_D_STATE)], ssd_c, [SSD_D_INNER], [(bsz, SSD_D_INNER, SSD_D_STATE)])
    hm, new_n, new_m, new_c = _step_call(
        _mlstm_step_kernel, [xm] + [tl[:, XBC_DIM:] for tl in tails]
        + [o_pre, small, mn.reshape(bsz, MLSTM_D_INNER), _pad_cols(mm, LANES)],
        [mc.reshape(bsz, MLSTM_D_INNER, MLSTM_HEAD_DIM)], ml_c,
        [MLSTM_D_INNER, MLSTM_D_INNER, LANES], [(bsz, MLSTM_D_INNER, MLSTM_HEAD_DIM)])
    return y_ssd, hm, new_ssm, new_c, new_n, new_m


def _trunk(x, conv, ssm, mc, mn, mm, shift, wkv, p):
    bsz, t, d = x.shape
    m = bsz * t
    step = t == 1
    assert step or t % SCAN_CHUNK == 0, "a group is either single-token or a multiple of the scan chunk"
    assert not step or bsz % STEP_SEQS == 0
    act = F32 if step else BF16
    x2 = x.reshape(m, d)

    z, xbc, xm, o_pre, small = _rowwise_call(
        _in0_kernel, [x2], [p['norm_mix0'], p['w_in0']],
        [(SSD_D_INNER, act), (XBC_DIM, act), (MLSTM_D_INNER, act), (MLSTM_D_INNER, act), (LANES, F32)], m,
        tile=ROW_TILE)
    last_rows = lambda a: a.reshape(bsz, t, -1)[:, -(CONV_W - 1):].astype(F32)
    conv_in_tail = jnp.concatenate([last_rows(xbc), last_rows(xm)], axis=-1)
    new_conv = jnp.concatenate([conv, conv_in_tail], axis=1)[:, -(CONV_W - 1):]
    if step:
        y_ssd, hm, new_ssm, new_c, new_n, new_m = _mixer0_step(xbc, xm, z, o_pre, small, conv, ssm, mc, mn, mm, p, bsz)
    else:
        y_ssd, hm, new_ssm, new_c, new_n, new_m = _mixer0_chunked(xbc, xm, z, o_pre, small, conv, ssm, mc, mn, mm,
                                                                  p, bsz, t)
    xn1, x2b = _rowwise_call(
        _mix0_ffn_kernel, [y_ssd, hm, x2],
        [p['w_out0'], p['norm_ffn0'], (p['ffn_gu'], 0), (p['ffn_d'], 0), p['norm_mix1']], [(d, act), (d, F32)], m,
        tile=ROW_TILE)

    rw_in_c = [p['rw_mu'], p['rw_wr'], p['rw_wk'], p['rw_wv'], p['rw_w1'], p['rw_w2'], p['rw_a1'], p['rw_a2'],
               p['rw_g1'], p['rw_g2'], p['rw_w0'], p['rw_a0'], p['rw_k_k'], p['rw_k_a']]
    rw_outs = [(d, act), (d, F32)] + [(d, act)] * 5
    if step:
        rw = _rowwise_call(_rw_in_kernel, [xn1, shift], rw_in_c, rw_outs, m)
    else:
        tm = _row_tile(t, ROW_TILE)
        tiles_per_seq = t // tm
        prev_spec = pl.BlockSpec((PREV_ROWS, d), lambda i: (jnp.maximum(i * (tm // PREV_ROWS) - 1, 0), 0))
        shift_spec = pl.BlockSpec((None, 1, d), lambda i: (i // tiles_per_seq, 0, 0))
        rw = _rowwise_call(functools.partial(_rw_in_shift_kernel, tiles_per_seq=tiles_per_seq), [xn1], rw_in_c,
                           rw_outs, m, tile=tm, extra_ins=[(xn1, prev_spec), (shift.reshape(bsz, 1, d), shift_spec)])
    g = rw[6]
    rw_c = [p['rw_r_k'], p['rw_ln_w'], p['rw_ln_b']]
    if step:
        y_rw, new_wkv = _rwkv_step_call(list(rw[:6]), wkv, *rw_c)
    else:
        wkv3 = wkv.reshape(bsz, RWKV_D, RWKV_HEAD_DIM)
        y_rw, new_wkv = _scan_call(
            _rwkv_kernel, [a.reshape(bsz, t, d) for a in rw[:6]], [wkv3], rw_c,
            [d], [wkv3.shape], [((RWKV_D, 2 * RWKV_HEAD_DIM), F32)])
        y_rw = y_rw.reshape(m, d)
    new_wkv = new_wkv.reshape(wkv.shape)
    (y,) = _rowwise_call(
        _mix1_ffn_kernel, [y_rw, g, x2b],
        [p['rw_wo'], p['norm_ffn1'], (p['ffn_gu'], 1), (p['ffn_d'], 1), p['norm_final']], [(d, F32)], m, tile=ROW_TILE)

    states = (new_conv, new_ssm.reshape(ssm.shape), new_c.reshape(mc.shape), new_n.reshape(mn.shape),
              new_m[:, :MLSTM_HEADS], xn1.reshape(bsz, t, d)[:, -1].astype(F32), new_wkv)
    return y.reshape(bsz, t, d), states


def kernel(x_prompt, x_sample, state_conv, state_ssm, state_mlstm_c, state_mlstm_n, state_mlstm_m, state_shift, state_wkv, norm_mix, norm_ffn, norm_final, w_in0, conv_w, conv_b, ssd_dt_bias, ssd_a_log, ssd_d, ssd_norm, ml_wq, ml_wk, ml_wv, ml_i_bias, ml_f_bias, ml_norm, ml_skip, w_out0, rw_mu, rw_wr, rw_wk, rw_wv, rw_wo, rw_w0, rw_w1, rw_w2, rw_a0, rw_a1, rw_a2, rw_g1, rw_g2, rw_k_k, rw_k_a, rw_r_k, rw_ln_w, rw_ln_b, ffn_w_gate_up, ffn_w_down):
    assert norm_mix.shape[0] == 2 and w_in0.shape[0] == 1 and rw_wr.shape[0] == 1, "two layers: SSD|mLSTM then RWKV-7"
    s1 = SSD_D_INNER
    s2 = s1 + XBC_DIM + MLSTM_D_INNER
    s3 = s2 + SSD_HEADS
    s4 = s3 + MLSTM_D_INNER
    s5 = s4 + MLSTM_HEADS
    w0 = w_in0[0]
    small_w = _pad_cols(jnp.concatenate([w0[:, s2:s3], w0[:, s4:s5], w0[:, s5:]], axis=1), LANES)
    lora = lambda w1, w2, width: (_pad_cols(w1, width).astype(BF16), _pad_rows_to(w2, width).astype(BF16))
    rw_w1p, rw_w2p = lora(rw_w1[0], rw_w2[0], LANES)
    rw_a1p, rw_a2p = lora(rw_a1[0], rw_a2[0], LANES)
    rw_g1p, rw_g2p = lora(rw_g1[0], rw_g2[0], 2 * LANES)
    p = dict(
        norm_mix0=_row(norm_mix[0]), norm_mix1=_row(norm_mix[1]), norm_ffn0=_row(norm_ffn[0]),
        norm_ffn1=_row(norm_ffn[1]), norm_final=_row(norm_final),
        w_in0=jnp.concatenate([w0[:, :s2], w0[:, s3:s4], small_w], axis=1).astype(BF16),
        conv_w=conv_w[0], conv_b=_row(conv_b[0]),
        small_bias=_pad_cols(jnp.concatenate([_row(ssd_dt_bias[0]), _row(ml_i_bias[0]), _row(ml_f_bias[0])], axis=1),
                             LANES),
        ssd_a_log=_pad_cols(_row(ssd_a_log[0]), LANES), ssd_d=_row(jnp.repeat(ssd_d[0], SSD_HEAD_DIM)),
        ssd_norm=_row(ssd_norm[0]),
        ml_wq=_blockdiag_slabs(ml_wq[0]), ml_wk=_blockdiag_slabs(ml_wk[0], MLSTM_HEAD_DIM ** -0.5),
        ml_wv=_blockdiag_slabs(ml_wv[0]), ml_norm=_row(ml_norm[0]), ml_skip=_row(ml_skip[0]),
        w_out0=w_out0[0].astype(BF16),
        rw_mu=_pad_rows_to(rw_mu[0], SUBLANES),
        rw_wr=rw_wr[0].astype(BF16), rw_wk=rw_wk[0].astype(BF16), rw_wv=rw_wv[0].astype(BF16),
        rw_wo=rw_wo[0].astype(BF16), rw_w0=_row(rw_w0[0]), rw_a0=_row(rw_a0[0]),
        rw_w1=rw_w1p, rw_w2=rw_w2p, rw_a1=rw_a1p, rw_a2=rw_a2p, rw_g1=rw_g1p, rw_g2=rw_g2p,
        rw_k_k=_row(rw_k_k[0]), rw_k_a=_row(rw_k_a[0]), rw_r_k=_row(rw_r_k[0]),
        rw_ln_w=_row(rw_ln_w[0]), rw_ln_b=_row(rw_ln_b[0]),
        ffn_gu=ffn_w_gate_up.astype(BF16), ffn_d=ffn_w_down.astype(BF16),
        segt=_head_spread(),
    )
    bp = x_prompt.shape[0]
    zeros = lambda s: jnp.zeros((bp,) + s.shape[2:], F32)
    y_p, st_p = _trunk(x_prompt, zeros(state_conv), zeros(state_ssm), zeros(state_mlstm_c),
                       zeros(state_mlstm_n), zeros(state_mlstm_m), zeros(state_shift), zeros(state_wkv), p)
    y_s, st_s = _trunk(x_sample, state_conv[0], state_ssm[0], state_mlstm_c[0], state_mlstm_n[0],
                       state_mlstm_m[0], state_shift[0], state_wkv[0], p)
    out = [y_p, y_s]
    for a, b in zip(st_p, st_s):
        out += [a[None], b[None]]
    return tuple(out)
```

```python
import functools

import jax
import jax.numpy as jnp
from jax import lax
from jax.experimental import pallas as pl
from jax.experimental.pallas import tpu as pltpu

F32 = jnp.float32
BF16 = jnp.bfloat16

D_MODEL = 1024
NORM_EPS = 1e-5
CONV_W = 4
SSD_HEADS = 16
SSD_HEAD_DIM = 64
SSD_GROUPS = 2
SSD_D_STATE = 128
SSD_D_INNER = SSD_HEADS * SSD_HEAD_DIM
SSD_GROUP_W = SSD_D_INNER // SSD_GROUPS
XBC_DIM = SSD_D_INNER + 2 * SSD_GROUPS * SSD_D_STATE
MLSTM_HEADS = 4
MLSTM_HEAD_DIM = 256
MLSTM_D_INNER = MLSTM_HEADS * MLSTM_HEAD_DIM
QKV_BLOCK = 4
RWKV_HEADS = 16
RWKV_HEAD_DIM = 64
RWKV_D = RWKV_HEADS * RWKV_HEAD_DIM
RWKV_LN_EPS = 64e-5
D_FF = 2816
FF_CHUNK = 256
LANES = 128
SUBLANES = 8
SMALL_DT, SMALL_I, SMALL_F = 0, 16, 20
NEG = -1e30
EXP_NEG_HALF = 0.6065306597126334
SCAN_CHUNK = 128
STEP_SEQS = SUBLANES
ROW_TILE = 512
PREV_ROWS = 16
VMEM_LIMIT = 56 * 1024 * 1024


def _dot(a, b):
    return jnp.dot(a.astype(BF16), b.astype(BF16), preferred_element_type=F32)


def _dot_nt(a, b):
    return lax.dot_general(a.astype(BF16), b.astype(BF16), (((1,), (1,)), ((), ())),
                           preferred_element_type=F32)


def _dot_tn(a, b):
    return lax.dot_general(a.astype(BF16), b.astype(BF16), (((0,), (0,)), ((), ())),
                           preferred_element_type=F32)


def _split(x, terms):
    parts = []
    rem = x
    for _ in range(terms):
        p = rem.astype(BF16)
        parts.append(p)
        rem = rem - p.astype(F32)
    return parts


def _dot_exact_rhs(x, m, terms=2):
    out = None
    for p in _split(x, terms):
        t = jnp.dot(p, m, preferred_element_type=F32)
        out = t if out is None else out + t
    return out


def _tri_cumsum(tri, x, terms=3):
    out = None
    for p in _split(x, terms):
        t = jnp.dot(tri, p, preferred_element_type=F32)
        out = t if out is None else out + t
    return out


def _rms_rows(x, g):
    return x * lax.rsqrt(jnp.mean(x * x, axis=-1, keepdims=True) + NORM_EPS) * g


def _group_rms(x, width):
    outs = []
    for lo in range(0, x.shape[1], width):
        grp = x[:, lo:lo + width]
        outs.append(grp * lax.rsqrt(jnp.mean(grp * grp, axis=-1, keepdims=True) + NORM_EPS))
    return jnp.concatenate(outs, axis=1)


def _sigmoid(x):
    return 0.5 * jnp.tanh(0.5 * x) + 0.5


def _silu(x):
    h = 0.5 * x
    return h * jnp.tanh(h) + h


def _softplus(x):
    return jnp.maximum(x, 0.0) + jnp.log(1.0 + jnp.exp(-jnp.abs(x)))


def _log_sigmoid(x):
    return jnp.minimum(x, 0.0) - jnp.log(1.0 + jnp.exp(-jnp.abs(x)))


def _tri_masks(n):
    row = lax.broadcasted_iota(jnp.int32, (n, n), 0)
    col = lax.broadcasted_iota(jnp.int32, (n, n), 1)
    return row >= col, row > col


def _chunk_conv(tail, u_ref, conv0_ref, cw_ref, cb_ref, first):
    assert u_ref.dtype == BF16
    rows = u_ref.shape[0]
    taps = CONV_W - 1

    @pl.when(first)
    def _():
        tail[...] = conv0_ref[...]

    ub = u_ref[...]
    u = ub.astype(F32)
    out_row = lax.broadcasted_iota(jnp.int32, (taps * rows, rows), 0)
    src_row = lax.broadcasted_iota(jnp.int32, (taps * rows, rows), 1)
    back = taps - out_row // rows
    shift = jnp.where(src_row == out_row % rows - back, 1.0, 0.0).astype(BF16)
    shifted = jnp.dot(shift, ub, preferred_element_type=F32)
    conv = cb_ref[...] + u * cw_ref[taps:CONV_W, :]
    window = jnp.concatenate([tail[...], jnp.zeros((SUBLANES, u.shape[1]), F32)], axis=0)
    edge = jnp.zeros((SUBLANES, u.shape[1]), F32)
    for j in range(taps):
        conv = conv + shifted[j * rows:(j + 1) * rows, :] * cw_ref[j:j + 1, :]
        edge = edge + pltpu.roll(window, shift=taps - j, axis=0)[SUBLANES:2 * SUBLANES, :] * cw_ref[j:j + 1, :]
    tail[...] = u[rows - SUBLANES:rows, :]
    return jnp.concatenate([conv[0:SUBLANES, :] + edge, conv[SUBLANES:rows, :]], axis=0), u


def _step_conv(u_ref, tail_refs, cw_ref, cb_ref):
    conv = cb_ref[...] + u_ref[...] * cw_ref[CONV_W - 1:CONV_W, :]
    for j, t in enumerate(tail_refs):
        conv = conv + t[...] * cw_ref[j:j + 1, :]
    return conv


def _blockdiag_qkv(act, raw, wq_ref, wk_ref, wv_ref):
    hd = MLSTM_HEAD_DIM
    cat = lambda x, w: jnp.concatenate(
        [_dot(x[:, s * hd:(s + 1) * hd], w[s]) for s in range(MLSTM_HEADS)], axis=1)
    return cat(act, wq_ref), cat(act, wk_ref), cat(raw, wv_ref)


def _head_sum(x):
    half = lax.broadcasted_iota(jnp.int32, (x.shape[0], LANES), 1) < RWKV_HEAD_DIM
    outs = []
    for lo in range(0, x.shape[1], LANES):
        xp = x[:, lo:lo + LANES]
        s0 = jnp.sum(jnp.where(half, xp, 0.0), axis=1, keepdims=True)
        s1 = jnp.sum(jnp.where(half, 0.0, xp), axis=1, keepdims=True)
        outs.append(jnp.where(half, s0, s1))
    return jnp.concatenate(outs, axis=1)


def _rwkv_head_norm(y, r, k, v, rk_ref, lnw_ref, lnb_ref):
    inv_n = 1.0 / RWKV_HEAD_DIM
    dlt = y - _head_sum(y) * inv_n
    yn = dlt * lax.rsqrt(_head_sum(dlt * dlt) * inv_n + RWKV_LN_EPS)
    return yn * lnw_ref[...] + lnb_ref[...] + _head_sum(r * k * rk_ref[...]) * v


def _in0_kernel(x_ref, g_ref, wt_ref, z_ref, xbc_ref, xm_ref, o_ref, sm_ref):
    xn = _rms_rows(x_ref[...], g_ref[...]).astype(BF16)
    off = 0
    for ref in (z_ref, xbc_ref, xm_ref, o_ref, sm_ref):
        n = ref.shape[-1]
        ref[...] = lax.dot_general(xn, wt_ref[off:off + n, :], (((1,), (1,)), ((), ())),
                                   preferred_element_type=F32).astype(ref.dtype)
        off += n


def _swiglu_residual(x, g_ref, wgu_ref, wd_ref):
    xn = _rms_rows(x, g_ref[...]).astype(BF16)
    acc = None
    for c in range(D_FF // FF_CHUNK):
        lo = c * FF_CHUNK
        gate = jnp.dot(xn, wgu_ref[:, lo:lo + FF_CHUNK], preferred_element_type=F32)
        up = jnp.dot(xn, wgu_ref[:, D_FF + lo:D_FF + lo + FF_CHUNK], preferred_element_type=F32)
        part = _dot(_silu(gate) * up, wd_ref[lo:lo + FF_CHUNK, :])
        acc = part if acc is None else acc + part
    return x + acc


def _mix0_ffn_kernel(y_ref, h_ref, x_ref, wo_ref, g_ref, wgu_ref, wd_ref, pg_ref, xn_ref, xo_ref):
    mixed = jnp.concatenate([y_ref[...].astype(BF16), h_ref[...].astype(BF16)], axis=1)
    x1 = x_ref[...] + jnp.dot(mixed, wo_ref[...], preferred_element_type=F32)
    x2 = _swiglu_residual(x1, g_ref, wgu_ref, wd_ref)
    xo_ref[...] = x2
    xn_ref[...] = _rms_rows(x2, pg_ref[...]).astype(xn_ref.dtype)


def _mix1_ffn_kernel(y_ref, gate_ref, x_ref, wo_ref, g_ref, wgu_ref, wd_ref, pg_ref, o_ref):
    x3 = x_ref[...] + _dot(y_ref[...].astype(F32) * gate_ref[...].astype(F32), wo_ref[...])
    o_ref[...] = _rms_rows(_swiglu_residual(x3, g_ref, wgu_ref, wd_ref), pg_ref[...])


def _rw_in_body(xn, xp, mu_ref, wr_ref, wk_ref, wv_ref, w1_ref, w2_ref, a1_ref, a2_ref,
                g1_ref, g2_ref, w0_ref, a0_ref, kk_w_ref, ka_w_ref,
                r_ref, lw_ref, k_ref, v_ref, kk_ref, a_ref, g_ref):
    xx = xp - xn
    xr, xw, xk, xv, xa, xg = (xn + xx * mu_ref[c:c + 1, :] for c in range(6))
    r_ref[...] = _dot(xr, wr_ref[...]).astype(r_ref.dtype)
    k = _dot(xk, wk_ref[...])
    v_ref[...] = _dot(xv, wv_ref[...]).astype(v_ref.dtype)
    z = w0_ref[...] + _dot(jnp.tanh(_dot(xw, w1_ref[...])), w2_ref[...])
    lw_ref[...] = -EXP_NEG_HALF * _sigmoid(z)
    a = _sigmoid(a0_ref[...] + _dot(_dot(xa, a1_ref[...]), a2_ref[...]))
    a_ref[...] = a.astype(a_ref.dtype)
    g_ref[...] = _dot(_sigmoid(_dot(xg, g1_ref[...])), g2_ref[...]).astype(g_ref.dtype)
    kk = k * kk_w_ref[...]
    ss = _head_sum(kk * kk)
    kk_ref[...] = (kk * lax.rsqrt(jnp.maximum(ss, 1e-24))).astype(kk_ref.dtype)
    k_ref[...] = (k * (1.0 + (a - 1.0) * ka_w_ref[...])).astype(k_ref.dtype)


def _rw_in_kernel(xn_ref, xp_ref, *refs):
    _rw_in_body(xn_ref[...].astype(F32), xp_ref[...].astype(F32), *refs)


def _rw_in_shift_kernel(xn_ref, prev_ref, shift_ref, *refs, tiles_per_seq):
    xn = xn_ref[...].astype(F32)
    starts_seq = pl.program_id(0) % tiles_per_seq == 0
    last = prev_ref.shape[0] - 1
    prev = jnp.where(starts_seq, shift_ref[...], prev_ref[last:last + 1, :].astype(F32))
    row = lax.broadcasted_iota(jnp.int32, xn.shape, 0)
    xp = jnp.where(row == 0, prev, pltpu.roll(xn, shift=1, axis=0))
    _rw_in_body(xn, xp, *refs)


def _ssd_kernel(xbc_ref, z_ref, sm_ref, conv0_ref, st0_ref, cw_ref, cb_ref, bias_ref, alog_ref,
                dexp_ref, nw_ref, segt_ref, y_ref, st_ref, ext, state):
    c = pl.program_id(1)
    first = c == 0
    rows = xbc_ref.shape[0]

    @pl.when(first)
    def _():
        state[...] = st0_ref[...]

    conv, _ = _chunk_conv(ext, xbc_ref, conv0_ref, cw_ref, cb_ref, first)
    act = _silu(conv)
    xs = act[:, 0:SSD_D_INNER]
    gn = SSD_GROUPS * SSD_D_STATE
    bm = act[:, SSD_D_INNER:SSD_D_INNER + gn]
    cm = act[:, SSD_D_INNER + gn:SSD_D_INNER + 2 * gn]

    tril, _ = _tri_masks(rows)
    tri = jnp.where(tril, 1.0, 0.0).astype(BF16)
    lane = lax.broadcasted_iota(jnp.int32, (rows, LANES), 1)
    head_lane = (lane >= SMALL_DT) & (lane < SMALL_DT + SSD_HEADS)
    dt = jnp.where(head_lane, _softplus(sm_ref[...] + bias_ref[...]), 0.0)
    acum = _tri_cumsum(tri, dt * (-jnp.exp(alog_ref[...])))
    acum_t = acum.T
    dt_t = dt.T
    segt = segt_ref[...]
    eac_x = _dot_exact_rhs(jnp.exp(acum), segt, terms=1)
    wend_x = _dot_exact_rhs(jnp.exp(acum[rows - 1:rows, :] - acum) * dt, segt, terms=1)
    xw = xs * wend_x

    half = lane < SSD_HEAD_DIM
    heads_per_group = SSD_HEADS // SSD_GROUPS
    grp = lambda x, g: x[:, g * SSD_D_STATE:(g + 1) * SSD_D_STATE]
    cbs = [_dot_nt(grp(cm, g), grp(bm, g)) for g in range(SSD_GROUPS)]
    decays = [jnp.exp(jnp.where(tril, acum[:, j:j + 1] - acum_t[j:j + 1, :], NEG)) for j in range(SSD_HEADS)]
    wts = [decays[j] * cbs[j // heads_per_group] * dt_t[j:j + 1, :] for j in range(SSD_HEADS)]
    lhs = [jnp.concatenate([wts[2 * p], wts[2 * p + 1]], axis=1).astype(BF16) for p in range(SSD_HEADS // 2)]
    rhs = []
    for p in range(SSD_HEADS // 2):
        xp = xs[:, p * LANES:(p + 1) * LANES]
        rhs.append(jnp.concatenate([jnp.where(half, xp, 0.0), jnp.where(half, 0.0, xp)], axis=0).astype(BF16))
    pairs = [_dot(l, r) for l, r in zip(lhs, rhs)]
    y = jnp.concatenate(pairs, axis=1)

    st = state[...]
    gw = SSD_GROUP_W
    y_state = jnp.concatenate(
        [_dot_nt(cm[:, g * SSD_D_STATE:(g + 1) * SSD_D_STATE], st[g * gw:(g + 1) * gw, :])
         for g in range(SSD_GROUPS)], axis=1)
    y = y + eac_x * y_state + dexp_ref[...] * xs
    y_ref[...] = (_group_rms(y * _silu(z_ref[...].astype(F32)), gw) * nw_ref[...]).astype(y_ref.dtype)

    dcol = jnp.exp(acum_t[:, rows - 1:rows])
    for g in range(SSD_GROUPS):
        upd = _dot_tn(xw[:, g * gw:(g + 1) * gw], bm[:, g * SSD_D_STATE:(g + 1) * SSD_D_STATE])
        for h in range(heads_per_group):
            j = g * heads_per_group + h
            lo = j * SSD_HEAD_DIM
            state[lo:lo + SSD_HEAD_DIM, :] = (st[lo:lo + SSD_HEAD_DIM, :] * dcol[j:j + 1, :]
                                              + upd[h * SSD_HEAD_DIM:(h + 1) * SSD_HEAD_DIM, :])

    @pl.when(c == pl.num_programs(1) - 1)
    def _():
        st_ref[...] = state[...]


def _mlstm_kernel(xm_ref, o_ref, sm_ref, conv0_ref, c0_ref, n0_ref, m0_ref, cw_ref, cb_ref, bias_ref,
                  wq_ref, wk_ref, wv_ref, nw_ref, skip_ref, h_ref, c_ref, n_ref, m_ref,
                  ext, cst, nst, mst):
    c = pl.program_id(1)
    first = c == 0
    rows = xm_ref.shape[0]

    @pl.when(first)
    def _():
        cst[...] = c0_ref[...]
        nst[...] = n0_ref[...]
        mst[...] = m0_ref[...]

    conv, raw = _chunk_conv(ext, xm_ref, conv0_ref, cw_ref, cb_ref, first)
    act = _silu(conv)
    q, k, v = _blockdiag_qkv(act, raw, wq_ref, wk_ref, wv_ref)

    tril, _ = _tri_masks(rows)
    tri = jnp.where(tril, 1.0, 0.0).astype(BF16)
    logi = sm_ref[...] + bias_ref[...]
    bcum = _tri_cumsum(tri, _log_sigmoid(logi))
    bcum_t = bcum.T
    logi_t = logi.T
    m_all = mst[...]
    lane1 = lax.broadcasted_iota(jnp.int32, (1, LANES), 1)
    hd = MLSTM_HEAD_DIM

    heads = range(MLSTM_HEADS)
    cols = [slice(h * hd, (h + 1) * hd) for h in heads]
    qs, ks, vs = ([t[:, c] for c in cols] for t in (q, k, v))
    qbs, vbs = [x.astype(BF16) for x in qs], [x.astype(BF16) for x in vs]
    bcs = [bcum[:, SMALL_F + h:SMALL_F + h + 1] for h in heads]
    lis = [logi[:, SMALL_I + h:SMALL_I + h + 1] for h in heads]
    m_hs = [m_all[:, h:h + 1] for h in heads]
    c_hs = [cst[c, :] for c in cols]
    n_hs = [nst[:, c] for c in cols]
    dlogs = [jnp.where(tril, bcs[h] - bcum_t[SMALL_F + h:SMALL_F + h + 1, :]
                       + logi_t[SMALL_I + h:SMALL_I + h + 1, :], NEG) for h in heads]
    inters = [bcs[h] + m_hs[h] for h in heads]
    m_ts = [jnp.maximum(inters[h], jnp.max(dlogs[h], axis=1, keepdims=True)) for h in heads]
    qks = [_dot_nt(qbs[h], ks[h]) for h in heads]
    qcs = [_dot(qbs[h], c_hs[h]) for h in heads]
    ss = [qks[h] * jnp.exp(dlogs[h] - m_ts[h]) for h in heads]
    w_inters = [jnp.exp(inters[h] - m_ts[h]) for h in heads]
    nums = [_dot(ss[h], vbs[h]) + w_inters[h] * qcs[h] for h in heads]
    dens = [jnp.sum(ss[h], axis=1, keepdims=True)
            + w_inters[h] * jnp.sum(qs[h] * n_hs[h], axis=1, keepdims=True) for h in heads]
    hs = [nums[h] * (1.0 / jnp.maximum(jnp.abs(dens[h]), jnp.exp(-m_ts[h]))) for h in heads]
    b_ends = [bcs[h][rows - 1:rows, :] for h in heads]
    wlogs = [b_ends[h] - bcs[h] + lis[h] for h in heads]
    m_news = [jnp.maximum(b_ends[h] + m_hs[h], jnp.max(wlogs[h], axis=0, keepdims=True)) for h in heads]
    dcs = [jnp.exp(b_ends[h] + m_hs[h] - m_news[h]) for h in heads]
    kwss = [ks[h] * jnp.exp(wlogs[h] - m_news[h]) for h in heads]
    upds = [_dot_tn(kwss[h], vbs[h]) for h in heads]
    for h in heads:
        cst[cols[h], :] = dcs[h] * c_hs[h] + upds[h]
        nst[:, cols[h]] = dcs[h] * n_hs[h] + jnp.sum(kwss[h], axis=0, keepdims=True)
        m_all = jnp.where(lane1 == h, m_news[h], m_all)
    mst[...] = m_all

    hm = _group_rms(jnp.concatenate(hs, axis=1), hd) * nw_ref[...]
    h_ref[...] = ((hm + skip_ref[...] * act) * _sigmoid(o_ref[...].astype(F32))).astype(h_ref.dtype)

    @pl.when(c == pl.num_programs(1) - 1)
    def _():
        c_ref[...] = cst[...]
        n_ref[...] = nst[...]
        m_ref[...] = mst[...]


def _rwkv_kernel(r_ref, lw_ref, k_ref, v_ref, kk_ref, a_ref, s0_ref, rk_ref, lnw_ref, lnb_ref,
                 y_ref, s_ref, sblk):
    c = pl.program_id(1)
    rows = r_ref.shape[0]

    @pl.when(c == 0)
    def _():
        s0 = s0_ref[...]
        shape = (RWKV_D, 2 * RWKV_HEAD_DIM)
        head_parity = (lax.broadcasted_iota(jnp.int32, shape, 0) // RWKV_HEAD_DIM) % 2
        lane_half = lax.broadcasted_iota(jnp.int32, shape, 1) // RWKV_HEAD_DIM
        sblk[...] = jnp.where(head_parity == lane_half, jnp.concatenate([s0, s0], axis=1), 0.0)

    r, lw, k, v, kk, a = (t[...].astype(F32) for t in (r_ref, lw_ref, k_ref, v_ref, kk_ref, a_ref))
    tril, strict = _tri_masks(rows)
    tri = jnp.where(tril, 1.0, 0.0).astype(BF16)
    cum = _tri_cumsum(tri, lw, terms=2)
    last = cum[rows - 1:rows, :]
    einv = jnp.exp(-cum)
    eend = jnp.exp(last - cum)
    ka = kk * a
    at = -(kk * jnp.exp(cum - lw))
    rt = r * jnp.exp(cum)
    bt = ka * einv
    kt = k * einv
    bh = ka * eend
    kh = k * eend
    wl = jnp.exp(last)
    half = lax.broadcasted_iota(jnp.int32, (rows, LANES), 1) < RWKV_HEAD_DIM
    steps = rows.bit_length() - 1

    def two(x):
        return [jnp.where(half, x, 0.0), jnp.where(half, 0.0, x)]

    npairs = RWKV_HEADS // 2
    tril4 = jnp.concatenate([tril] * 4, axis=1)
    strict4 = jnp.concatenate([strict] * 4, axis=1)
    lanes = [slice(p * LANES, (p + 1) * LANES) for p in range(npairs)]

    a2s = [jnp.concatenate([at[:, sl], rt[:, sl]], axis=0).astype(BF16) for sl in lanes]
    rbs = [jnp.concatenate(two(bt[:, sl]) + two(kt[:, sl]), axis=0).astype(BF16) for sl in lanes]
    sps = [sblk[sl, :] for sl in lanes]
    vss = [two(v[:, sl]) for sl in lanes]
    pbks = [_dot_nt(a2, rb) for a2, rb in zip(a2s, rbs)]
    xss = [_dot_nt(a2, sp) for a2, sp in zip(a2s, sps)]
    tops = [jnp.where(strict4, pbk[0:rows, :], 0.0) for pbk in pbks]
    us = [xs[0:rows, :] + _dot(top[:, 2 * rows:4 * rows], jnp.concatenate(vs, axis=0))
          for xs, top, vs in zip(xss, tops, vss)]
    pws = [[top[:, e * rows:(e + 1) * rows].astype(BF16) for e in range(2)] for top in tops]
    ybase = [xs[rows:2 * rows, :] for xs in xss]
    w4s = [jnp.where(tril4, pbk[rows:2 * rows, :], 0.0).astype(BF16) for pbk in pbks]

    for j in range(steps):
        for p in range(npairs):
            u2 = jnp.concatenate(two(us[p]), axis=0)
            us[p] = us[p] + _dot(jnp.concatenate(pws[p], axis=1), u2)
        if j < steps - 1:
            for p in range(npairs):
                pws[p] = [_dot(pw, pw).astype(BF16) for pw in pws[p]]

    ys = []
    for p, sl in enumerate(lanes):
        r4 = jnp.concatenate(two(us[p]) + vss[p], axis=0).astype(BF16)
        ys.append(ybase[p] + _dot(w4s[p], r4))
        rh = jnp.concatenate(two(bh[:, sl]) + two(kh[:, sl]), axis=0)
        sblk[sl, :] = sps[p] * wl[:, sl] + _dot_tn(r4, rh)
    y = jnp.concatenate(ys, axis=1)
    y_ref[...] = _rwkv_head_norm(y, r, k, v, rk_ref, lnw_ref, lnb_ref).astype(y_ref.dtype)

    @pl.when(c == pl.num_programs(1) - 1)
    def _():
        sb = sblk[...]
        s_ref[...] = sb[:, 0:RWKV_HEAD_DIM] + sb[:, RWKV_HEAD_DIM:2 * RWKV_HEAD_DIM]


def _ssd_step_kernel(xbc_ref, t0_ref, t1_ref, t2_ref, z_ref, sm_ref, st_ref, cw_ref, cb_ref, bias_ref,
                     alog_ref, dexp_ref, nw_ref, segt_ref, y_ref, sto_ref):
    nseq = xbc_ref.shape[0]
    act = _silu(_step_conv(xbc_ref, (t0_ref, t1_ref, t2_ref), cw_ref, cb_ref))
    xs = act[:, 0:SSD_D_INNER]
    gn = SSD_GROUPS * SSD_D_STATE
    bm = act[:, SSD_D_INNER:SSD_D_INNER + gn]
    cm = act[:, SSD_D_INNER + gn:SSD_D_INNER + 2 * gn]
    lane = lax.broadcasted_iota(jnp.int32, (nseq, LANES), 1)
    head_lane = (lane >= SMALL_DT) & (lane < SMALL_DT + SSD_HEADS)
    dt = jnp.where(head_lane, _softplus(sm_ref[...] + bias_ref[...]), 0.0)
    segt = segt_ref[...]
    dec_t = _dot_exact_rhs(jnp.exp(dt * (-jnp.exp(alog_ref[...]))), segt).T
    xdt = xs * _dot_exact_rhs(dt, segt)
    gw = SSD_GROUP_W
    rowi = lax.broadcasted_iota(jnp.int32, (nseq, gw), 0)
    ys = [jnp.zeros((nseq, gw), F32) for _ in range(SSD_GROUPS)]
    for s in range(nseq):
        for g in range(SSD_GROUPS):
            rs = slice(g * gw, (g + 1) * gw)
            ns = slice(g * SSD_D_STATE, (g + 1) * SSD_D_STATE)
            outer = _dot_tn(jnp.where(rowi == s, xdt[:, rs], 0.0), bm[:, ns])
            new = st_ref[s, rs, :] * dec_t[rs, s:s + 1] + outer
            sto_ref[s, rs, :] = new
            ys[g] = jnp.where(rowi == s, _dot_nt(cm[:, ns], new), ys[g])
    y = jnp.concatenate(ys, axis=1) + dexp_ref[...] * xs
    y_ref[...] = _group_rms(y * _silu(z_ref[...]), gw) * nw_ref[...]


def _mlstm_step_kernel(xm_ref, t0_ref, t1_ref, t2_ref, o_ref, sm_ref, n_ref, m_ref, c_ref, cw_ref, cb_ref,
                       bias_ref, wq_ref, wk_ref, wv_ref, nw_ref, skip_ref, h_ref, no_ref, mo_ref, co_ref):
    nseq = xm_ref.shape[0]
    raw = xm_ref[...]
    act = _silu(_step_conv(xm_ref, (t0_ref, t1_ref, t2_ref), cw_ref, cb_ref))
    q, k, v = _blockdiag_qkv(act, raw, wq_ref, wk_ref, wv_ref)
    logi = sm_ref[...] + bias_ref[...]
    logf = _log_sigmoid(logi)
    m_old = m_ref[...]
    m_all = m_old
    n_old = n_ref[...]
    lane = lax.broadcasted_iota(jnp.int32, (nseq, LANES), 1)
    hd = MLSTM_HEAD_DIM
    rowi = lax.broadcasted_iota(jnp.int32, (nseq, hd), 0)
    hs, ns = [], []
    for h in range(MLSTM_HEADS):
        cols = slice(h * hd, (h + 1) * hd)
        qh, kh, vh = q[:, cols], k[:, cols], v[:, cols]
        lf = logf[:, SMALL_F + h:SMALL_F + h + 1] + m_old[:, h:h + 1]
        li = logi[:, SMALL_I + h:SMALL_I + h + 1]
        m_new = jnp.maximum(lf, li)
        dc = jnp.exp(lf - m_new)
        kws = kh * jnp.exp(li - m_new)
        n_new = dc * n_old[:, cols] + kws
        den = jnp.sum(qh * n_new, axis=1, keepdims=True)
        q_t = qh.T
        num = jnp.zeros((nseq, hd), F32)
        for s in range(nseq):
            outer = _dot_tn(jnp.where(rowi == s, kws, 0.0), vh)
            c_new = dc[s:s + 1, :] * c_ref[s, cols, :] + outer
            co_ref[s, cols, :] = c_new
            num = jnp.where(rowi == s, jnp.sum(q_t[:, s:s + 1] * c_new, axis=0, keepdims=True), num)
        hs.append(num / jnp.maximum(jnp.abs(den), jnp.exp(-m_new)))
        ns.append(n_new)
        m_all = jnp.where(lane == h, m_new, m_all)
    no_ref[...] = jnp.concatenate(ns, axis=1)
    mo_ref[...] = m_all
    hm = _group_rms(jnp.concatenate(hs, axis=1), hd) * nw_ref[...]
    h_ref[...] = (hm + skip_ref[...] * act) * _sigmoid(o_ref[...])


def _rwkv_step_kernel(r_ref, lw_ref, k_ref, v_ref, kk_ref, a_ref, s_ref, rk_ref, lnw_ref, lnb_ref,
                      y_ref, so_ref, yt):
    r, k, v, kk = (t[...] for t in (r_ref, k_ref, v_ref, kk_ref))
    n = RWKV_HEAD_DIM
    w_t, kk_t, ka_t, k_t, r_t, v_t = (x.T for x in (jnp.exp(lw_ref[...]), kk, kk * a_ref[...], k, r, v))
    for e in range(2):
        hs = slice(e * n, (e + 1) * n)
        w_h, kk_h, ka_h, k_h, r_h = w_t[hs], kk_t[hs], ka_t[hs], k_t[hs], r_t[hs]
        for i in range(n):
            row = e * n + i
            st = s_ref[e, i]
            sa = jnp.sum(st * kk_h, axis=0, keepdims=True)
            new = st * w_h - sa * ka_h + v_t[row:row + 1, :] * k_h
            so_ref[e, i] = new
            yt[row:row + 1, :] = jnp.sum(new * r_h, axis=0, keepdims=True)
    y_ref[...] = _rwkv_head_norm(yt[...].T, r, k, v, rk_ref, lnw_ref, lnb_ref)


def _params(sem):
    return pltpu.CompilerParams(dimension_semantics=sem, vmem_limit_bytes=VMEM_LIMIT)


def _call_name(kernel, n):
    fn = getattr(kernel, "func", kernel)
    return f"{fn.__name__.strip('_')}_{n}"


def _const_spec(shape, grid_rank):
    nd = len(shape)
    if grid_rank == 1:
        return pl.BlockSpec(shape, lambda i: (0,) * nd, pipeline_mode=pl.Buffered(1))
    return pl.BlockSpec(shape, lambda b, c: (0,) * nd, pipeline_mode=pl.Buffered(1))


def _row_tile(m, tile):
    return tile if m % tile == 0 else m


def _layer_spec(stacked, layer):
    nd = stacked.ndim
    return pl.BlockSpec((None,) + stacked.shape[1:], lambda i: (layer,) + (0,) * (nd - 1),
                        pipeline_mode=pl.Buffered(1))


def _rowwise_call(kernel, row_ins, const_ins, outs, m, tile=ROW_TILE, extra_ins=()):
    tm = _row_tile(m, tile)
    rows = lambda n: pl.BlockSpec((tm, n), lambda i: (i, 0))
    consts = [c if isinstance(c, tuple) else (c, None) for c in const_ins]
    return pl.pallas_call(
        kernel,
        name=_call_name(kernel, m),
        grid=(m // tm,),
        in_specs=([rows(a.shape[1]) for a in row_ins] + [spec for _, spec in extra_ins]
                  + [_const_spec(a.shape, 1) if layer is None else _layer_spec(a, layer) for a, layer in consts]),
        out_specs=[rows(n) for n, _ in outs],
        out_shape=[jax.ShapeDtypeStruct((m, n), dt) for n, dt in outs],
        compiler_params=_params(("parallel",)),
    )(*row_ins, *[a for a, _ in extra_ins], *[a for a, _ in consts])


def _scan_call(kernel, seq_ins, state_ins, const_ins, seq_out_widths, state_out_shapes, scratch):
    bsz, t = seq_ins[0].shape[:2]
    seq = lambda n: pl.BlockSpec((None, SCAN_CHUNK, n), lambda b, c: (b, c, 0))
    per_seq = lambda shape: pl.BlockSpec((None,) + tuple(shape[1:]), lambda b, c: (b,) + (0,) * (len(shape) - 1))
    return pl.pallas_call(
        kernel,
        name=_call_name(kernel, bsz),
        grid=(bsz, t // SCAN_CHUNK),
        in_specs=([seq(a.shape[2]) for a in seq_ins] + [per_seq(a.shape) for a in state_ins]
                  + [_const_spec(a.shape, 2) for a in const_ins]),
        out_specs=[seq(n) for n in seq_out_widths] + [per_seq(s) for s in state_out_shapes],
        out_shape=([jax.ShapeDtypeStruct((bsz, t, n), seq_ins[0].dtype) for n in seq_out_widths]
                   + [jax.ShapeDtypeStruct(s, F32) for s in state_out_shapes]),
        scratch_shapes=[pltpu.VMEM(shape, dt) for shape, dt in scratch],
        compiler_params=_params(("parallel", "arbitrary")),
    )(*seq_ins, *state_ins, *const_ins)


def _step_call(kernel, row_ins, state_ins, const_ins, row_out_widths, state_out_shapes):
    bsz = row_ins[0].shape[0]
    rows = lambda n: pl.BlockSpec((STEP_SEQS, n), lambda i: (i, 0))
    state = lambda shape: pl.BlockSpec((STEP_SEQS,) + tuple(shape[1:]), lambda i: (i, 0, 0))
    return pl.pallas_call(
        kernel,
        name=_call_name(kernel, bsz),
        grid=(bsz // STEP_SEQS,),
        in_specs=([rows(a.shape[1]) for a in row_ins] + [state(a.shape) for a in state_ins]
                  + [_const_spec(a.shape, 1) for a in const_ins]),
        out_specs=[rows(n) for n in row_out_widths] + [state(s) for s in state_out_shapes],
        out_shape=([jax.ShapeDtypeStruct((bsz, n), F32) for n in row_out_widths]
                   + [jax.ShapeDtypeStruct(s, F32) for s in state_out_shapes]),
        compiler_params=_params(("parallel",)),
    )(*row_ins, *state_ins, *const_ins)


def _rwkv_step_call(rows, wkv, rk, lnw, lnb):
    bsz = wkv.shape[0]
    assert bsz == LANES, "the single-token RWKV kernel keeps the sequences on the lanes"
    n = RWKV_HEAD_DIM
    pair = pl.BlockSpec((bsz, LANES), lambda i: (0, i))
    vec = pl.BlockSpec((1, LANES), lambda i: (0, i))
    state = pl.BlockSpec((2, n, n, bsz), lambda i: (i, 0, 0, 0))
    y, new = pl.pallas_call(
        _rwkv_step_kernel,
        name=_call_name(_rwkv_step_kernel, bsz),
        grid=(RWKV_HEADS // 2,),
        in_specs=[pair] * 6 + [state] + [vec] * 3,
        out_specs=[pair, state],
        out_shape=[jax.ShapeDtypeStruct((bsz, RWKV_D), F32), jax.ShapeDtypeStruct((RWKV_HEADS, n, n, bsz), F32)],
        scratch_shapes=[pltpu.VMEM((LANES, bsz), F32)],
        compiler_params=_params(("parallel",)),
    )(*rows, jnp.transpose(wkv, (1, 2, 3, 0)), rk, lnw, lnb)
    return y, jnp.transpose(new, (3, 0, 1, 2))


def _row(v):
    return v.reshape(1, -1).astype(F32)


def _pad_cols(w, width):
    return jnp.pad(w, ((0, 0), (0, width - w.shape[1])))


def _pad_rows_to(w, rows):
    return jnp.pad(w, ((0, rows - w.shape[0]), (0, 0)))


def _blockdiag_slabs(w, scale=1.0):
    hd = MLSTM_HEAD_DIM
    rows = (w * scale).reshape(-1, hd, QKV_BLOCK)
    col = jnp.arange(hd)
    pick = (col[None, :] % QKV_BLOCK == jnp.arange(QKV_BLOCK)[:, None]).astype(w.dtype)
    tiled = jnp.einsum('srd,dc->src', rows, pick, precision=lax.Precision.HIGHEST)
    same_block = col[:, None] // QKV_BLOCK == col[None, :] // QKV_BLOCK
    return jnp.where(same_block, tiled, 0.0).astype(BF16)


def _head_spread():
    ch = jnp.arange(SSD_D_INNER) // SSD_HEAD_DIM
    return (jnp.arange(LANES)[:, None] == ch[None, :]).astype(BF16)


def _mixer0_consts(p):
    ssd = [p['conv_w'][:, :XBC_DIM], p['conv_b'][:, :XBC_DIM], p['small_bias'], p['ssd_a_log'], p['ssd_d'],
           p['ssd_norm'], p['segt']]
    ml = [p['conv_w'][:, XBC_DIM:], p['conv_b'][:, XBC_DIM:], p['small_bias'], p['ml_wq'], p['ml_wk'],
          p['ml_wv'], p['ml_norm'], p['ml_skip']]
    return ssd, ml


def _mixer0_chunked(xbc, xm, z, o_pre, small, conv, ssm, mc, mn, mm, p, bsz, t):
    seq = lambda a: a.reshape(bsz, t, -1)
    tail = jnp.pad(conv, ((0, 0), (SUBLANES - (CONV_W - 1), 0), (0, 0)))
    ssd_c, ml_c = _mixer0_consts(p)
    y_ssd, new_ssm = _scan_call(
        _ssd_kernel, [seq(xbc), seq(z), seq(small)],
        [tail[:, :, :XBC_DIM], ssm.reshape(bsz, SSD_D_INNER, SSD_D_STATE)], ssd_c,
        [SSD_D_INNER], [(bsz, SSD_D_INNER, SSD_D_STATE)],
        [((SUBLANES, XBC_DIM), F32), ((SSD_D_INNER, SSD_D_STATE), F32)])
    hm, new_c, new_n, new_m = _scan_call(
        _mlstm_kernel, [seq(xm), seq(o_pre), seq(small)],
        [tail[:, :, XBC_DIM:], mc.reshape(bsz, MLSTM_D_INNER, MLSTM_HEAD_DIM), mn.reshape(bsz, 1, MLSTM_D_INNER),
         _pad_cols(mm, LANES).reshape(bsz, 1, LANES)], ml_c,
        [MLSTM_D_INNER], [(bsz, MLSTM_D_INNER, MLSTM_HEAD_DIM), (bsz, 1, MLSTM_D_INNER), (bsz, 1, LANES)],
        [((SUBLANES, MLSTM_D_INNER), F32), ((MLSTM_D_INNER, MLSTM_HEAD_DIM), F32),
         ((1, MLSTM_D_INNER), F32), ((1, LANES), F32)])
    flat = lambda a: a.reshape(bsz * t, -1)
    return flat(y_ssd), flat(hm), new_ssm, new_c, new_n, new_m.reshape(bsz, LANES)


def _mixer0_step(xbc, xm, z, o_pre, small, conv, ssm, mc, mn, mm, p, bsz):
    ssd_c, ml_c = _mixer0_consts(p)
    tails = [conv[:, j, :] for j in range(CONV_W - 1)]
    y_ssd, new_ssm = _step_call(
        _ssd_step_kernel, [xbc] + [tl[:, :XBC_DIM] for tl in tails] + [z, small],
        [ssm.reshape(bsz, SSD_D_INNER, SSD_D_STATE)], ssd_c, [SSD_D_INNER], [(bsz, SSD_D_INNER, SSD_D_STATE)])
    hm, new_n, new_m, new_c = _step_call(
        _mlstm_step_kernel, [xm] + [tl[:, XBC_DIM:] for tl in tails]
        + [o_pre, small, mn.reshape(bsz, MLSTM_D_INNER), _pad_cols(mm, LANES)],
        [mc.reshape(bsz, MLSTM_D_INNER, MLSTM_HEAD_DIM)], ml_c,
        [MLSTM_D_INNER, MLSTM_D_INNER, LANES], [(bsz, MLSTM_D_INNER, MLSTM_HEAD_DIM)])
    return y_ssd, hm, new_ssm, new_c, new_n, new_m


def _trunk(x, conv, ssm, mc, mn, mm, shift, wkv, p):
    bsz, t, d = x.shape
    m = bsz * t
    step = t == 1
    assert step or t % SCAN_CHUNK == 0, "a group is either single-token or a multiple of the scan chunk"
    assert not step or bsz % STEP_SEQS == 0
    act = F32 if step else BF16
    x2 = x.reshape(m, d)

    z, xbc, xm, o_pre, small = _rowwise_call(
        _in0_kernel, [x2], [p['norm_mix0'], p['w_in0']],
        [(SSD_D_INNER, act), (XBC_DIM, act), (MLSTM_D_INNER, act), (MLSTM_D_INNER, act), (LANES, F32)], m,
        tile=ROW_TILE)
    last_rows = lambda a: a.reshape(bsz, t, -1)[:, -(CONV_W - 1):].astype(F32)
    conv_in_tail = jnp.concatenate([last_rows(xbc), last_rows(xm)], axis=-1)
    new_conv = jnp.concatenate([conv, conv_in_tail], axis=1)[:, -(CONV_W - 1):]
    if step:
        y_ssd, hm, new_ssm, new_c, new_n, new_m = _mixer0_step(xbc, xm, z, o_pre, small, conv, ssm, mc, mn, mm, p, bsz)
    else:
        y_ssd, hm, new_ssm, new_c, new_n, new_m = _mixer0_chunked(xbc, xm, z, o_pre, small, conv, ssm, mc, mn, mm,
                                                                  p, bsz, t)
    xn1, x2b = _rowwise_call(
        _mix0_ffn_kernel, [y_ssd, hm, x2],
        [p['w_out0'], p['norm_ffn0'], (p['ffn_gu'], 0), (p['ffn_d'], 0), p['norm_mix1']], [(d, act), (d, F32)], m,
        tile=ROW_TILE)

    rw_in_c = [p['rw_mu'], p['rw_wr'], p['rw_wk'], p['rw_wv'], p['rw_w1'], p['rw_w2'], p['rw_a1'], p['rw_a2'],
               p['rw_g1'], p['rw_g2'], p['rw_w0'], p['rw_a0'], p['rw_k_k'], p['rw_k_a']]
    rw_outs = [(d, act), (d, F32)] + [(d, act)] * 5
    if step:
        rw = _rowwise_call(_rw_in_kernel, [xn1, shift], rw_in_c, rw_outs, m)
    else:
        tm = _row_tile(t, ROW_TILE)
        tiles_per_seq = t // tm
        prev_spec = pl.BlockSpec((PREV_ROWS, d), lambda i: (jnp.maximum(i * (tm // PREV_ROWS) - 1, 0), 0))
        shift_spec = pl.BlockSpec((None, 1, d), lambda i: (i // tiles_per_seq, 0, 0))
        rw = _rowwise_call(functools.partial(_rw_in_shift_kernel, tiles_per_seq=tiles_per_seq), [xn1], rw_in_c,
                           rw_outs, m, tile=tm, extra_ins=[(xn1, prev_spec), (shift.reshape(bsz, 1, d), shift_spec)])
    g = rw[6]
    rw_c = [p['rw_r_k'], p['rw_ln_w'], p['rw_ln_b']]
    if step:
        y_rw, new_wkv = _rwkv_step_call(list(rw[:6]), wkv, *rw_c)
    else:
        wkv3 = wkv.reshape(bsz, RWKV_D, RWKV_HEAD_DIM)
        y_rw, new_wkv = _scan_call(
            _rwkv_kernel, [a.reshape(bsz, t, d) for a in rw[:6]], [wkv3], rw_c,
            [d], [wkv3.shape], [((RWKV_D, 2 * RWKV_HEAD_DIM), F32)])
        y_rw = y_rw.reshape(m, d)
    new_wkv = new_wkv.reshape(wkv.shape)
    (y,) = _rowwise_call(
        _mix1_ffn_kernel, [y_rw, g, x2b],
        [p['rw_wo'], p['norm_ffn1'], (p['ffn_gu'], 1), (p['ffn_d'], 1), p['norm_final']], [(d, F32)], m, tile=ROW_TILE)

    states = (new_conv, new_ssm.reshape(ssm.shape), new_c.reshape(mc.shape), new_n.reshape(mn.shape),
              new_m[:, :MLSTM_HEADS], xn1.reshape(bsz, t, d)[:, -1].astype(F32), new_wkv)
    return y.reshape(bsz, t, d), states


def kernel(x_prompt, x_sample, state_conv, state_ssm, state_mlstm_c, state_mlstm_n, state_mlstm_m, state_shift, state_wkv, norm_mix, norm_ffn, norm_final, w_in0, conv_w, conv_b, ssd_dt_bias, ssd_a_log, ssd_d, ssd_norm, ml_wq, ml_wk, ml_wv, ml_i_bias, ml_f_bias, ml_norm, ml_skip, w_out0, rw_mu, rw_wr, rw_wk, rw_wv, rw_wo, rw_w0, rw_w1, rw_w2, rw_a0, rw_a1, rw_a2, rw_g1, rw_g2, rw_k_k, rw_k_a, rw_r_k, rw_ln_w, rw_ln_b, ffn_w_gate_up, ffn_w_down):
    assert norm_mix.shape[0] == 2 and w_in0.shape[0] == 1 and rw_wr.shape[0] == 1, "two layers: SSD|mLSTM then RWKV-7"
    s1 = SSD_D_INNER
    s2 = s1 + XBC_DIM + MLSTM_D_INNER
    s3 = s2 + SSD_HEADS
    s4 = s3 + MLSTM_D_INNER
    s5 = s4 + MLSTM_HEADS
    wt = jnp.transpose(w_in0[0])
    small_wt = _pad_rows_to(jnp.concatenate([wt[s2:s3], wt[s4:s5], wt[s5:]], axis=0), LANES)
    lora = lambda w1, w2, width: (_pad_cols(w1, width).astype(BF16), _pad_rows_to(w2, width).astype(BF16))
    rw_w1p, rw_w2p = lora(rw_w1[0], rw_w2[0], LANES)
    rw_a1p, rw_a2p = lora(rw_a1[0], rw_a2[0], LANES)
    rw_g1p, rw_g2p = lora(rw_g1[0], rw_g2[0], 2 * LANES)
    p = dict(
        norm_mix0=_row(norm_mix[0]), norm_mix1=_row(norm_mix[1]), norm_ffn0=_row(norm_ffn[0]),
        norm_ffn1=_row(norm_ffn[1]), norm_final=_row(norm_final),
        w_in0=jnp.concatenate([wt[:s2], wt[s3:s4], small_wt], axis=0).astype(BF16),
        conv_w=conv_w[0], conv_b=_row(conv_b[0]),
        small_bias=_pad_cols(jnp.concatenate([_row(ssd_dt_bias[0]), _row(ml_i_bias[0]), _row(ml_f_bias[0])], axis=1),
                             LANES),
        ssd_a_log=_pad_cols(_row(ssd_a_log[0]), LANES), ssd_d=_row(jnp.repeat(ssd_d[0], SSD_HEAD_DIM)),
        ssd_norm=_row(ssd_norm[0]),
        ml_wq=_blockdiag_slabs(ml_wq[0]), ml_wk=_blockdiag_slabs(ml_wk[0], MLSTM_HEAD_DIM ** -0.5),
        ml_wv=_blockdiag_slabs(ml_wv[0]), ml_norm=_row(ml_norm[0]), ml_skip=_row(ml_skip[0]),
        w_out0=w_out0[0].astype(BF16),
        rw_mu=_pad_rows_to(rw_mu[0], SUBLANES),
        rw_wr=rw_wr[0].astype(BF16), rw_wk=rw_wk[0].astype(BF16), rw_wv=rw_wv[0].astype(BF16),
        rw_wo=rw_wo[0].astype(BF16), rw_w0=_row(rw_w0[0]), rw_a0=_row(rw_a0[0]),
        rw_w1=rw_w1p, rw_w2=rw_w2p, rw_a1=rw_a1p, rw_a2=rw_a2p, rw_g1=rw_g1p, rw_g2=rw_g2p,
        rw_k_k=_row(rw_k_k[0]), rw_k_a=_row(rw_k_a[0]), rw_r_k=_row(rw_r_k[0]),
        rw_ln_w=_row(rw_ln_w[0]), rw_ln_b=_row(rw_ln_b[0]),
        ffn_gu=ffn_w_gate_up.astype(BF16), ffn_d=ffn_w_down.astype(BF16),
        segt=_head_spread(),
    )
    bp = x_prompt.shape[0]
    zeros = lambda s: jnp.zeros((bp,) + s.shape[2:], F32)
    y_p, st_p = _trunk(x_prompt, zeros(state_conv), zeros(state_ssm), zeros(state_mlstm_c),
                       zeros(state_mlstm_n), zeros(state_mlstm_m), zeros(state_shift), zeros(state_wkv), p)
    y_s, st_s = _trunk(x_sample, state_conv[0], state_ssm[0], state_mlstm_c[0], state_mlstm_n[0],
                       state_mlstm_m[0], state_shift[0], state_wkv[0], p)
    out = [y_p, y_s]
    for a, b in zip(st_p, st_s):
        out += [a[None], b[None]]
    return tuple(out)
```

```python
import functools

import jax
import jax.numpy as jnp
from jax import lax
from jax.experimental import pallas as pl
from jax.experimental.pallas import tpu as pltpu

F32 = jnp.float32
BF16 = jnp.bfloat16

D_MODEL = 1024
NORM_EPS = 1e-5
CONV_W = 4
SSD_HEADS = 16
SSD_HEAD_DIM = 64
SSD_GROUPS = 2
SSD_D_STATE = 128
SSD_D_INNER = SSD_HEADS * SSD_HEAD_DIM
SSD_GROUP_W = SSD_D_INNER // SSD_GROUPS
XBC_DIM = SSD_D_INNER + 2 * SSD_GROUPS * SSD_D_STATE
MLSTM_HEADS = 4
MLSTM_HEAD_DIM = 256
MLSTM_D_INNER = MLSTM_HEADS * MLSTM_HEAD_DIM
QKV_BLOCK = 4
RWKV_HEADS = 16
RWKV_HEAD_DIM = 64
RWKV_D = RWKV_HEADS * RWKV_HEAD_DIM
RWKV_LN_EPS = 64e-5
D_FF = 2816
FF_CHUNK = 256
LANES = 128
SUBLANES = 8
SMALL_DT, SMALL_I, SMALL_F = 0, 16, 20
NEG = -1e30
EXP_NEG_HALF = 0.6065306597126334
SCAN_CHUNK = 128
STEP_SEQS = SUBLANES
STATE_SLOTS = 3
ROW_TILE = 512
PREV_ROWS = 16
VMEM_LIMIT = 56 * 1024 * 1024


def _dot(a, b):
    return jnp.dot(a.astype(BF16), b.astype(BF16), preferred_element_type=F32)


def _dot_nt(a, b):
    return lax.dot_general(a.astype(BF16), b.astype(BF16), (((1,), (1,)), ((), ())),
                           preferred_element_type=F32)


def _dot_tn(a, b):
    return lax.dot_general(a.astype(BF16), b.astype(BF16), (((0,), (0,)), ((), ())),
                           preferred_element_type=F32)


def _split(x, terms):
    parts = []
    rem = x
    for _ in range(terms):
        p = rem.astype(BF16)
        parts.append(p)
        rem = rem - p.astype(F32)
    return parts


def _dot_exact_rhs(x, m, terms=2):
    out = None
    for p in _split(x, terms):
        t = jnp.dot(p, m, preferred_element_type=F32)
        out = t if out is None else out + t
    return out


def _tri_cumsum(tri, x, terms=3):
    out = None
    for p in _split(x, terms):
        t = jnp.dot(tri, p, preferred_element_type=F32)
        out = t if out is None else out + t
    return out


def _rms_rows(x, g):
    return x * lax.rsqrt(jnp.mean(x * x, axis=-1, keepdims=True) + NORM_EPS) * g


def _group_rms(x, width):
    outs = []
    for lo in range(0, x.shape[1], width):
        grp = x[:, lo:lo + width]
        outs.append(grp * lax.rsqrt(jnp.mean(grp * grp, axis=-1, keepdims=True) + NORM_EPS))
    return jnp.concatenate(outs, axis=1)


def _sigmoid(x):
    return 0.5 * jnp.tanh(0.5 * x) + 0.5


def _silu(x):
    h = 0.5 * x
    return h * jnp.tanh(h) + h


def _softplus(x):
    return jnp.maximum(x, 0.0) + jnp.log(1.0 + jnp.exp(-jnp.abs(x)))


def _log_sigmoid(x):
    return jnp.minimum(x, 0.0) - jnp.log(1.0 + jnp.exp(-jnp.abs(x)))


def _tri_masks(n):
    row = lax.broadcasted_iota(jnp.int32, (n, n), 0)
    col = lax.broadcasted_iota(jnp.int32, (n, n), 1)
    return row >= col, row > col


def _chunk_conv(tail, u_ref, conv0_ref, cw_ref, cb_ref, first):
    assert u_ref.dtype == BF16
    rows = u_ref.shape[0]
    taps = CONV_W - 1

    @pl.when(first)
    def _():
        tail[...] = conv0_ref[...]

    ub = u_ref[...]
    u = ub.astype(F32)
    out_row = lax.broadcasted_iota(jnp.int32, (taps * rows, rows), 0)
    src_row = lax.broadcasted_iota(jnp.int32, (taps * rows, rows), 1)
    back = taps - out_row // rows
    shift = jnp.where(src_row == out_row % rows - back, 1.0, 0.0).astype(BF16)
    shifted = jnp.dot(shift, ub, preferred_element_type=F32)
    conv = cb_ref[...] + u * cw_ref[taps:CONV_W, :]
    window = jnp.concatenate([tail[...], jnp.zeros((SUBLANES, u.shape[1]), F32)], axis=0)
    edge = jnp.zeros((SUBLANES, u.shape[1]), F32)
    for j in range(taps):
        conv = conv + shifted[j * rows:(j + 1) * rows, :] * cw_ref[j:j + 1, :]
        edge = edge + pltpu.roll(window, shift=taps - j, axis=0)[SUBLANES:2 * SUBLANES, :] * cw_ref[j:j + 1, :]
    tail[...] = u[rows - SUBLANES:rows, :]
    return jnp.concatenate([conv[0:SUBLANES, :] + edge, conv[SUBLANES:rows, :]], axis=0), u


def _step_conv(u_ref, tail_refs, cw_ref, cb_ref):
    conv = cb_ref[...] + u_ref[...] * cw_ref[CONV_W - 1:CONV_W, :]
    for j, t in enumerate(tail_refs):
        conv = conv + t[...] * cw_ref[j:j + 1, :]
    return conv


def _blockdiag_qkv(act, raw, wq_ref, wk_ref, wv_ref):
    hd = MLSTM_HEAD_DIM
    cat = lambda x, w: jnp.concatenate(
        [_dot(x[:, s * hd:(s + 1) * hd], w[s]) for s in range(MLSTM_HEADS)], axis=1)
    return cat(act, wq_ref), cat(act, wk_ref), cat(raw, wv_ref)


def _head_sum(x):
    half = lax.broadcasted_iota(jnp.int32, (x.shape[0], LANES), 1) < RWKV_HEAD_DIM
    outs = []
    for lo in range(0, x.shape[1], LANES):
        xp = x[:, lo:lo + LANES]
        s0 = jnp.sum(jnp.where(half, xp, 0.0), axis=1, keepdims=True)
        s1 = jnp.sum(jnp.where(half, 0.0, xp), axis=1, keepdims=True)
        outs.append(jnp.where(half, s0, s1))
    return jnp.concatenate(outs, axis=1)


def _rwkv_head_norm(y, r, k, v, rk_ref, lnw_ref, lnb_ref):
    inv_n = 1.0 / RWKV_HEAD_DIM
    dlt = y - _head_sum(y) * inv_n
    yn = dlt * lax.rsqrt(_head_sum(dlt * dlt) * inv_n + RWKV_LN_EPS)
    return yn * lnw_ref[...] + lnb_ref[...] + _head_sum(r * k * rk_ref[...]) * v


def _in0_kernel(x_ref, g_ref, w_ref, z_ref, xbc_ref, xm_ref, o_ref, sm_ref):
    xn = _rms_rows(x_ref[...], g_ref[...]).astype(BF16)
    off = 0
    for ref in (z_ref, xbc_ref, xm_ref, o_ref, sm_ref):
        n = ref.shape[-1]
        ref[...] = jnp.dot(xn, w_ref[:, off:off + n], preferred_element_type=F32).astype(ref.dtype)
        off += n


def _swiglu_residual(x, g_ref, wgu_ref, wd_ref):
    xn = _rms_rows(x, g_ref[...]).astype(BF16)
    acc = None
    for c in range(D_FF // FF_CHUNK):
        lo = c * FF_CHUNK
        gate = jnp.dot(xn, wgu_ref[:, lo:lo + FF_CHUNK], preferred_element_type=F32)
        up = jnp.dot(xn, wgu_ref[:, D_FF + lo:D_FF + lo + FF_CHUNK], preferred_element_type=F32)
        part = _dot(_silu(gate) * up, wd_ref[lo:lo + FF_CHUNK, :])
        acc = part if acc is None else acc + part
    return x + acc


def _mix0_ffn_kernel(y_ref, h_ref, x_ref, wo_ref, g_ref, wgu_ref, wd_ref, pg_ref, xn_ref, xo_ref):
    mixed = jnp.concatenate([y_ref[...].astype(BF16), h_ref[...].astype(BF16)], axis=1)
    x1 = x_ref[...] + jnp.dot(mixed, wo_ref[...], preferred_element_type=F32)
    x2 = _swiglu_residual(x1, g_ref, wgu_ref, wd_ref)
    xo_ref[...] = x2
    xn_ref[...] = _rms_rows(x2, pg_ref[...]).astype(xn_ref.dtype)


def _mix1_ffn_kernel(y_ref, gate_ref, x_ref, wo_ref, g_ref, wgu_ref, wd_ref, pg_ref, o_ref):
    x3 = x_ref[...] + _dot(y_ref[...].astype(F32) * gate_ref[...].astype(F32), wo_ref[...])
    o_ref[...] = _rms_rows(_swiglu_residual(x3, g_ref, wgu_ref, wd_ref), pg_ref[...])


def _rw_in_body(xn, xp, mu_ref, wr_ref, wk_ref, wv_ref, w1_ref, w2_ref, a1_ref, a2_ref,
                g1_ref, g2_ref, w0_ref, a0_ref, kk_w_ref, ka_w_ref,
                r_ref, lw_ref, k_ref, v_ref, kk_ref, a_ref, g_ref):
    xx = xp - xn
    xr, xw, xk, xv, xa, xg = (xn + xx * mu_ref[c:c + 1, :] for c in range(6))
    r_ref[...] = _dot(xr, wr_ref[...]).astype(r_ref.dtype)
    k = _dot(xk, wk_ref[...])
    v_ref[...] = _dot(xv, wv_ref[...]).astype(v_ref.dtype)
    z = w0_ref[...] + _dot(jnp.tanh(_dot(xw, w1_ref[...])), w2_ref[...])
    lw_ref[...] = -EXP_NEG_HALF * _sigmoid(z)
    a = _sigmoid(a0_ref[...] + _dot(_dot(xa, a1_ref[...]), a2_ref[...]))
    a_ref[...] = a.astype(a_ref.dtype)
    g_ref[...] = _dot(_sigmoid(_dot(xg, g1_ref[...])), g2_ref[...]).astype(g_ref.dtype)
    kk = k * kk_w_ref[...]
    ss = _head_sum(kk * kk)
    kk_ref[...] = (kk * lax.rsqrt(jnp.maximum(ss, 1e-24))).astype(kk_ref.dtype)
    k_ref[...] = (k * (1.0 + (a - 1.0) * ka_w_ref[...])).astype(k_ref.dtype)


def _rw_in_kernel(xn_ref, xp_ref, *refs):
    _rw_in_body(xn_ref[...].astype(F32), xp_ref[...].astype(F32), *refs)


def _rw_in_shift_kernel(xn_ref, prev_ref, shift_ref, *refs, tiles_per_seq):
    xn = xn_ref[...].astype(F32)
    starts_seq = pl.program_id(0) % tiles_per_seq == 0
    last = prev_ref.shape[0] - 1
    prev = jnp.where(starts_seq, shift_ref[...], prev_ref[last:last + 1, :].astype(F32))
    row = lax.broadcasted_iota(jnp.int32, xn.shape, 0)
    xp = jnp.where(row == 0, prev, pltpu.roll(xn, shift=1, axis=0))
    _rw_in_body(xn, xp, *refs)


def _ssd_kernel(xbc_ref, z_ref, sm_ref, conv0_ref, st0_ref, cw_ref, cb_ref, bias_ref, alog_ref,
                dexp_ref, nw_ref, segt_ref, y_ref, st_ref, ext, state):
    c = pl.program_id(1)
    first = c == 0
    rows = xbc_ref.shape[0]

    @pl.when(first)
    def _():
        state[...] = st0_ref[...]

    conv, _ = _chunk_conv(ext, xbc_ref, conv0_ref, cw_ref, cb_ref, first)
    act = _silu(conv)
    xs = act[:, 0:SSD_D_INNER]
    gn = SSD_GROUPS * SSD_D_STATE
    bm = act[:, SSD_D_INNER:SSD_D_INNER + gn]
    cm = act[:, SSD_D_INNER + gn:SSD_D_INNER + 2 * gn]

    tril, _ = _tri_masks(rows)
    tri = jnp.where(tril, 1.0, 0.0).astype(BF16)
    lane = lax.broadcasted_iota(jnp.int32, (rows, LANES), 1)
    head_lane = (lane >= SMALL_DT) & (lane < SMALL_DT + SSD_HEADS)
    dt = jnp.where(head_lane, _softplus(sm_ref[...] + bias_ref[...]), 0.0)
    acum = _tri_cumsum(tri, dt * (-jnp.exp(alog_ref[...])))
    acum_t = acum.T
    dt_t = dt.T
    segt = segt_ref[...]
    eac_x = _dot_exact_rhs(jnp.exp(acum), segt, terms=1)
    wend_x = _dot_exact_rhs(jnp.exp(acum[rows - 1:rows, :] - acum) * dt, segt, terms=1)
    xw = xs * wend_x

    half = lane < SSD_HEAD_DIM
    heads_per_group = SSD_HEADS // SSD_GROUPS
    grp = lambda x, g: x[:, g * SSD_D_STATE:(g + 1) * SSD_D_STATE]
    cbs = [_dot_nt(grp(cm, g), grp(bm, g)) for g in range(SSD_GROUPS)]
    decays = [jnp.exp(jnp.where(tril, acum[:, j:j + 1] - acum_t[j:j + 1, :], NEG)) for j in range(SSD_HEADS)]
    wts = [decays[j] * cbs[j // heads_per_group] * dt_t[j:j + 1, :] for j in range(SSD_HEADS)]
    lhs = [jnp.concatenate([wts[2 * p], wts[2 * p + 1]], axis=1).astype(BF16) for p in range(SSD_HEADS // 2)]
    rhs = []
    for p in range(SSD_HEADS // 2):
        xp = xs[:, p * LANES:(p + 1) * LANES]
        rhs.append(jnp.concatenate([jnp.where(half, xp, 0.0), jnp.where(half, 0.0, xp)], axis=0).astype(BF16))
    pairs = [_dot(l, r) for l, r in zip(lhs, rhs)]
    y = jnp.concatenate(pairs, axis=1)

    st = state[...]
    gw = SSD_GROUP_W
    y_state = jnp.concatenate(
        [_dot_nt(cm[:, g * SSD_D_STATE:(g + 1) * SSD_D_STATE], st[g * gw:(g + 1) * gw, :])
         for g in range(SSD_GROUPS)], axis=1)
    y = y + eac_x * y_state + dexp_ref[...] * xs
    y_ref[...] = (_group_rms(y * _silu(z_ref[...].astype(F32)), gw) * nw_ref[...]).astype(y_ref.dtype)

    dcol = jnp.exp(acum_t[:, rows - 1:rows])
    for g in range(SSD_GROUPS):
        upd = _dot_tn(xw[:, g * gw:(g + 1) * gw], bm[:, g * SSD_D_STATE:(g + 1) * SSD_D_STATE])
        for h in range(heads_per_group):
            j = g * heads_per_group + h
            lo = j * SSD_HEAD_DIM
            state[lo:lo + SSD_HEAD_DIM, :] = (st[lo:lo + SSD_HEAD_DIM, :] * dcol[j:j + 1, :]
                                              + upd[h * SSD_HEAD_DIM:(h + 1) * SSD_HEAD_DIM, :])

    @pl.when(c == pl.num_programs(1) - 1)
    def _():
        st_ref[...] = state[...]


def _mlstm_kernel(xm_ref, o_ref, sm_ref, conv0_ref, c0_ref, n0_ref, m0_ref, cw_ref, cb_ref, bias_ref,
                  wq_ref, wk_ref, wv_ref, nw_ref, skip_ref, h_ref, c_ref, n_ref, m_ref,
                  ext, cst, nst, mst):
    c = pl.program_id(1)
    first = c == 0
    rows = xm_ref.shape[0]

    @pl.when(first)
    def _():
        cst[...] = c0_ref[...]
        nst[...] = n0_ref[...]
        mst[...] = m0_ref[...]

    conv, raw = _chunk_conv(ext, xm_ref, conv0_ref, cw_ref, cb_ref, first)
    act = _silu(conv)
    q, k, v = _blockdiag_qkv(act, raw, wq_ref, wk_ref, wv_ref)

    tril, _ = _tri_masks(rows)
    tri = jnp.where(tril, 1.0, 0.0).astype(BF16)
    logi = sm_ref[...] + bias_ref[...]
    bcum = _tri_cumsum(tri, _log_sigmoid(logi))
    bcum_t = bcum.T
    logi_t = logi.T
    m_all = mst[...]
    lane1 = lax.broadcasted_iota(jnp.int32, (1, LANES), 1)
    hd = MLSTM_HEAD_DIM

    heads = range(MLSTM_HEADS)
    cols = [slice(h * hd, (h + 1) * hd) for h in heads]
    qs, ks, vs = ([t[:, c] for c in cols] for t in (q, k, v))
    qbs, vbs = [x.astype(BF16) for x in qs], [x.astype(BF16) for x in vs]
    bcs = [bcum[:, SMALL_F + h:SMALL_F + h + 1] for h in heads]
    lis = [logi[:, SMALL_I + h:SMALL_I + h + 1] for h in heads]
    m_hs = [m_all[:, h:h + 1] for h in heads]
    c_hs = [cst[c, :] for c in cols]
    n_hs = [nst[:, c] for c in cols]
    dlogs = [jnp.where(tril, bcs[h] - bcum_t[SMALL_F + h:SMALL_F + h + 1, :]
                       + logi_t[SMALL_I + h:SMALL_I + h + 1, :], NEG) for h in heads]
    inters = [bcs[h] + m_hs[h] for h in heads]
    m_ts = [jnp.maximum(inters[h], jnp.max(dlogs[h], axis=1, keepdims=True)) for h in heads]
    qks = [_dot_nt(qbs[h], ks[h]) for h in heads]
    qcs = [_dot(qbs[h], c_hs[h]) for h in heads]
    ss = [qks[h] * jnp.exp(dlogs[h] - m_ts[h]) for h in heads]
    w_inters = [jnp.exp(inters[h] - m_ts[h]) for h in heads]
    nums = [_dot(ss[h], vbs[h]) + w_inters[h] * qcs[h] for h in heads]
    dens = [jnp.sum(ss[h], axis=1, keepdims=True)
            + w_inters[h] * jnp.sum(qs[h] * n_hs[h], axis=1, keepdims=True) for h in heads]
    hs = [nums[h] * (1.0 / jnp.maximum(jnp.abs(dens[h]), jnp.exp(-m_ts[h]))) for h in heads]
    b_ends = [bcs[h][rows - 1:rows, :] for h in heads]
    wlogs = [b_ends[h] - bcs[h] + lis[h] for h in heads]
    m_news = [jnp.maximum(b_ends[h] + m_hs[h], jnp.max(wlogs[h], axis=0, keepdims=True)) for h in heads]
    dcs = [jnp.exp(b_ends[h] + m_hs[h] - m_news[h]) for h in heads]
    kwss = [ks[h] * jnp.exp(wlogs[h] - m_news[h]) for h in heads]
    upds = [_dot_tn(kwss[h], vbs[h]) for h in heads]
    for h in heads:
        cst[cols[h], :] = dcs[h] * c_hs[h] + upds[h]
        nst[:, cols[h]] = dcs[h] * n_hs[h] + jnp.sum(kwss[h], axis=0, keepdims=True)
        m_all = jnp.where(lane1 == h, m_news[h], m_all)
    mst[...] = m_all

    hm = _group_rms(jnp.concatenate(hs, axis=1), hd) * nw_ref[...]
    h_ref[...] = ((hm + skip_ref[...] * act) * _sigmoid(o_ref[...].astype(F32))).astype(h_ref.dtype)

    @pl.when(c == pl.num_programs(1) - 1)
    def _():
        c_ref[...] = cst[...]
        n_ref[...] = nst[...]
        m_ref[...] = mst[...]


def _rwkv_kernel(r_ref, lw_ref, k_ref, v_ref, kk_ref, a_ref, s0_ref, rk_ref, lnw_ref, lnb_ref,
                 y_ref, s_ref, sblk):
    c = pl.program_id(1)
    rows = r_ref.shape[0]

    @pl.when(c == 0)
    def _():
        s0 = s0_ref[...]
        shape = (RWKV_D, 2 * RWKV_HEAD_DIM)
        head_parity = (lax.broadcasted_iota(jnp.int32, shape, 0) // RWKV_HEAD_DIM) % 2
        lane_half = lax.broadcasted_iota(jnp.int32, shape, 1) // RWKV_HEAD_DIM
        sblk[...] = jnp.where(head_parity == lane_half, jnp.concatenate([s0, s0], axis=1), 0.0)

    r, lw, k, v, kk, a = (t[...].astype(F32) for t in (r_ref, lw_ref, k_ref, v_ref, kk_ref, a_ref))
    tril, strict = _tri_masks(rows)
    tri = jnp.where(tril, 1.0, 0.0).astype(BF16)
    cum = _tri_cumsum(tri, lw, terms=2)
    last = cum[rows - 1:rows, :]
    einv = jnp.exp(-cum)
    eend = jnp.exp(last - cum)
    ka = kk * a
    at = -(kk * jnp.exp(cum - lw))
    rt = r * jnp.exp(cum)
    bt = ka * einv
    kt = k * einv
    bh = ka * eend
    kh = k * eend
    wl = jnp.exp(last)
    half = lax.broadcasted_iota(jnp.int32, (rows, LANES), 1) < RWKV_HEAD_DIM
    steps = rows.bit_length() - 1

    def two(x):
        return [jnp.where(half, x, 0.0), jnp.where(half, 0.0, x)]

    npairs = RWKV_HEADS // 2
    tril4 = jnp.concatenate([tril] * 4, axis=1)
    strict4 = jnp.concatenate([strict] * 4, axis=1)
    lanes = [slice(p * LANES, (p + 1) * LANES) for p in range(npairs)]

    a2s = [jnp.concatenate([at[:, sl], rt[:, sl]], axis=0).astype(BF16) for sl in lanes]
    rbs = [jnp.concatenate(two(bt[:, sl]) + two(kt[:, sl]), axis=0).astype(BF16) for sl in lanes]
    sps = [sblk[sl, :] for sl in lanes]
    vss = [two(v[:, sl]) for sl in lanes]
    pbks = [_dot_nt(a2, rb) for a2, rb in zip(a2s, rbs)]
    xss = [_dot_nt(a2, sp) for a2, sp in zip(a2s, sps)]
    tops = [jnp.where(strict4, pbk[0:rows, :], 0.0) for pbk in pbks]
    us = [xs[0:rows, :] + _dot(top[:, 2 * rows:4 * rows], jnp.concatenate(vs, axis=0))
          for xs, top, vs in zip(xss, tops, vss)]
    pws = [[top[:, e * rows:(e + 1) * rows].astype(BF16) for e in range(2)] for top in tops]
    ybase = [xs[rows:2 * rows, :] for xs in xss]
    w4s = [jnp.where(tril4, pbk[rows:2 * rows, :], 0.0).astype(BF16) for pbk in pbks]

    for j in range(steps):
        for p in range(npairs):
            u2 = jnp.concatenate(two(us[p]), axis=0)
            us[p] = us[p] + _dot(jnp.concatenate(pws[p], axis=1), u2)
        if j < steps - 1:
            for p in range(npairs):
                pws[p] = [_dot(pw, pw).astype(BF16) for pw in pws[p]]

    ys = []
    for p, sl in enumerate(lanes):
        r4 = jnp.concatenate(two(us[p]) + vss[p], axis=0).astype(BF16)
        ys.append(ybase[p] + _dot(w4s[p], r4))
        rh = jnp.concatenate(two(bh[:, sl]) + two(kh[:, sl]), axis=0)
        sblk[sl, :] = sps[p] * wl[:, sl] + _dot_tn(r4, rh)
    y = jnp.concatenate(ys, axis=1)
    y_ref[...] = _rwkv_head_norm(y, r, k, v, rk_ref, lnw_ref, lnb_ref).astype(y_ref.dtype)

    @pl.when(c == pl.num_programs(1) - 1)
    def _():
        sb = sblk[...]
        s_ref[...] = sb[:, 0:RWKV_HEAD_DIM] + sb[:, RWKV_HEAD_DIM:2 * RWKV_HEAD_DIM]


def _ssd_step_kernel(xbc_ref, t0_ref, t1_ref, t2_ref, z_ref, sm_ref, st_ref, cw_ref, cb_ref, bias_ref,
                     alog_ref, dexp_ref, nw_ref, segt_ref, y_ref, sto_ref):
    nseq = xbc_ref.shape[0]
    act = _silu(_step_conv(xbc_ref, (t0_ref, t1_ref, t2_ref), cw_ref, cb_ref))
    xs = act[:, 0:SSD_D_INNER]
    gn = SSD_GROUPS * SSD_D_STATE
    bm = act[:, SSD_D_INNER:SSD_D_INNER + gn]
    cm = act[:, SSD_D_INNER + gn:SSD_D_INNER + 2 * gn]
    lane = lax.broadcasted_iota(jnp.int32, (nseq, LANES), 1)
    head_lane = (lane >= SMALL_DT) & (lane < SMALL_DT + SSD_HEADS)
    dt = jnp.where(head_lane, _softplus(sm_ref[...] + bias_ref[...]), 0.0)
    segt = segt_ref[...]
    dec_t = _dot_exact_rhs(jnp.exp(dt * (-jnp.exp(alog_ref[...]))), segt).T
    xdt = xs * _dot_exact_rhs(dt, segt)
    gw = SSD_GROUP_W
    rowi = lax.broadcasted_iota(jnp.int32, (nseq, gw), 0)
    ys = [jnp.zeros((nseq, gw), F32) for _ in range(SSD_GROUPS)]
    for s in range(nseq):
        for g in range(SSD_GROUPS):
            rs = slice(g * gw, (g + 1) * gw)
            ns = slice(g * SSD_D_STATE, (g + 1) * SSD_D_STATE)
            outer = _dot_tn(jnp.where(rowi == s, xdt[:, rs], 0.0), bm[:, ns])
            new = st_ref[s, rs, :] * dec_t[rs, s:s + 1] + outer
            sto_ref[s, rs, :] = new
            ys[g] = jnp.where(rowi == s, _dot_nt(cm[:, ns], new), ys[g])
    y = jnp.concatenate(ys, axis=1) + dexp_ref[...] * xs
    y_ref[...] = _group_rms(y * _silu(z_ref[...]), gw) * nw_ref[...]


def _mlstm_step_kernel(xm_ref, t0_ref, t1_ref, t2_ref, o_ref, sm_ref, n_ref, m_ref, c_ref, cw_ref, cb_ref,
                       bias_ref, wq_ref, wk_ref, wv_ref, nw_ref, skip_ref, h_ref, no_ref, mo_ref, co_ref):
    nseq = xm_ref.shape[0]
    raw = xm_ref[...]
    act = _silu(_step_conv(xm_ref, (t0_ref, t1_ref, t2_ref), cw_ref, cb_ref))
    q, k, v = _blockdiag_qkv(act, raw, wq_ref, wk_ref, wv_ref)
    logi = sm_ref[...] + bias_ref[...]
    logf = _log_sigmoid(logi)
    m_old = m_ref[...]
    m_all = m_old
    n_old = n_ref[...]
    lane = lax.broadcasted_iota(jnp.int32, (nseq, LANES), 1)
    hd = MLSTM_HEAD_DIM
    rowi = lax.broadcasted_iota(jnp.int32, (nseq, hd), 0)
    hs, ns = [], []
    for h in range(MLSTM_HEADS):
        cols = slice(h * hd, (h + 1) * hd)
        qh, kh, vh = q[:, cols], k[:, cols], v[:, cols]
        lf = logf[:, SMALL_F + h:SMALL_F + h + 1] + m_old[:, h:h + 1]
        li = logi[:, SMALL_I + h:SMALL_I + h + 1]
        m_new = jnp.maximum(lf, li)
        dc = jnp.exp(lf - m_new)
        kws = kh * jnp.exp(li - m_new)
        n_new = dc * n_old[:, cols] + kws
        den = jnp.sum(qh * n_new, axis=1, keepdims=True)
        q_t = qh.T
        num = jnp.zeros((nseq, hd), F32)
        for s in range(nseq):
            outer = _dot_tn(jnp.where(rowi == s, kws, 0.0), vh)
            c_new = dc[s:s + 1, :] * c_ref[s, cols, :] + outer
            co_ref[s, cols, :] = c_new
            num = jnp.where(rowi == s, jnp.sum(q_t[:, s:s + 1] * c_new, axis=0, keepdims=True), num)
        hs.append(num / jnp.maximum(jnp.abs(den), jnp.exp(-m_new)))
        ns.append(n_new)
        m_all = jnp.where(lane == h, m_new, m_all)
    no_ref[...] = jnp.concatenate(ns, axis=1)
    mo_ref[...] = m_all
    hm = _group_rms(jnp.concatenate(hs, axis=1), hd) * nw_ref[...]
    h_ref[...] = (hm + skip_ref[...] * act) * _sigmoid(o_ref[...])


def _rwkv_step_kernel(r_ref, lw_ref, k_ref, v_ref, kk_ref, a_ref, s_ref, rk_ref, lnw_ref, lnb_ref,
                      y_ref, so_ref, yt):
    r, k, v, kk = (t[...] for t in (r_ref, k_ref, v_ref, kk_ref))
    n = RWKV_HEAD_DIM
    w_t, kk_t, ka_t, k_t, r_t, v_t = (x.T for x in (jnp.exp(lw_ref[...]), kk, kk * a_ref[...], k, r, v))
    for e in range(2):
        hs = slice(e * n, (e + 1) * n)
        w_h, kk_h, ka_h, k_h, r_h = w_t[hs], kk_t[hs], ka_t[hs], k_t[hs], r_t[hs]
        for i in range(n):
            row = e * n + i
            st = s_ref[e, i]
            sa = jnp.sum(st * kk_h, axis=0, keepdims=True)
            new = st * w_h - sa * ka_h + v_t[row:row + 1, :] * k_h
            so_ref[e, i] = new
            yt[row:row + 1, :] = jnp.sum(new * r_h, axis=0, keepdims=True)
    y_ref[...] = _rwkv_head_norm(yt[...].T, r, k, v, rk_ref, lnw_ref, lnb_ref)


def _params(sem):
    return pltpu.CompilerParams(dimension_semantics=sem, vmem_limit_bytes=VMEM_LIMIT)


def _call_name(kernel, n):
    fn = getattr(kernel, "func", kernel)
    return f"{fn.__name__.strip('_')}_{n}"


def _const_spec(shape, grid_rank):
    nd = len(shape)
    if grid_rank == 1:
        return pl.BlockSpec(shape, lambda i: (0,) * nd, pipeline_mode=pl.Buffered(1))
    return pl.BlockSpec(shape, lambda b, c: (0,) * nd, pipeline_mode=pl.Buffered(1))


def _row_tile(m, tile):
    return tile if m % tile == 0 else m


def _layer_spec(stacked, layer):
    nd = stacked.ndim
    return pl.BlockSpec((None,) + stacked.shape[1:], lambda i: (layer,) + (0,) * (nd - 1),
                        pipeline_mode=pl.Buffered(1))


def _rowwise_call(kernel, row_ins, const_ins, outs, m, tile=ROW_TILE, extra_ins=()):
    tm = _row_tile(m, tile)
    rows = lambda n: pl.BlockSpec((tm, n), lambda i: (i, 0))
    consts = [c if isinstance(c, tuple) else (c, None) for c in const_ins]
    return pl.pallas_call(
        kernel,
        name=_call_name(kernel, m),
        grid=(m // tm,),
        in_specs=([rows(a.shape[1]) for a in row_ins] + [spec for _, spec in extra_ins]
                  + [_const_spec(a.shape, 1) if layer is None else _layer_spec(a, layer) for a, layer in consts]),
        out_specs=[rows(n) for n, _ in outs],
        out_shape=[jax.ShapeDtypeStruct((m, n), dt) for n, dt in outs],
        compiler_params=_params(("parallel",)),
    )(*row_ins, *[a for a, _ in extra_ins], *[a for a, _ in consts])


def _scan_call(kernel, seq_ins, state_ins, const_ins, seq_out_widths, state_out_shapes, scratch):
    bsz, t = seq_ins[0].shape[:2]
    seq = lambda n: pl.BlockSpec((None, SCAN_CHUNK, n), lambda b, c: (b, c, 0))
    per_seq = lambda shape: pl.BlockSpec((None,) + tuple(shape[1:]), lambda b, c: (b,) + (0,) * (len(shape) - 1))
    return pl.pallas_call(
        kernel,
        name=_call_name(kernel, bsz),
        grid=(bsz, t // SCAN_CHUNK),
        in_specs=([seq(a.shape[2]) for a in seq_ins] + [per_seq(a.shape) for a in state_ins]
                  + [_const_spec(a.shape, 2) for a in const_ins]),
        out_specs=[seq(n) for n in seq_out_widths] + [per_seq(s) for s in state_out_shapes],
        out_shape=([jax.ShapeDtypeStruct((bsz, t, n), seq_ins[0].dtype) for n in seq_out_widths]
                   + [jax.ShapeDtypeStruct(s, F32) for s in state_out_shapes]),
        scratch_shapes=[pltpu.VMEM(shape, dt) for shape, dt in scratch],
        compiler_params=_params(("parallel", "arbitrary")),
    )(*seq_ins, *state_ins, *const_ins)


def _state_ring(kernel, n_rows, nsteps):
    def body(*refs):
        *main, ring, sems = refs
        hbm = main[n_rows]
        i = pl.program_id(0)

        def copy(j):
            slot = j % STATE_SLOTS
            return pltpu.make_async_copy(hbm.at[pl.ds(j * STEP_SEQS, STEP_SEQS)], ring.at[slot], sems.at[slot])

        @pl.when(i == 0)
        def _():
            for j in range(min(STATE_SLOTS - 1, nsteps)):
                copy(j).start()

        @pl.when(i + STATE_SLOTS - 1 < nsteps)
        def _():
            copy(i + STATE_SLOTS - 1).start()

        copy(i).wait()
        kernel(*main[:n_rows], ring.at[i % STATE_SLOTS], *main[n_rows + 1:])
    return body


def _step_call(kernel, row_ins, state_in, const_ins, row_out_widths, state_out_shapes):
    bsz = row_ins[0].shape[0]
    nsteps = bsz // STEP_SEQS
    rows = lambda n: pl.BlockSpec((STEP_SEQS, n), lambda i: (i, 0))
    state = lambda shape: pl.BlockSpec((STEP_SEQS,) + tuple(shape[1:]), lambda i: (i, 0, 0))
    return pl.pallas_call(
        _state_ring(kernel, len(row_ins), nsteps),
        name=_call_name(kernel, bsz),
        grid=(nsteps,),
        in_specs=([rows(a.shape[1]) for a in row_ins] + [pl.BlockSpec(memory_space=pl.ANY)]
                  + [_const_spec(a.shape, 1) for a in const_ins]),
        out_specs=[rows(n) for n in row_out_widths] + [state(s) for s in state_out_shapes],
        out_shape=([jax.ShapeDtypeStruct((bsz, n), F32) for n in row_out_widths]
                   + [jax.ShapeDtypeStruct(s, F32) for s in state_out_shapes]),
        scratch_shapes=[pltpu.VMEM((STATE_SLOTS, STEP_SEQS) + tuple(state_in.shape[1:]), F32),
                        pltpu.SemaphoreType.DMA((STATE_SLOTS,))],
        compiler_params=_params(("arbitrary",)),
    )(*row_ins, state_in, *const_ins)


def _rwkv_step_call(rows, wkv, rk, lnw, lnb):
    bsz = wkv.shape[0]
    assert bsz == LANES, "the single-token RWKV kernel keeps the sequences on the lanes"
    n = RWKV_HEAD_DIM
    pair = pl.BlockSpec((bsz, LANES), lambda i: (0, i))
    vec = pl.BlockSpec((1, LANES), lambda i: (0, i))
    state = pl.BlockSpec((2, n, n, bsz), lambda i: (i, 0, 0, 0))
    y, new = pl.pallas_call(
        _rwkv_step_kernel,
        name=_call_name(_rwkv_step_kernel, bsz),
        grid=(RWKV_HEADS // 2,),
        in_specs=[pair] * 6 + [state] + [vec] * 3,
        out_specs=[pair, state],
        out_shape=[jax.ShapeDtypeStruct((bsz, RWKV_D), F32), jax.ShapeDtypeStruct((RWKV_HEADS, n, n, bsz), F32)],
        scratch_shapes=[pltpu.VMEM((LANES, bsz), F32)],
        compiler_params=_params(("parallel",)),
    )(*rows, jnp.transpose(wkv, (1, 2, 3, 0)), rk, lnw, lnb)
    return y, jnp.transpose(new, (3, 0, 1, 2))


def _row(v):
    return v.reshape(1, -1).astype(F32)


def _pad_cols(w, width):
    return jnp.pad(w, ((0, 0), (0, width - w.shape[1])))


def _pad_rows_to(w, rows):
    return jnp.pad(w, ((0, rows - w.shape[0]), (0, 0)))


def _blockdiag_slabs(w, scale=1.0):
    hd = MLSTM_HEAD_DIM
    rows = (w * scale).reshape(-1, hd, QKV_BLOCK)
    col = jnp.arange(hd)
    pick = (col[None, :] % QKV_BLOCK == jnp.arange(QKV_BLOCK)[:, None]).astype(w.dtype)
    tiled = jnp.einsum('srd,dc->src', rows, pick, precision=lax.Precision.HIGHEST)
    same_block = col[:, None] // QKV_BLOCK == col[None, :] // QKV_BLOCK
    return jnp.where(same_block, tiled, 0.0).astype(BF16)


def _head_spread():
    ch = jnp.arange(SSD_D_INNER) // SSD_HEAD_DIM
    return (jnp.arange(LANES)[:, None] == ch[None, :]).astype(BF16)


def _mixer0_consts(p):
    ssd = [p['conv_w'][:, :XBC_DIM], p['conv_b'][:, :XBC_DIM], p['small_bias'], p['ssd_a_log'], p['ssd_d'],
           p['ssd_norm'], p['segt']]
    ml = [p['conv_w'][:, XBC_DIM:], p['conv_b'][:, XBC_DIM:], p['small_bias'], p['ml_wq'], p['ml_wk'],
          p['ml_wv'], p['ml_norm'], p['ml_skip']]
    return ssd, ml


def _mixer0_chunked(xbc, xm, z, o_pre, small, conv, ssm, mc, mn, mm, p, bsz, t):
    seq = lambda a: a.reshape(bsz, t, -1)
    tail = jnp.pad(conv, ((0, 0), (SUBLANES - (CONV_W - 1), 0), (0, 0)))
    ssd_c, ml_c = _mixer0_consts(p)
    y_ssd, new_ssm = _scan_call(
        _ssd_kernel, [seq(xbc), seq(z), seq(small)],
        [tail[:, :, :XBC_DIM], ssm.reshape(bsz, SSD_D_INNER, SSD_D_STATE)], ssd_c,
        [SSD_D_INNER], [(bsz, SSD_D_INNER, SSD_D_STATE)],
        [((SUBLANES, XBC_DIM), F32), ((SSD_D_INNER, SSD_D_STATE), F32)])
    hm, new_c, new_n, new_m = _scan_call(
        _mlstm_kernel, [seq(xm), seq(o_pre), seq(small)],
        [tail[:, :, XBC_DIM:], mc.reshape(bsz, MLSTM_D_INNER, MLSTM_HEAD_DIM), mn.reshape(bsz, 1, MLSTM_D_INNER),
         _pad_cols(mm, LANES).reshape(bsz, 1, LANES)], ml_c,
        [MLSTM_D_INNER], [(bsz, MLSTM_D_INNER, MLSTM_HEAD_DIM), (bsz, 1, MLSTM_D_INNER), (bsz, 1, LANES)],
        [((SUBLANES, MLSTM_D_INNER), F32), ((MLSTM_D_INNER, MLSTM_HEAD_DIM), F32),
         ((1, MLSTM_D_INNER), F32), ((1, LANES), F32)])
    flat = lambda a: a.reshape(bsz * t, -1)
    return flat(y_ssd), flat(hm), new_ssm, new_c, new_n, new_m.reshape(bsz, LANES)


def _mixer0_step(xbc, xm, z, o_pre, small, conv, ssm, mc, mn, mm, p, bsz):
    ssd_c, ml_c = _mixer0_consts(p)
    tails = [conv[:, j, :] for j in range(CONV_W - 1)]
    y_ssd, new_ssm = _step_call(
        _ssd_step_kernel, [xbc] + [tl[:, :XBC_DIM] for tl in tails] + [z, small],
        ssm.reshape(bsz, SSD_D_INNER, SSD_D_STATE), ssd_c, [SSD_D_INNER], [(bsz, SSD_D_INNER, SSD_D_STATE)])
    hm, new_n, new_m, new_c = _step_call(
        _mlstm_step_kernel, [xm] + [tl[:, XBC_DIM:] for tl in tails]
        + [o_pre, small, mn.reshape(bsz, MLSTM_D_INNER), _pad_cols(mm, LANES)],
        mc.reshape(bsz, MLSTM_D_INNER, MLSTM_HEAD_DIM), ml_c,
        [MLSTM_D_INNER, MLSTM_D_INNER, LANES], [(bsz, MLSTM_D_INNER, MLSTM_HEAD_DIM)])
    return y_ssd, hm, new_ssm, new_c, new_n, new_m


def _trunk(x, conv, ssm, mc, mn, mm, shift, wkv, p):
    bsz, t, d = x.shape
    m = bsz * t
    step = t == 1
    assert step or t % SCAN_CHUNK == 0, "a group is either single-token or a multiple of the scan chunk"
    assert not step or bsz % STEP_SEQS == 0
    act = F32 if step else BF16
    x2 = x.reshape(m, d)

    z, xbc, xm, o_pre, small = _rowwise_call(
        _in0_kernel, [x2], [p['norm_mix0'], p['w_in0']],
        [(SSD_D_INNER, act), (XBC_DIM, act), (MLSTM_D_INNER, act), (MLSTM_D_INNER, act), (LANES, F32)], m,
        tile=ROW_TILE)
    last_rows = lambda a: a.reshape(bsz, t, -1)[:, -(CONV_W - 1):].astype(F32)
    conv_in_tail = jnp.concatenate([last_rows(xbc), last_rows(xm)], axis=-1)
    new_conv = jnp.concatenate([conv, conv_in_tail], axis=1)[:, -(CONV_W - 1):]
    if step:
        y_ssd, hm, new_ssm, new_c, new_n, new_m = _mixer0_step(xbc, xm, z, o_pre, small, conv, ssm, mc, mn, mm, p, bsz)
    else:
        y_ssd, hm, new_ssm, new_c, new_n, new_m = _mixer0_chunked(xbc, xm, z, o_pre, small, conv, ssm, mc, mn, mm,
                                                                  p, bsz, t)
    xn1, x2b = _rowwise_call(
        _mix0_ffn_kernel, [y_ssd, hm, x2],
        [p['w_out0'], p['norm_ffn0'], (p['ffn_gu'], 0), (p['ffn_d'], 0), p['norm_mix1']], [(d, act), (d, F32)], m,
        tile=ROW_TILE)

    rw_in_c = [p['rw_mu'], p['rw_wr'], p['rw_wk'], p['rw_wv'], p['rw_w1'], p['rw_w2'], p['rw_a1'], p['rw_a2'],
               p['rw_g1'], p['rw_g2'], p['rw_w0'], p['rw_a0'], p['rw_k_k'], p['rw_k_a']]
    rw_outs = [(d, act), (d, F32)] + [(d, act)] * 5
    if step:
        rw = _rowwise_call(_rw_in_kernel, [xn1, shift], rw_in_c, rw_outs, m)
    else:
        tm = _row_tile(t, ROW_TILE)
        tiles_per_seq = t // tm
        prev_spec = pl.BlockSpec((PREV_ROWS, d), lambda i: (jnp.maximum(i * (tm // PREV_ROWS) - 1, 0), 0))
        shift_spec = pl.BlockSpec((None, 1, d), lambda i: (i // tiles_per_seq, 0, 0))
        rw = _rowwise_call(functools.partial(_rw_in_shift_kernel, tiles_per_seq=tiles_per_seq), [xn1], rw_in_c,
                           rw_outs, m, tile=tm, extra_ins=[(xn1, prev_spec), (shift.reshape(bsz, 1, d), shift_spec)])
    g = rw[6]
    rw_c = [p['rw_r_k'], p['rw_ln_w'], p['rw_ln_b']]
    if step:
        y_rw, new_wkv = _rwkv_step_call(list(rw[:6]), wkv, *rw_c)
    else:
        wkv3 = wkv.reshape(bsz, RWKV_D, RWKV_HEAD_DIM)
        y_rw, new_wkv = _scan_call(
            _rwkv_kernel, [a.reshape(bsz, t, d) for a in rw[:6]], [wkv3], rw_c,
            [d], [wkv3.shape], [((RWKV_D, 2 * RWKV_HEAD_DIM), F32)])
        y_rw = y_rw.reshape(m, d)
    new_wkv = new_wkv.reshape(wkv.shape)
    (y,) = _rowwise_call(
        _mix1_ffn_kernel, [y_rw, g, x2b],
        [p['rw_wo'], p['norm_ffn1'], (p['ffn_gu'], 1), (p['ffn_d'], 1), p['norm_final']], [(d, F32)], m, tile=ROW_TILE)

    states = (new_conv, new_ssm.reshape(ssm.shape), new_c.reshape(mc.shape), new_n.reshape(mn.shape),
              new_m[:, :MLSTM_HEADS], xn1.reshape(bsz, t, d)[:, -1].astype(F32), new_wkv)
    return y.reshape(bsz, t, d), states


def kernel(x_prompt, x_sample, state_conv, state_ssm, state_mlstm_c, state_mlstm_n, state_mlstm_m, state_shift, state_wkv, norm_mix, norm_ffn, norm_final, w_in0, conv_w, conv_b, ssd_dt_bias, ssd_a_log, ssd_d, ssd_norm, ml_wq, ml_wk, ml_wv, ml_i_bias, ml_f_bias, ml_norm, ml_skip, w_out0, rw_mu, rw_wr, rw_wk, rw_wv, rw_wo, rw_w0, rw_w1, rw_w2, rw_a0, rw_a1, rw_a2, rw_g1, rw_g2, rw_k_k, rw_k_a, rw_r_k, rw_ln_w, rw_ln_b, ffn_w_gate_up, ffn_w_down):
    assert norm_mix.shape[0] == 2 and w_in0.shape[0] == 1 and rw_wr.shape[0] == 1, "two layers: SSD|mLSTM then RWKV-7"
    s1 = SSD_D_INNER
    s2 = s1 + XBC_DIM + MLSTM_D_INNER
    s3 = s2 + SSD_HEADS
    s4 = s3 + MLSTM_D_INNER
    s5 = s4 + MLSTM_HEADS
    w0 = w_in0[0]
    small_w = _pad_cols(jnp.concatenate([w0[:, s2:s3], w0[:, s4:s5], w0[:, s5:]], axis=1), LANES)
    lora = lambda w1, w2, width: (_pad_cols(w1, width).astype(BF16), _pad_rows_to(w2, width).astype(BF16))
    rw_w1p, rw_w2p = lora(rw_w1[0], rw_w2[0], LANES)
    rw_a1p, rw_a2p = lora(rw_a1[0], rw_a2[0], LANES)
    rw_g1p, rw_g2p = lora(rw_g1[0], rw_g2[0], 2 * LANES)
    p = dict(
        norm_mix0=_row(norm_mix[0]), norm_mix1=_row(norm_mix[1]), norm_ffn0=_row(norm_ffn[0]),
        norm_ffn1=_row(norm_ffn[1]), norm_final=_row(norm_final),
        w_in0=jnp.concatenate([w0[:, :s2], w0[:, s3:s4], small_w], axis=1).astype(BF16),
        conv_w=conv_w[0], conv_b=_row(conv_b[0]),
        small_bias=_pad_cols(jnp.concatenate([_row(ssd_dt_bias[0]), _row(ml_i_bias[0]), _row(ml_f_bias[0])], axis=1),
                             LANES),
        ssd_a_log=_pad_cols(_row(ssd_a_log[0]), LANES), ssd_d=_row(jnp.repeat(ssd_d[0], SSD_HEAD_DIM)),
        ssd_norm=_row(ssd_norm[0]),
        ml_wq=_blockdiag_slabs(ml_wq[0]), ml_wk=_blockdiag_slabs(ml_wk[0], MLSTM_HEAD_DIM ** -0.5),
        ml_wv=_blockdiag_slabs(ml_wv[0]), ml_norm=_row(ml_norm[0]), ml_skip=_row(ml_skip[0]),
        w_out0=w_out0[0].astype(BF16),
        rw_mu=_pad_rows_to(rw_mu[0], SUBLANES),
        rw_wr=rw_wr[0].astype(BF16), rw_wk=rw_wk[0].astype(BF16), rw_wv=rw_wv[0].astype(BF16),
        rw_wo=rw_wo[0].astype(BF16), rw_w0=_row(rw_w0[0]), rw_a0=_row(rw_a0[0]),
        rw_w1=rw_w1p, rw_w2=rw_w2p, rw_a1=rw_a1p, rw_a2=rw_a2p, rw_g1=rw_g1p, rw_g2=rw_g2p,
        rw_k_k=_row(rw_k_k[0]), rw_k_a=_row(rw_k_a[0]), rw_r_k=_row(rw_r_k[0]),
        rw_ln_w=_row(rw_ln_w[0]), rw_ln_b=_row(rw_ln_b[0]),
        ffn_gu=ffn_w_gate_up.astype(BF16), ffn_d=ffn_w_down.astype(BF16),
        segt=_head_spread(),
    )
    bp = x_prompt.shape[0]
    zeros = lambda s: jnp.zeros((bp,) + s.shape[2:], F32)
    y_p, st_p = _trunk(x_prompt, zeros(state_conv), zeros(state_ssm), zeros(state_mlstm_c),
                       zeros(state_mlstm_n), zeros(state_mlstm_m), zeros(state_shift), zeros(state_wkv), p)
    y_s, st_s = _trunk(x_sample, state_conv[0], state_ssm[0], state_mlstm_c[0], state_mlstm_n[0],
                       state_mlstm_m[0], state_shift[0], state_wkv[0], p)
    out = [y_p, y_s]
    for a, b in zip(st_p, st_s):
        out += [a[None], b[None]]
    return tuple(out)
```

```python
import functools

import jax
import jax.numpy as jnp
from jax import lax
from jax.experimental import pallas as pl
from jax.experimental.pallas import tpu as pltpu

F32 = jnp.float32
BF16 = jnp.bfloat16

D_MODEL = 1024
NORM_EPS = 1e-5
CONV_W = 4
SSD_HEADS = 16
SSD_HEAD_DIM = 64
SSD_GROUPS = 2
SSD_D_STATE = 128
SSD_D_INNER = SSD_HEADS * SSD_HEAD_DIM
SSD_GROUP_W = SSD_D_INNER // SSD_GROUPS
XBC_DIM = SSD_D_INNER + 2 * SSD_GROUPS * SSD_D_STATE
MLSTM_HEADS = 4
MLSTM_HEAD_DIM = 256
MLSTM_D_INNER = MLSTM_HEADS * MLSTM_HEAD_DIM
QKV_BLOCK = 4
RWKV_HEADS = 16
RWKV_HEAD_DIM = 64
RWKV_D = RWKV_HEADS * RWKV_HEAD_DIM
RWKV_LN_EPS = 64e-5
D_FF = 2816
FF_CHUNK = 256
LANES = 128
SUBLANES = 8
SMALL_DT, SMALL_I, SMALL_F = 0, 16, 20
NEG = -1e30
EXP_NEG_HALF = 0.6065306597126334
SCAN_CHUNK = 128
STEP_SEQS = SUBLANES
STATE_SLOTS = 3
ROW_TILE = 512
PREV_ROWS = 16
VMEM_LIMIT = 56 * 1024 * 1024


def _dot(a, b):
    return jnp.dot(a.astype(BF16), b.astype(BF16), preferred_element_type=F32)


def _dot_nt(a, b):
    return lax.dot_general(a.astype(BF16), b.astype(BF16), (((1,), (1,)), ((), ())),
                           preferred_element_type=F32)


def _dot_tn(a, b):
    return lax.dot_general(a.astype(BF16), b.astype(BF16), (((0,), (0,)), ((), ())),
                           preferred_element_type=F32)


def _split(x, terms):
    parts = []
    rem = x
    for _ in range(terms):
        p = rem.astype(BF16)
        parts.append(p)
        rem = rem - p.astype(F32)
    return parts


def _dot_exact_rhs(x, m, terms=2):
    out = None
    for p in _split(x, terms):
        t = jnp.dot(p, m, preferred_element_type=F32)
        out = t if out is None else out + t
    return out


def _tri_cumsum(tri, x, terms=3):
    out = None
    for p in _split(x, terms):
        t = jnp.dot(tri, p, preferred_element_type=F32)
        out = t if out is None else out + t
    return out


def _rms_rows(x, g):
    return x * lax.rsqrt(jnp.mean(x * x, axis=-1, keepdims=True) + NORM_EPS) * g


def _group_rms(x, width):
    outs = []
    for lo in range(0, x.shape[1], width):
        grp = x[:, lo:lo + width]
        outs.append(grp * lax.rsqrt(jnp.mean(grp * grp, axis=-1, keepdims=True) + NORM_EPS))
    return jnp.concatenate(outs, axis=1)


def _sigmoid(x):
    return 0.5 * jnp.tanh(0.5 * x) + 0.5


def _silu(x):
    h = 0.5 * x
    return h * jnp.tanh(h) + h


def _softplus(x):
    return jnp.maximum(x, 0.0) + jnp.log(1.0 + jnp.exp(-jnp.abs(x)))


def _log_sigmoid(x):
    return jnp.minimum(x, 0.0) - jnp.log(1.0 + jnp.exp(-jnp.abs(x)))


def _tri_masks(n):
    row = lax.broadcasted_iota(jnp.int32, (n, n), 0)
    col = lax.broadcasted_iota(jnp.int32, (n, n), 1)
    return row >= col, row > col


def _chunk_conv(tail, u_ref, conv0_ref, cw_ref, cb_ref, first):
    assert u_ref.dtype == BF16
    rows = u_ref.shape[0]
    taps = CONV_W - 1

    @pl.when(first)
    def _():
        tail[...] = conv0_ref[...]

    ub = u_ref[...]
    u = ub.astype(F32)
    out_row = lax.broadcasted_iota(jnp.int32, (taps * rows, rows), 0)
    src_row = lax.broadcasted_iota(jnp.int32, (taps * rows, rows), 1)
    back = taps - out_row // rows
    shift = jnp.where(src_row == out_row % rows - back, 1.0, 0.0).astype(BF16)
    shifted = jnp.dot(shift, ub, preferred_element_type=F32)
    conv = cb_ref[...] + u * cw_ref[taps:CONV_W, :]
    window = jnp.concatenate([tail[...], jnp.zeros((SUBLANES, u.shape[1]), F32)], axis=0)
    edge = jnp.zeros((SUBLANES, u.shape[1]), F32)
    for j in range(taps):
        conv = conv + shifted[j * rows:(j + 1) * rows, :] * cw_ref[j:j + 1, :]
        edge = edge + pltpu.roll(window, shift=taps - j, axis=0)[SUBLANES:2 * SUBLANES, :] * cw_ref[j:j + 1, :]
    tail[...] = u[rows - SUBLANES:rows, :]
    return jnp.concatenate([conv[0:SUBLANES, :] + edge, conv[SUBLANES:rows, :]], axis=0), u


def _step_conv(u_ref, tail_refs, cw_ref, cb_ref):
    conv = cb_ref[...] + u_ref[...] * cw_ref[CONV_W - 1:CONV_W, :]
    for j, t in enumerate(tail_refs):
        conv = conv + t[...] * cw_ref[j:j + 1, :]
    return conv


def _blockdiag_qkv(act, raw, wq_ref, wk_ref, wv_ref):
    hd = MLSTM_HEAD_DIM
    cat = lambda x, w: jnp.concatenate(
        [_dot(x[:, s * hd:(s + 1) * hd], w[s]) for s in range(MLSTM_HEADS)], axis=1)
    return cat(act, wq_ref), cat(act, wk_ref), cat(raw, wv_ref)


def _head_sum(x):
    half = lax.broadcasted_iota(jnp.int32, (x.shape[0], LANES), 1) < RWKV_HEAD_DIM
    outs = []
    for lo in range(0, x.shape[1], LANES):
        xp = x[:, lo:lo + LANES]
        s0 = jnp.sum(jnp.where(half, xp, 0.0), axis=1, keepdims=True)
        s1 = jnp.sum(jnp.where(half, 0.0, xp), axis=1, keepdims=True)
        outs.append(jnp.where(half, s0, s1))
    return jnp.concatenate(outs, axis=1)


def _rwkv_head_norm(y, r, k, v, rk_ref, lnw_ref, lnb_ref):
    inv_n = 1.0 / RWKV_HEAD_DIM
    dlt = y - _head_sum(y) * inv_n
    yn = dlt * lax.rsqrt(_head_sum(dlt * dlt) * inv_n + RWKV_LN_EPS)
    return yn * lnw_ref[...] + lnb_ref[...] + _head_sum(r * k * rk_ref[...]) * v


def _in0_kernel(x_ref, g_ref, w_ref, z_ref, xbc_ref, xm_ref, o_ref, sm_ref):
    xn = _rms_rows(x_ref[...], g_ref[...]).astype(BF16)
    off = 0
    for ref in (z_ref, xbc_ref, xm_ref, o_ref, sm_ref):
        n = ref.shape[-1]
        ref[...] = jnp.dot(xn, w_ref[:, off:off + n], preferred_element_type=F32).astype(ref.dtype)
        off += n


def _swiglu_residual(x, g_ref, wgu_ref, wd_ref):
    xn = _rms_rows(x, g_ref[...]).astype(BF16)
    acc = None
    for c in range(D_FF // FF_CHUNK):
        lo = c * FF_CHUNK
        gate = jnp.dot(xn, wgu_ref[:, lo:lo + FF_CHUNK], preferred_element_type=F32)
        up = jnp.dot(xn, wgu_ref[:, D_FF + lo:D_FF + lo + FF_CHUNK], preferred_element_type=F32)
        part = _dot(_silu(gate) * up, wd_ref[lo:lo + FF_CHUNK, :])
        acc = part if acc is None else acc + part
    return x + acc


def _mix0_ffn_kernel(y_ref, h_ref, x_ref, wo_ref, g_ref, wgu_ref, wd_ref, pg_ref, xn_ref, xo_ref):
    mixed = jnp.concatenate([y_ref[...].astype(BF16), h_ref[...].astype(BF16)], axis=1)
    x1 = x_ref[...] + jnp.dot(mixed, wo_ref[...], preferred_element_type=F32)
    x2 = _swiglu_residual(x1, g_ref, wgu_ref, wd_ref)
    xo_ref[...] = x2
    xn_ref[...] = _rms_rows(x2, pg_ref[...]).astype(xn_ref.dtype)


def _mix1_ffn_kernel(y_ref, gate_ref, x_ref, wo_ref, g_ref, wgu_ref, wd_ref, pg_ref, o_ref):
    x3 = x_ref[...] + _dot(y_ref[...].astype(F32) * gate_ref[...].astype(F32), wo_ref[...])
    o_ref[...] = _rms_rows(_swiglu_residual(x3, g_ref, wgu_ref, wd_ref), pg_ref[...])


def _rw_in_body(xn, xp, mu_ref, wr_ref, wk_ref, wv_ref, w1_ref, w2_ref, a1_ref, a2_ref,
                g1_ref, g2_ref, w0_ref, a0_ref, kk_w_ref, ka_w_ref,
                r_ref, lw_ref, k_ref, v_ref, kk_ref, a_ref, g_ref):
    xx = xp - xn
    xr, xw, xk, xv, xa, xg = (xn + xx * mu_ref[c:c + 1, :] for c in range(6))
    r_ref[...] = _dot(xr, wr_ref[...]).astype(r_ref.dtype)
    k = _dot(xk, wk_ref[...])
    v_ref[...] = _dot(xv, wv_ref[...]).astype(v_ref.dtype)
    z = w0_ref[...] + _dot(jnp.tanh(_dot(xw, w1_ref[...])), w2_ref[...])
    lw_ref[...] = -EXP_NEG_HALF * _sigmoid(z)
    a = _sigmoid(a0_ref[...] + _dot(_dot(xa, a1_ref[...]), a2_ref[...]))
    a_ref[...] = a.astype(a_ref.dtype)
    g_ref[...] = _dot(_sigmoid(_dot(xg, g1_ref[...])), g2_ref[...]).astype(g_ref.dtype)
    kk = k * kk_w_ref[...]
    ss = _head_sum(kk * kk)
    kk_ref[...] = (kk * lax.rsqrt(jnp.maximum(ss, 1e-24))).astype(kk_ref.dtype)
    k_ref[...] = (k * (1.0 + (a - 1.0) * ka_w_ref[...])).astype(k_ref.dtype)


def _rw_in_kernel(xn_ref, xp_ref, *refs):
    _rw_in_body(xn_ref[...].astype(F32), xp_ref[...].astype(F32), *refs)


def _rw_in_shift_kernel(xn_ref, prev_ref, shift_ref, *refs, tiles_per_seq):
    xn = xn_ref[...].astype(F32)
    starts_seq = pl.program_id(0) % tiles_per_seq == 0
    last = prev_ref.shape[0] - 1
    prev = jnp.where(starts_seq, shift_ref[...], prev_ref[last:last + 1, :].astype(F32))
    row = lax.broadcasted_iota(jnp.int32, xn.shape, 0)
    xp = jnp.where(row == 0, prev, pltpu.roll(xn, shift=1, axis=0))
    _rw_in_body(xn, xp, *refs)


def _ssd_kernel(xbc_ref, z_ref, sm_ref, conv0_ref, st0_ref, cw_ref, cb_ref, bias_ref, alog_ref,
                dexp_ref, nw_ref, segt_ref, y_ref, st_ref, ext, state):
    c = pl.program_id(1)
    first = c == 0
    rows = xbc_ref.shape[0]

    @pl.when(first)
    def _():
        state[...] = st0_ref[...]

    conv, _ = _chunk_conv(ext, xbc_ref, conv0_ref, cw_ref, cb_ref, first)
    act = _silu(conv)
    xs = act[:, 0:SSD_D_INNER]
    gn = SSD_GROUPS * SSD_D_STATE
    bm = act[:, SSD_D_INNER:SSD_D_INNER + gn]
    cm = act[:, SSD_D_INNER + gn:SSD_D_INNER + 2 * gn]

    tril, _ = _tri_masks(rows)
    tri = jnp.where(tril, 1.0, 0.0).astype(BF16)
    lane = lax.broadcasted_iota(jnp.int32, (rows, LANES), 1)
    head_lane = (lane >= SMALL_DT) & (lane < SMALL_DT + SSD_HEADS)
    dt = jnp.where(head_lane, _softplus(sm_ref[...] + bias_ref[...]), 0.0)
    acum = _tri_cumsum(tri, dt * (-jnp.exp(alog_ref[...])))
    acum_t = acum.T
    dt_t = dt.T
    segt = segt_ref[...]
    eac_x = _dot_exact_rhs(jnp.exp(acum), segt, terms=1)
    wend_x = _dot_exact_rhs(jnp.exp(acum[rows - 1:rows, :] - acum) * dt, segt, terms=1)
    xw = xs * wend_x

    half = lane < SSD_HEAD_DIM
    heads_per_group = SSD_HEADS // SSD_GROUPS
    grp = lambda x, g: x[:, g * SSD_D_STATE:(g + 1) * SSD_D_STATE]
    cbs = [_dot_nt(grp(cm, g), grp(bm, g)) for g in range(SSD_GROUPS)]
    decays = [jnp.exp(jnp.where(tril, acum[:, j:j + 1] - acum_t[j:j + 1, :], NEG)) for j in range(SSD_HEADS)]
    wts = [decays[j] * cbs[j // heads_per_group] * dt_t[j:j + 1, :] for j in range(SSD_HEADS)]
    lhs = [jnp.concatenate([wts[2 * p], wts[2 * p + 1]], axis=1).astype(BF16) for p in range(SSD_HEADS // 2)]
    rhs = []
    for p in range(SSD_HEADS // 2):
        xp = xs[:, p * LANES:(p + 1) * LANES]
        rhs.append(jnp.concatenate([jnp.where(half, xp, 0.0), jnp.where(half, 0.0, xp)], axis=0).astype(BF16))
    pairs = [_dot(l, r) for l, r in zip(lhs, rhs)]
    y = jnp.concatenate(pairs, axis=1)

    st = state[...]
    gw = SSD_GROUP_W
    y_state = jnp.concatenate(
        [_dot_nt(cm[:, g * SSD_D_STATE:(g + 1) * SSD_D_STATE], st[g * gw:(g + 1) * gw, :])
         for g in range(SSD_GROUPS)], axis=1)
    y = y + eac_x * y_state + dexp_ref[...] * xs
    y_ref[...] = (_group_rms(y * _silu(z_ref[...].astype(F32)), gw) * nw_ref[...]).astype(y_ref.dtype)

    dcol = jnp.exp(acum_t[:, rows - 1:rows])
    for g in range(SSD_GROUPS):
        upd = _dot_tn(xw[:, g * gw:(g + 1) * gw], bm[:, g * SSD_D_STATE:(g + 1) * SSD_D_STATE])
        for h in range(heads_per_group):
            j = g * heads_per_group + h
            lo = j * SSD_HEAD_DIM
            state[lo:lo + SSD_HEAD_DIM, :] = (st[lo:lo + SSD_HEAD_DIM, :] * dcol[j:j + 1, :]
                                              + upd[h * SSD_HEAD_DIM:(h + 1) * SSD_HEAD_DIM, :])

    @pl.when(c == pl.num_programs(1) - 1)
    def _():
        st_ref[...] = state[...]


def _mlstm_kernel(xm_ref, o_ref, sm_ref, conv0_ref, c0_ref, n0_ref, m0_ref, cw_ref, cb_ref, bias_ref,
                  wq_ref, wk_ref, wv_ref, nw_ref, skip_ref, h_ref, c_ref, n_ref, m_ref,
                  ext, cst, nst, mst):
    c = pl.program_id(1)
    first = c == 0
    rows = xm_ref.shape[0]

    @pl.when(first)
    def _():
        cst[...] = c0_ref[...]
        nst[...] = n0_ref[...]
        mst[...] = m0_ref[...]

    conv, raw = _chunk_conv(ext, xm_ref, conv0_ref, cw_ref, cb_ref, first)
    act = _silu(conv)
    q, k, v = _blockdiag_qkv(act, raw, wq_ref, wk_ref, wv_ref)

    tril, _ = _tri_masks(rows)
    tri = jnp.where(tril, 1.0, 0.0).astype(BF16)
    logi = sm_ref[...] + bias_ref[...]
    bcum = _tri_cumsum(tri, _log_sigmoid(logi))
    bcum_t = bcum.T
    logi_t = logi.T
    m_all = mst[...]
    lane1 = lax.broadcasted_iota(jnp.int32, (1, LANES), 1)
    hd = MLSTM_HEAD_DIM

    heads = range(MLSTM_HEADS)
    cols = [slice(h * hd, (h + 1) * hd) for h in heads]
    qs, ks, vs = ([t[:, c] for c in cols] for t in (q, k, v))
    qbs, vbs = [x.astype(BF16) for x in qs], [x.astype(BF16) for x in vs]
    bcs = [bcum[:, SMALL_F + h:SMALL_F + h + 1] for h in heads]
    lis = [logi[:, SMALL_I + h:SMALL_I + h + 1] for h in heads]
    m_hs = [m_all[:, h:h + 1] for h in heads]
    c_hs = [cst[c, :] for c in cols]
    n_hs = [nst[:, c] for c in cols]
    dlogs = [jnp.where(tril, bcs[h] - bcum_t[SMALL_F + h:SMALL_F + h + 1, :]
                       + logi_t[SMALL_I + h:SMALL_I + h + 1, :], NEG) for h in heads]
    inters = [bcs[h] + m_hs[h] for h in heads]
    m_ts = [jnp.maximum(inters[h], jnp.max(dlogs[h], axis=1, keepdims=True)) for h in heads]
    qks = [_dot_nt(qbs[h], ks[h]) for h in heads]
    qcs = [_dot(qbs[h], c_hs[h]) for h in heads]
    ss = [qks[h] * jnp.exp(dlogs[h] - m_ts[h]) for h in heads]
    w_inters = [jnp.exp(inters[h] - m_ts[h]) for h in heads]
    nums = [_dot(ss[h], vbs[h]) + w_inters[h] * qcs[h] for h in heads]
    dens = [jnp.sum(ss[h], axis=1, keepdims=True)
            + w_inters[h] * jnp.sum(qs[h] * n_hs[h], axis=1, keepdims=True) for h in heads]
    hs = [nums[h] * (1.0 / jnp.maximum(jnp.abs(dens[h]), jnp.exp(-m_ts[h]))) for h in heads]
    b_ends = [bcs[h][rows - 1:rows, :] for h in heads]
    wlogs = [b_ends[h] - bcs[h] + lis[h] for h in heads]
    m_news = [jnp.maximum(b_ends[h] + m_hs[h], jnp.max(wlogs[h], axis=0, keepdims=True)) for h in heads]
    dcs = [jnp.exp(b_ends[h] + m_hs[h] - m_news[h]) for h in heads]
    kwss = [ks[h] * jnp.exp(wlogs[h] - m_news[h]) for h in heads]
    upds = [_dot_tn(kwss[h], vbs[h]) for h in heads]
    for h in heads:
        cst[cols[h], :] = dcs[h] * c_hs[h] + upds[h]
        nst[:, cols[h]] = dcs[h] * n_hs[h] + jnp.sum(kwss[h], axis=0, keepdims=True)
        m_all = jnp.where(lane1 == h, m_news[h], m_all)
    mst[...] = m_all

    hm = _group_rms(jnp.concatenate(hs, axis=1), hd) * nw_ref[...]
    h_ref[...] = ((hm + skip_ref[...] * act) * _sigmoid(o_ref[...].astype(F32))).astype(h_ref.dtype)

    @pl.when(c == pl.num_programs(1) - 1)
    def _():
        c_ref[...] = cst[...]
        n_ref[...] = nst[...]
        m_ref[...] = mst[...]


def _rwkv_kernel(r_ref, lw_ref, k_ref, v_ref, kk_ref, a_ref, s0_ref, rk_ref, lnw_ref, lnb_ref,
                 y_ref, s_ref, sblk):
    c = pl.program_id(1)
    rows = r_ref.shape[0]

    @pl.when(c == 0)
    def _():
        s0 = s0_ref[...]
        shape = (RWKV_D, 2 * RWKV_HEAD_DIM)
        head_parity = (lax.broadcasted_iota(jnp.int32, shape, 0) // RWKV_HEAD_DIM) % 2
        lane_half = lax.broadcasted_iota(jnp.int32, shape, 1) // RWKV_HEAD_DIM
        sblk[...] = jnp.where(head_parity == lane_half, jnp.concatenate([s0, s0], axis=1), 0.0)

    tril, strict = _tri_masks(rows)
    tri = jnp.where(tril, 1.0, 0.0).astype(BF16)
    npairs = RWKV_HEADS // 2
    lanes = [slice(p * LANES, (p + 1) * LANES) for p in range(npairs)]
    ld = lambda ref, sl: ref[:, sl].astype(F32)
    cums = [_tri_cumsum(tri, ld(lw_ref, sl), terms=2) for sl in lanes]
    lasts = [c_[rows - 1:rows, :] for c_ in cums]
    einvs = [jnp.exp(-c_) for c_ in cums]
    eends = [jnp.exp(l_ - c_) for l_, c_ in zip(lasts, cums)]
    kas = [ld(kk_ref, sl) * ld(a_ref, sl) for sl in lanes]
    ats = [-(ld(kk_ref, sl) * jnp.exp(c_ - ld(lw_ref, sl))) for sl, c_ in zip(lanes, cums)]
    rts = [ld(r_ref, sl) * jnp.exp(c_) for sl, c_ in zip(lanes, cums)]
    bts = [x * e for x, e in zip(kas, einvs)]
    kts = [ld(k_ref, sl) * e for sl, e in zip(lanes, einvs)]
    bhs = [x * e for x, e in zip(kas, eends)]
    khs = [ld(k_ref, sl) * e for sl, e in zip(lanes, eends)]
    wls = [jnp.exp(l_) for l_ in lasts]
    half = lax.broadcasted_iota(jnp.int32, (rows, LANES), 1) < RWKV_HEAD_DIM
    steps = rows.bit_length() - 1

    def two(x):
        zero = jnp.zeros_like(x)
        return [jnp.where(half, x, zero), jnp.where(half, zero, x)]

    tril4 = jnp.concatenate([tril] * 4, axis=1)
    strict4 = jnp.concatenate([strict] * 4, axis=1)

    a2s = [jnp.concatenate([x.astype(BF16), y_.astype(BF16)], axis=0) for x, y_ in zip(ats, rts)]
    rbs = [jnp.concatenate(two(x.astype(BF16)) + two(y_.astype(BF16)), axis=0) for x, y_ in zip(bts, kts)]
    sps = [sblk[sl, :] for sl in lanes]
    vss = [two(v_ref[:, sl].astype(BF16)) for sl in lanes]
    pbks = [_dot_nt(a2, rb) for a2, rb in zip(a2s, rbs)]
    xss = [_dot_nt(a2, sp) for a2, sp in zip(a2s, sps)]
    tops = [jnp.where(strict4, pbk[0:rows, :], 0.0) for pbk in pbks]
    us = [xs[0:rows, :] + _dot(top[:, 2 * rows:4 * rows], jnp.concatenate(vs, axis=0))
          for xs, top, vs in zip(xss, tops, vss)]
    pws = [[top[:, e * rows:(e + 1) * rows].astype(BF16) for e in range(2)] for top in tops]
    ybase = [xs[rows:2 * rows, :] for xs in xss]
    w4s = [jnp.where(tril4, pbk[rows:2 * rows, :], 0.0).astype(BF16) for pbk in pbks]

    for j in range(steps):
        for p in range(npairs):
            u2 = jnp.concatenate(two(us[p].astype(BF16)), axis=0)
            us[p] = us[p] + _dot(jnp.concatenate(pws[p], axis=1), u2)
        if j < steps - 1:
            for p in range(npairs):
                pws[p] = [_dot(pw, pw).astype(BF16) for pw in pws[p]]

    ys = []
    for p, sl in enumerate(lanes):
        r4 = jnp.concatenate(two(us[p].astype(BF16)) + vss[p], axis=0)
        ys.append(ybase[p] + _dot(w4s[p], r4))
        rh = jnp.concatenate(two(bhs[p].astype(BF16)) + two(khs[p].astype(BF16)), axis=0)
        sblk[sl, :] = sps[p] * wls[p] + _dot_tn(r4, rh)
    y = jnp.concatenate(ys, axis=1)
    r, k, v = (t[...].astype(F32) for t in (r_ref, k_ref, v_ref))
    y_ref[...] = _rwkv_head_norm(y, r, k, v, rk_ref, lnw_ref, lnb_ref).astype(y_ref.dtype)

    @pl.when(c == pl.num_programs(1) - 1)
    def _():
        sb = sblk[...]
        s_ref[...] = sb[:, 0:RWKV_HEAD_DIM] + sb[:, RWKV_HEAD_DIM:2 * RWKV_HEAD_DIM]


def _ssd_step_kernel(xbc_ref, t0_ref, t1_ref, t2_ref, z_ref, sm_ref, st_ref, cw_ref, cb_ref, bias_ref,
                     alog_ref, dexp_ref, nw_ref, segt_ref, y_ref, sto_ref):
    nseq = xbc_ref.shape[0]
    act = _silu(_step_conv(xbc_ref, (t0_ref, t1_ref, t2_ref), cw_ref, cb_ref))
    xs = act[:, 0:SSD_D_INNER]
    gn = SSD_GROUPS * SSD_D_STATE
    bm = act[:, SSD_D_INNER:SSD_D_INNER + gn]
    cm = act[:, SSD_D_INNER + gn:SSD_D_INNER + 2 * gn]
    lane = lax.broadcasted_iota(jnp.int32, (nseq, LANES), 1)
    head_lane = (lane >= SMALL_DT) & (lane < SMALL_DT + SSD_HEADS)
    dt = jnp.where(head_lane, _softplus(sm_ref[...] + bias_ref[...]), 0.0)
    segt = segt_ref[...]
    dec_t = _dot_exact_rhs(jnp.exp(dt * (-jnp.exp(alog_ref[...]))), segt).T
    xdt = xs * _dot_exact_rhs(dt, segt)
    gw = SSD_GROUP_W
    rowi = lax.broadcasted_iota(jnp.int32, (nseq, gw), 0)
    ys = [jnp.zeros((nseq, gw), F32) for _ in range(SSD_GROUPS)]
    for s in range(nseq):
        for g in range(SSD_GROUPS):
            rs = slice(g * gw, (g + 1) * gw)
            ns = slice(g * SSD_D_STATE, (g + 1) * SSD_D_STATE)
            outer = _dot_tn(jnp.where(rowi == s, xdt[:, rs], 0.0), bm[:, ns])
            new = st_ref[s, rs, :] * dec_t[rs, s:s + 1] + outer
            sto_ref[s, rs, :] = new
            ys[g] = jnp.where(rowi == s, _dot_nt(cm[:, ns], new), ys[g])
    y = jnp.concatenate(ys, axis=1) + dexp_ref[...] * xs
    y_ref[...] = _group_rms(y * _silu(z_ref[...]), gw) * nw_ref[...]


def _mlstm_step_kernel(xm_ref, t0_ref, t1_ref, t2_ref, o_ref, sm_ref, n_ref, m_ref, c_ref, cw_ref, cb_ref,
                       bias_ref, wq_ref, wk_ref, wv_ref, nw_ref, skip_ref, h_ref, no_ref, mo_ref, co_ref):
    nseq = xm_ref.shape[0]
    raw = xm_ref[...]
    act = _silu(_step_conv(xm_ref, (t0_ref, t1_ref, t2_ref), cw_ref, cb_ref))
    q, k, v = _blockdiag_qkv(act, raw, wq_ref, wk_ref, wv_ref)
    logi = sm_ref[...] + bias_ref[...]
    logf = _log_sigmoid(logi)
    m_old = m_ref[...]
    m_all = m_old
    n_old = n_ref[...]
    lane = lax.broadcasted_iota(jnp.int32, (nseq, LANES), 1)
    hd = MLSTM_HEAD_DIM
    rowi = lax.broadcasted_iota(jnp.int32, (nseq, hd), 0)
    hs, ns = [], []
    for h in range(MLSTM_HEADS):
        cols = slice(h * hd, (h + 1) * hd)
        qh, kh, vh = q[:, cols], k[:, cols], v[:, cols]
        lf = logf[:, SMALL_F + h:SMALL_F + h + 1] + m_old[:, h:h + 1]
        li = logi[:, SMALL_I + h:SMALL_I + h + 1]
        m_new = jnp.maximum(lf, li)
        dc = jnp.exp(lf - m_new)
        kws = kh * jnp.exp(li - m_new)
        n_new = dc * n_old[:, cols] + kws
        den = jnp.sum(qh * n_new, axis=1, keepdims=True)
        q_t = qh.T
        num = jnp.zeros((nseq, hd), F32)
        for s in range(nseq):
            outer = _dot_tn(jnp.where(rowi == s, kws, 0.0), vh)
            c_new = dc[s:s + 1, :] * c_ref[s, cols, :] + outer
            co_ref[s, cols, :] = c_new
            num = jnp.where(rowi == s, jnp.sum(q_t[:, s:s + 1] * c_new, axis=0, keepdims=True), num)
        hs.append(num / jnp.maximum(jnp.abs(den), jnp.exp(-m_new)))
        ns.append(n_new)
        m_all = jnp.where(lane == h, m_new, m_all)
    no_ref[...] = jnp.concatenate(ns, axis=1)
    mo_ref[...] = m_all
    hm = _group_rms(jnp.concatenate(hs, axis=1), hd) * nw_ref[...]
    h_ref[...] = (hm + skip_ref[...] * act) * _sigmoid(o_ref[...])


def _rwkv_step_kernel(r_ref, lw_ref, k_ref, v_ref, kk_ref, a_ref, s_ref, rk_ref, lnw_ref, lnb_ref,
                      y_ref, so_ref, yt):
    r, k, v, kk = (t[...] for t in (r_ref, k_ref, v_ref, kk_ref))
    n = RWKV_HEAD_DIM
    w_t, kk_t, ka_t, k_t, r_t, v_t = (x.T for x in (jnp.exp(lw_ref[...]), kk, kk * a_ref[...], k, r, v))
    for e in range(2):
        hs = slice(e * n, (e + 1) * n)
        w_h, kk_h, ka_h, k_h, r_h = w_t[hs], kk_t[hs], ka_t[hs], k_t[hs], r_t[hs]
        for i in range(n):
            row = e * n + i
            st = s_ref[e, i]
            sa = jnp.sum(st * kk_h, axis=0, keepdims=True)
            new = st * w_h - sa * ka_h + v_t[row:row + 1, :] * k_h
            so_ref[e, i] = new
            yt[row:row + 1, :] = jnp.sum(new * r_h, axis=0, keepdims=True)
    y_ref[...] = _rwkv_head_norm(yt[...].T, r, k, v, rk_ref, lnw_ref, lnb_ref)


def _params(sem):
    return pltpu.CompilerParams(dimension_semantics=sem, vmem_limit_bytes=VMEM_LIMIT)


def _call_name(kernel, n):
    fn = getattr(kernel, "func", kernel)
    return f"{fn.__name__.strip('_')}_{n}"


def _const_spec(shape, grid_rank):
    nd = len(shape)
    if grid_rank == 1:
        return pl.BlockSpec(shape, lambda i: (0,) * nd, pipeline_mode=pl.Buffered(1))
    return pl.BlockSpec(shape, lambda b, c: (0,) * nd, pipeline_mode=pl.Buffered(1))


def _row_tile(m, tile):
    return tile if m % tile == 0 else m


def _layer_spec(stacked, layer):
    nd = stacked.ndim
    return pl.BlockSpec((None,) + stacked.shape[1:], lambda i: (layer,) + (0,) * (nd - 1),
                        pipeline_mode=pl.Buffered(1))


def _rowwise_call(kernel, row_ins, const_ins, outs, m, tile=ROW_TILE, extra_ins=()):
    tm = _row_tile(m, tile)
    rows = lambda n: pl.BlockSpec((tm, n), lambda i: (i, 0))
    consts = [c if isinstance(c, tuple) else (c, None) for c in const_ins]
    return pl.pallas_call(
        kernel,
        name=_call_name(kernel, m),
        grid=(m // tm,),
        in_specs=([rows(a.shape[1]) for a in row_ins] + [spec for _, spec in extra_ins]
                  + [_const_spec(a.shape, 1) if layer is None else _layer_spec(a, layer) for a, layer in consts]),
        out_specs=[rows(n) for n, _ in outs],
        out_shape=[jax.ShapeDtypeStruct((m, n), dt) for n, dt in outs],
        compiler_params=_params(("parallel",)),
    )(*row_ins, *[a for a, _ in extra_ins], *[a for a, _ in consts])


def _scan_call(kernel, seq_ins, state_ins, const_ins, seq_out_widths, state_out_shapes, scratch):
    bsz, t = seq_ins[0].shape[:2]
    seq = lambda n: pl.BlockSpec((None, SCAN_CHUNK, n), lambda b, c: (b, c, 0))
    per_seq = lambda shape: pl.BlockSpec((None,) + tuple(shape[1:]), lambda b, c: (b,) + (0,) * (len(shape) - 1))
    return pl.pallas_call(
        kernel,
        name=_call_name(kernel, bsz),
        grid=(bsz, t // SCAN_CHUNK),
        in_specs=([seq(a.shape[2]) for a in seq_ins] + [per_seq(a.shape) for a in state_ins]
                  + [_const_spec(a.shape, 2) for a in const_ins]),
        out_specs=[seq(n) for n in seq_out_widths] + [per_seq(s) for s in state_out_shapes],
        out_shape=([jax.ShapeDtypeStruct((bsz, t, n), seq_ins[0].dtype) for n in seq_out_widths]
                   + [jax.ShapeDtypeStruct(s, F32) for s in state_out_shapes]),
        scratch_shapes=[pltpu.VMEM(shape, dt) for shape, dt in scratch],
        compiler_params=_params(("parallel", "arbitrary")),
    )(*seq_ins, *state_ins, *const_ins)


def _state_ring(kernel, n_rows, nsteps):
    def body(*refs):
        *main, ring, sems = refs
        hbm = main[n_rows]
        i = pl.program_id(0)

        def copy(j):
            slot = j % STATE_SLOTS
            return pltpu.make_async_copy(hbm.at[pl.ds(j * STEP_SEQS, STEP_SEQS)], ring.at[slot], sems.at[slot])

        @pl.when(i == 0)
        def _():
            for j in range(min(STATE_SLOTS - 1, nsteps)):
                copy(j).start()

        @pl.when(i + STATE_SLOTS - 1 < nsteps)
        def _():
            copy(i + STATE_SLOTS - 1).start()

        copy(i).wait()
        kernel(*main[:n_rows], ring.at[i % STATE_SLOTS], *main[n_rows + 1:])
    return body


def _step_call(kernel, row_ins, state_in, const_ins, row_out_widths, state_out_shapes):
    bsz = row_ins[0].shape[0]
    nsteps = bsz // STEP_SEQS
    rows = lambda n: pl.BlockSpec((STEP_SEQS, n), lambda i: (i, 0))
    state = lambda shape: pl.BlockSpec((STEP_SEQS,) + tuple(shape[1:]), lambda i: (i, 0, 0))
    return pl.pallas_call(
        _state_ring(kernel, len(row_ins), nsteps),
        name=_call_name(kernel, bsz),
        grid=(nsteps,),
        in_specs=([rows(a.shape[1]) for a in row_ins] + [pl.BlockSpec(memory_space=pl.ANY)]
                  + [_const_spec(a.shape, 1) for a in const_ins]),
        out_specs=[rows(n) for n in row_out_widths] + [state(s) for s in state_out_shapes],
        out_shape=([jax.ShapeDtypeStruct((bsz, n), F32) for n in row_out_widths]
                   + [jax.ShapeDtypeStruct(s, F32) for s in state_out_shapes]),
        scratch_shapes=[pltpu.VMEM((STATE_SLOTS, STEP_SEQS) + tuple(state_in.shape[1:]), F32),
                        pltpu.SemaphoreType.DMA((STATE_SLOTS,))],
        compiler_params=_params(("arbitrary",)),
    )(*row_ins, state_in, *const_ins)


def _rwkv_step_call(rows, wkv, rk, lnw, lnb):
    bsz = wkv.shape[0]
    assert bsz == LANES, "the single-token RWKV kernel keeps the sequences on the lanes"
    n = RWKV_HEAD_DIM
    pair = pl.BlockSpec((bsz, LANES), lambda i: (0, i))
    vec = pl.BlockSpec((1, LANES), lambda i: (0, i))
    state = pl.BlockSpec((2, n, n, bsz), lambda i: (i, 0, 0, 0))
    y, new = pl.pallas_call(
        _rwkv_step_kernel,
        name=_call_name(_rwkv_step_kernel, bsz),
        grid=(RWKV_HEADS // 2,),
        in_specs=[pair] * 6 + [state] + [vec] * 3,
        out_specs=[pair, state],
        out_shape=[jax.ShapeDtypeStruct((bsz, RWKV_D), F32), jax.ShapeDtypeStruct((RWKV_HEADS, n, n, bsz), F32)],
        scratch_shapes=[pltpu.VMEM((LANES, bsz), F32)],
        compiler_params=_params(("parallel",)),
    )(*rows, jnp.transpose(wkv, (1, 2, 3, 0)), rk, lnw, lnb)
    return y, jnp.transpose(new, (3, 0, 1, 2))


def _row(v):
    return v.reshape(1, -1).astype(F32)


def _pad_cols(w, width):
    return jnp.pad(w, ((0, 0), (0, width - w.shape[1])))


def _pad_rows_to(w, rows):
    return jnp.pad(w, ((0, rows - w.shape[0]), (0, 0)))


def _blockdiag_slabs(w, scale=1.0):
    hd = MLSTM_HEAD_DIM
    rows = (w * scale).reshape(-1, hd, QKV_BLOCK)
    col = jnp.arange(hd)
    pick = (col[None, :] % QKV_BLOCK == jnp.arange(QKV_BLOCK)[:, None]).astype(w.dtype)
    tiled = jnp.einsum('srd,dc->src', rows, pick, precision=lax.Precision.HIGHEST)
    same_block = col[:, None] // QKV_BLOCK == col[None, :] // QKV_BLOCK
    return jnp.where(same_block, tiled, 0.0).astype(BF16)


def _head_spread():
    ch = jnp.arange(SSD_D_INNER) // SSD_HEAD_DIM
    return (jnp.arange(LANES)[:, None] == ch[None, :]).astype(BF16)


def _mixer0_consts(p):
    ssd = [p['conv_w'][:, :XBC_DIM], p['conv_b'][:, :XBC_DIM], p['small_bias'], p['ssd_a_log'], p['ssd_d'],
           p['ssd_norm'], p['segt']]
    ml = [p['conv_w'][:, XBC_DIM:], p['conv_b'][:, XBC_DIM:], p['small_bias'], p['ml_wq'], p['ml_wk'],
          p['ml_wv'], p['ml_norm'], p['ml_skip']]
    return ssd, ml


def _mixer0_chunked(xbc, xm, z, o_pre, small, conv, ssm, mc, mn, mm, p, bsz, t):
    seq = lambda a: a.reshape(bsz, t, -1)
    tail = jnp.pad(conv, ((0, 0), (SUBLANES - (CONV_W - 1), 0), (0, 0)))
    ssd_c, ml_c = _mixer0_consts(p)
    y_ssd, new_ssm = _scan_call(
        _ssd_kernel, [seq(xbc), seq(z), seq(small)],
        [tail[:, :, :XBC_DIM], ssm.reshape(bsz, SSD_D_INNER, SSD_D_STATE)], ssd_c,
        [SSD_D_INNER], [(bsz, SSD_D_INNER, SSD_D_STATE)],
        [((SUBLANES, XBC_DIM), F32), ((SSD_D_INNER, SSD_D_STATE), F32)])
    hm, new_c, new_n, new_m = _scan_call(
        _mlstm_kernel, [seq(xm), seq(o_pre), seq(small)],
        [tail[:, :, XBC_DIM:], mc.reshape(bsz, MLSTM_D_INNER, MLSTM_HEAD_DIM), mn.reshape(bsz, 1, MLSTM_D_INNER),
         _pad_cols(mm, LANES).reshape(bsz, 1, LANES)], ml_c,
        [MLSTM_D_INNER], [(bsz, MLSTM_D_INNER, MLSTM_HEAD_DIM), (bsz, 1, MLSTM_D_INNER), (bsz, 1, LANES)],
        [((SUBLANES, MLSTM_D_INNER), F32), ((MLSTM_D_INNER, MLSTM_HEAD_DIM), F32),
         ((1, MLSTM_D_INNER), F32), ((1, LANES), F32)])
    flat = lambda a: a.reshape(bsz * t, -1)
    return flat(y_ssd), flat(hm), new_ssm, new_c, new_n, new_m.reshape(bsz, LANES)


def _mixer0_step(xbc, xm, z, o_pre, small, conv, ssm, mc, mn, mm, p, bsz):
    ssd_c, ml_c = _mixer0_consts(p)
    tails = [conv[:, j, :] for j in range(CONV_W - 1)]
    y_ssd, new_ssm = _step_call(
        _ssd_step_kernel, [xbc] + [tl[:, :XBC_DIM] for tl in tails] + [z, small],
        ssm.reshape(bsz, SSD_D_INNER, SSD_D_STATE), ssd_c, [SSD_D_INNER], [(bsz, SSD_D_INNER, SSD_D_STATE)])
    hm, new_n, new_m, new_c = _step_call(
        _mlstm_step_kernel, [xm] + [tl[:, XBC_DIM:] for tl in tails]
        + [o_pre, small, mn.reshape(bsz, MLSTM_D_INNER), _pad_cols(mm, LANES)],
        mc.reshape(bsz, MLSTM_D_INNER, MLSTM_HEAD_DIM), ml_c,
        [MLSTM_D_INNER, MLSTM_D_INNER, LANES], [(bsz, MLSTM_D_INNER, MLSTM_HEAD_DIM)])
    return y_ssd, hm, new_ssm, new_c, new_n, new_m


def _trunk(x, conv, ssm, mc, mn, mm, shift, wkv, p):
    bsz, t, d = x.shape
    m = bsz * t
    step = t == 1
    assert step or t % SCAN_CHUNK == 0, "a group is either single-token or a multiple of the scan chunk"
    assert not step or bsz % STEP_SEQS == 0
    act = F32 if step else BF16
    x2 = x.reshape(m, d)

    z, xbc, xm, o_pre, small = _rowwise_call(
        _in0_kernel, [x2], [p['norm_mix0'], p['w_in0']],
        [(SSD_D_INNER, act), (XBC_DIM, act), (MLSTM_D_INNER, act), (MLSTM_D_INNER, act), (LANES, F32)], m,
        tile=ROW_TILE)
    last_rows = lambda a: a.reshape(bsz, t, -1)[:, -(CONV_W - 1):].astype(F32)
    conv_in_tail = jnp.concatenate([last_rows(xbc), last_rows(xm)], axis=-1)
    new_conv = jnp.concatenate([conv, conv_in_tail], axis=1)[:, -(CONV_W - 1):]
    if step:
        y_ssd, hm, new_ssm, new_c, new_n, new_m = _mixer0_step(xbc, xm, z, o_pre, small, conv, ssm, mc, mn, mm, p, bsz)
    else:
        y_ssd, hm, new_ssm, new_c, new_n, new_m = _mixer0_chunked(xbc, xm, z, o_pre, small, conv, ssm, mc, mn, mm,
                                                                  p, bsz, t)
    xn1, x2b = _rowwise_call(
        _mix0_ffn_kernel, [y_ssd, hm, x2],
        [p['w_out0'], p['norm_ffn0'], (p['ffn_gu'], 0), (p['ffn_d'], 0), p['norm_mix1']], [(d, act), (d, F32)], m,
        tile=ROW_TILE)

    rw_in_c = [p['rw_mu'], p['rw_wr'], p['rw_wk'], p['rw_wv'], p['rw_w1'], p['rw_w2'], p['rw_a1'], p['rw_a2'],
               p['rw_g1'], p['rw_g2'], p['rw_w0'], p['rw_a0'], p['rw_k_k'], p['rw_k_a']]
    rw_outs = [(d, act), (d, F32)] + [(d, act)] * 5
    if step:
        rw = _rowwise_call(_rw_in_kernel, [xn1, shift], rw_in_c, rw_outs, m)
    else:
        tm = _row_tile(t, ROW_TILE)
        tiles_per_seq = t // tm
        prev_spec = pl.BlockSpec((PREV_ROWS, d), lambda i: (jnp.maximum(i * (tm // PREV_ROWS) - 1, 0), 0))
        shift_spec = pl.BlockSpec((None, 1, d), lambda i: (i // tiles_per_seq, 0, 0))
        rw = _rowwise_call(functools.partial(_rw_in_shift_kernel, tiles_per_seq=tiles_per_seq), [xn1], rw_in_c,
                           rw_outs, m, tile=tm, extra_ins=[(xn1, prev_spec), (shift.reshape(bsz, 1, d), shift_spec)])
    g = rw[6]
    rw_c = [p['rw_r_k'], p['rw_ln_w'], p['rw_ln_b']]
    if step:
        y_rw, new_wkv = _rwkv_step_call(list(rw[:6]), wkv, *rw_c)
    else:
        wkv3 = wkv.reshape(bsz, RWKV_D, RWKV_HEAD_DIM)
        y_rw, new_wkv = _scan_call(
            _rwkv_kernel, [a.reshape(bsz, t, d) for a in rw[:6]], [wkv3], rw_c,
            [d], [wkv3.shape], [((RWKV_D, 2 * RWKV_HEAD_DIM), F32)])
        y_rw = y_rw.reshape(m, d)
    new_wkv = new_wkv.reshape(wkv.shape)
    (y,) = _rowwise_call(
        _mix1_ffn_kernel, [y_rw, g, x2b],
        [p['rw_wo'], p['norm_ffn1'], (p['ffn_gu'], 1), (p['ffn_d'], 1), p['norm_final']], [(d, F32)], m, tile=ROW_TILE)

    states = (new_conv, new_ssm.reshape(ssm.shape), new_c.reshape(mc.shape), new_n.reshape(mn.shape),
              new_m[:, :MLSTM_HEADS], xn1.reshape(bsz, t, d)[:, -1].astype(F32), new_wkv)
    return y.reshape(bsz, t, d), states


def kernel(x_prompt, x_sample, state_conv, state_ssm, state_mlstm_c, state_mlstm_n, state_mlstm_m, state_shift, state_wkv, norm_mix, norm_ffn, norm_final, w_in0, conv_w, conv_b, ssd_dt_bias, ssd_a_log, ssd_d, ssd_norm, ml_wq, ml_wk, ml_wv, ml_i_bias, ml_f_bias, ml_norm, ml_skip, w_out0, rw_mu, rw_wr, rw_wk, rw_wv, rw_wo, rw_w0, rw_w1, rw_w2, rw_a0, rw_a1, rw_a2, rw_g1, rw_g2, rw_k_k, rw_k_a, rw_r_k, rw_ln_w, rw_ln_b, ffn_w_gate_up, ffn_w_down):
    assert norm_mix.shape[0] == 2 and w_in0.shape[0] == 1 and rw_wr.shape[0] == 1, "two layers: SSD|mLSTM then RWKV-7"
    s1 = SSD_D_INNER
    s2 = s1 + XBC_DIM + MLSTM_D_INNER
    s3 = s2 + SSD_HEADS
    s4 = s3 + MLSTM_D_INNER
    s5 = s4 + MLSTM_HEADS
    w0 = w_in0[0]
    small_w = _pad_cols(jnp.concatenate([w0[:, s2:s3], w0[:, s4:s5], w0[:, s5:]], axis=1), LANES)
    lora = lambda w1, w2, width: (_pad_cols(w1, width).astype(BF16), _pad_rows_to(w2, width).astype(BF16))
    rw_w1p, rw_w2p = lora(rw_w1[0], rw_w2[0], LANES)
    rw_a1p, rw_a2p = lora(rw_a1[0], rw_a2[0], LANES)
    rw_g1p, rw_g2p = lora(rw_g1[0], rw_g2[0], 2 * LANES)
    p = dict(
        norm_mix0=_row(norm_mix[0]), norm_mix1=_row(norm_mix[1]), norm_ffn0=_row(norm_ffn[0]),
        norm_ffn1=_row(norm_ffn[1]), norm_final=_row(norm_final),
        w_in0=jnp.concatenate([w0[:, :s2], w0[:, s3:s4], small_w], axis=1).astype(BF16),
        conv_w=conv_w[0], conv_b=_row(conv_b[0]),
        small_bias=_pad_cols(jnp.concatenate([_row(ssd_dt_bias[0]), _row(ml_i_bias[0]), _row(ml_f_bias[0])], axis=1),
                             LANES),
        ssd_a_log=_pad_cols(_row(ssd_a_log[0]), LANES), ssd_d=_row(jnp.repeat(ssd_d[0], SSD_HEAD_DIM)),
        ssd_norm=_row(ssd_norm[0]),
        ml_wq=_blockdiag_slabs(ml_wq[0]), ml_wk=_blockdiag_slabs(ml_wk[0], MLSTM_HEAD_DIM ** -0.5),
        ml_wv=_blockdiag_slabs(ml_wv[0]), ml_norm=_row(ml_norm[0]), ml_skip=_row(ml_skip[0]),
        w_out0=w_out0[0].astype(BF16),
        rw_mu=_pad_rows_to(rw_mu[0], SUBLANES),
        rw_wr=rw_wr[0].astype(BF16), rw_wk=rw_wk[0].astype(BF16), rw_wv=rw_wv[0].astype(BF16),
        rw_wo=rw_wo[0].astype(BF16), rw_w0=_row(rw_w0[0]), rw_a0=_row(rw_a0[0]),
        rw_w1=rw_w1p, rw_w2=rw_w2p, rw_a1=rw_a1p, rw_a2=rw_a2p, rw_g1=rw_g1p, rw_g2=rw_g2p,
        rw_k_k=_row(rw_k_k[0]), rw_k_a=_row(rw_k_a[0]), rw_r_k=_row(rw_r_k[0]),
        rw_ln_w=_row(rw_ln_w[0]), rw_ln_b=_row(rw_ln_b[0]),
        ffn_gu=ffn_w_gate_up.astype(BF16), ffn_d=ffn_w_down.astype(BF16),
        segt=_head_spread(),
    )
    bp = x_prompt.shape[0]
    zeros = lambda s: jnp.zeros((bp,) + s.shape[2:], F32)
    y_p, st_p = _trunk(x_prompt, zeros(state_conv), zeros(state_ssm), zeros(state_mlstm_c),
                       zeros(state_mlstm_n), zeros(state_mlstm_m), zeros(state_shift), zeros(state_wkv), p)
    y_s, st_s = _trunk(x_sample, state_conv[0], state_ssm[0], state_mlstm_c[0], state_mlstm_n[0],
                       state_mlstm_m[0], state_shift[0], state_wkv[0], p)
    out = [y_p, y_s]
    for a, b in zip(st_p, st_s):
        out += [a[None], b[None]]
    return tuple(out)
```
